```python
import math
import jax, jax.numpy as jnp
from jax import lax
import numpy as np

D_MODEL = 1024
BATCH = 4
SEQ = 4096
DEPTH = 4

ATT_Q_HEADS = 8
ATT_KV_HEADS = 2
ATT_HEAD_DIM = 64
WINDOW = 128
ATT_BLOCK = 128
REL_BUCKETS = 32
REL_MAX_DIST = 128
HG_HEADS = 4
HG_DK = 128
HG_DV = 128
HG_CHUNK = 16
ATT_Q_W = ATT_Q_HEADS * ATT_HEAD_DIM
ATT_KV_W = ATT_KV_HEADS * ATT_HEAD_DIM
HG_K_W = HG_HEADS * HG_DK
HG_V_W = HG_HEADS * HG_DV
IN_SPLITS = (ATT_Q_W, ATT_KV_W, ATT_KV_W, HG_K_W, HG_K_W, HG_V_W, HG_V_W, D_MODEL, D_MODEL)
IN_WIDTH = sum(IN_SPLITS)
FFN_DENSE = 2816
N_EXPERTS = 8
TOP_K = 2
FFN_EXPERT = 3584
N_DENSE = (DEPTH + 1) // 2
N_MOE = DEPTH // 2
PLE_DIM = 256
EPS = 1e-6

kernel_name = 'hybrid_swa_hgrn2_moe_ple_trunk'


def _rmsnorm(x, g):
    xf = x.astype(jnp.float32)
    y = xf * lax.rsqrt(jnp.mean(xf * xf, axis=-1, keepdims=True) + EPS)
    return (y * g.astype(jnp.float32)).astype(x.dtype)


def _t5_bucket(dist):
    max_exact = REL_BUCKETS // 2
    d = jnp.maximum(dist, 0)
    large = max_exact + (jnp.log(jnp.maximum(d, 1).astype(jnp.float32) / max_exact)
                         / math.log(REL_MAX_DIST / max_exact)
                         * (REL_BUCKETS - max_exact)).astype(jnp.int32)
    large = jnp.minimum(large, REL_BUCKETS - 1)
    return jnp.where(d < max_exact, d, large)


def _split(z):
    outs = []
    off = 0
    for w in IN_SPLITS:
        outs.append(z[..., off:off + w])
        off += w
    return outs


def _sliding_window_attention(q, k, v, sinks, band_bias):
    B, S = q.shape[0], q.shape[1]
    nb = S // ATT_BLOCK
    G = ATT_Q_HEADS // ATT_KV_HEADS
    qb = q.reshape(B, nb, ATT_BLOCK, ATT_KV_HEADS, G, ATT_HEAD_DIM)

    def band(t):
        tp = jnp.pad(t, ((0, 0), (ATT_BLOCK, 0), (0, 0), (0, 0)))
        tb = tp.reshape(B, nb + 1, ATT_BLOCK, ATT_KV_HEADS, ATT_HEAD_DIM)
        return jnp.concatenate([tb[:, :-1], tb[:, 1:]], axis=2)

    kb, vb = band(k), band(v)
    s = jnp.einsum('bnqhgd,bnkhd->bnhgqk', qb, kb).astype(jnp.float32) * (ATT_HEAD_DIM ** -0.5)
    s = s + band_bias.astype(jnp.float32).reshape(ATT_KV_HEADS, G, ATT_BLOCK, 2 * ATT_BLOCK)
    qi = jnp.arange(ATT_BLOCK)[:, None]
    kj = jnp.arange(2 * ATT_BLOCK)[None, :]
    dist = qi + ATT_BLOCK - kj
    kpos = jnp.arange(nb)[:, None, None] * ATT_BLOCK - ATT_BLOCK + kj
    valid = (dist >= 0) & (dist < WINDOW) & (kpos >= 0)
    s = jnp.where(valid[None, :, None, None], s, -1e30)
    sink = sinks.astype(jnp.float32).reshape(ATT_KV_HEADS, G)[None, None, :, :, None, None]
    m = jnp.maximum(jnp.max(s, axis=-1, keepdims=True), sink)
    pr = jnp.exp(s - m)
    denom = jnp.sum(pr, axis=-1, keepdims=True) + jnp.exp(sink - m)
    pr = (pr / denom).astype(v.dtype)
    o = jnp.einsum('bnhgqk,bnkhd->bnqhgd', pr, vb)
    return o.reshape(B, S, ATT_Q_W)


def _hgrn2(q, f_logit, v, g, lb, norm_g):
    B, S = q.shape[0], q.shape[1]
    nc = S // HG_CHUNK
    dt = q.dtype
    lbh = lb.astype(jnp.float32).reshape(HG_HEADS, HG_DK)
    f = lbh + (1.0 - lbh) * jax.nn.sigmoid(f_logit.astype(jnp.float32))
    log_f = jnp.log(f)
    k = 1.0 - f

    def chunk(t):
        return t.reshape(B, nc, HG_CHUNK, HG_HEADS, t.shape[-1])

    qc, kc, vc = chunk(q.astype(jnp.float32)), chunk(k), chunk(v.astype(jnp.float32))
    A = jnp.cumsum(chunk(log_f), axis=2)
    A_last = A[:, :, -1:]
    q_dec = qc * jnp.exp(A)
    k_in = kc * jnp.exp(-A)
    k_end = kc * jnp.exp(A_last - A)
    causal = jnp.tril(jnp.ones((HG_CHUNK, HG_CHUNK), dtype=bool))
    P = jnp.where(causal, jnp.einsum('bnthd,bnshd->bnhts', q_dec, k_in), 0.0)
    o_intra = jnp.einsum('bnhts,bnshe->bnthe', P, vc)
    decay = jnp.exp(A_last[:, :, 0])

    def step(state, inp):
        qd, ke, vv, a = inp
        o_int = jnp.einsum('bthd,bhde->bthe', qd, state)
        state = a[..., None] * state + jnp.einsum('bshd,bshe->bhde', ke, vv)
        return state, o_int

    s0 = jnp.zeros((B, HG_HEADS, HG_DK, HG_DV), jnp.float32)
    xs = (jnp.moveaxis(q_dec, 1, 0), jnp.moveaxis(k_end, 1, 0),
          jnp.moveaxis(vc, 1, 0), jnp.moveaxis(decay, 1, 0))
    _, o_inter = lax.scan(step, s0, xs)
    o = (o_intra + jnp.moveaxis(o_inter, 0, 1)).reshape(B, S, HG_HEADS, HG_DV)
    o = o * lax.rsqrt(jnp.mean(o * o, axis=-1, keepdims=True) + EPS) * norm_g.astype(jnp.float32)
    o = o.reshape(B, S, HG_V_W) * jax.nn.silu(g.astype(jnp.float32))
    return o.astype(dt)


def _swiglu(u, w1, w3, w2):
    return (jax.nn.silu(u @ w1) * (u @ w3)) @ w2


def _moe(u, router_w, w1, w3, w2):
    logits = (u @ router_w).astype(jnp.float32)
    top_val, top_idx = lax.top_k(logits, TOP_K)
    wts = jax.nn.softmax(top_val, axis=-1)
    gate = jnp.sum(jax.nn.one_hot(top_idx, N_EXPERTS, dtype=jnp.float32) * wts[..., None], axis=-2)
    gate = gate.astype(u.dtype)
    out = jnp.zeros_like(u)
    for e in range(N_EXPERTS):
        out = out + gate[..., e:e + 1] * _swiglu(u, w1[e], w3[e], w2[e])
    return out


def setup_inputs(seed: int = 0) -> dict:
    key = jax.random.key(seed)
    ks = jax.random.split(key, 24)
    f32 = jnp.float32
    nrm = lambda k, shp, s: jax.random.normal(k, shp, f32) * s
    return {
        'x': nrm(ks[0], (BATCH, SEQ, D_MODEL), 1.0),
        'p': nrm(ks[1], (DEPTH, BATCH, SEQ, PLE_DIM), 1.0),
        'w_in': nrm(ks[2], (DEPTH, D_MODEL, IN_WIDTH), D_MODEL ** -0.5),
        'sinks': nrm(ks[3], (DEPTH, ATT_Q_HEADS), 0.5),
        'rel_bias': nrm(ks[4], (REL_BUCKETS, ATT_Q_HEADS), 0.5),
        'lb_logits': nrm(ks[5], (DEPTH, HG_K_W), 1.0),
        'hgrn_norm': 1.0 + nrm(ks[6], (DEPTH, HG_DV), 0.01),
        'w_branch_a': nrm(ks[7], (DEPTH, ATT_Q_W, D_MODEL), ATT_Q_W ** -0.5),
        'w_branch_b': nrm(ks[8], (DEPTH, HG_V_W, D_MODEL), HG_V_W ** -0.5),
        'w_out': nrm(ks[9], (DEPTH, D_MODEL, D_MODEL), D_MODEL ** -0.5),
        'norm_mix': 1.0 + nrm(ks[10], (DEPTH, D_MODEL), 0.01),
        'norm_ffn': 1.0 + nrm(ks[11], (DEPTH, D_MODEL), 0.01),
        'norm_ple': 1.0 + nrm(ks[12], (DEPTH, D_MODEL), 0.01),
        'dense_w1': nrm(ks[13], (N_DENSE, D_MODEL, FFN_DENSE), D_MODEL ** -0.5),
        'dense_w3': nrm(ks[14], (N_DENSE, D_MODEL, FFN_DENSE), D_MODEL ** -0.5),
        'dense_w2': nrm(ks[15], (N_DENSE, FFN_DENSE, D_MODEL), FFN_DENSE ** -0.5),
        'router_w': nrm(ks[16], (N_MOE, D_MODEL, N_EXPERTS), D_MODEL ** -0.5),
        'moe_w1': nrm(ks[17], (N_MOE, N_EXPERTS, D_MODEL, FFN_EXPERT), D_MODEL ** -0.5),
        'moe_w3': nrm(ks[18], (N_MOE, N_EXPERTS, D_MODEL, FFN_EXPERT), D_MODEL ** -0.5),
        'moe_w2': nrm(ks[19], (N_MOE, N_EXPERTS, FFN_EXPERT, D_MODEL), FFN_EXPERT ** -0.5),
        'ple_proj': nrm(ks[20], (DEPTH, PLE_DIM, D_MODEL), PLE_DIM ** -0.5),
        'ple_gate': nrm(ks[21], (DEPTH, D_MODEL, D_MODEL), D_MODEL ** -0.5),
        'final_norm': 1.0 + nrm(ks[22], (D_MODEL,), 0.01),
    }


def reference(x, p, w_in, sinks, rel_bias, lb_logits, hgrn_norm, w_branch_a, w_branch_b,
              w_out, norm_mix, norm_ffn, norm_ple, dense_w1, dense_w3, dense_w2, router_w,
              moe_w1, moe_w3, moe_w2, ple_proj, ple_gate, final_norm):
    B, S = x.shape[0], x.shape[1]
    qi = jnp.arange(ATT_BLOCK)[:, None]
    kj = jnp.arange(2 * ATT_BLOCK)[None, :]
    band_bias = jnp.transpose(rel_bias[_t5_bucket(qi + ATT_BLOCK - kj)], (2, 0, 1))
    sm = jax.nn.softmax(lb_logits.astype(jnp.float32), axis=0)
    lower_bounds = jnp.cumsum(sm, axis=0) - sm[0:1]

    h = x
    for l in range(DEPTH):
        u = _rmsnorm(h, norm_mix[l])
        aq, ak, av, hq, hf, hi, hg, ga, gb = _split(u @ w_in[l])
        ya = _sliding_window_attention(
            aq.reshape(B, S, ATT_Q_HEADS, ATT_HEAD_DIM),
            ak.reshape(B, S, ATT_KV_HEADS, ATT_HEAD_DIM),
            av.reshape(B, S, ATT_KV_HEADS, ATT_HEAD_DIM),
            sinks[l], band_bias) @ w_branch_a[l]
        yb = _hgrn2(
            hq.reshape(B, S, HG_HEADS, HG_DK),
            hf.reshape(B, S, HG_HEADS, HG_DK),
            hi.reshape(B, S, HG_HEADS, HG_DV),
            hg, lower_bounds[l], hgrn_norm[l]) @ w_branch_b[l]
        merged = jax.nn.sigmoid(ga) * ya + jax.nn.sigmoid(gb) * yb
        h = h + merged @ w_out[l]

        u = _rmsnorm(h, norm_ffn[l])
        if l % 2 == 0:
            h = h + _swiglu(u, dense_w1[l // 2], dense_w3[l // 2], dense_w2[l // 2])
        else:
            h = h + _moe(u, router_w[l // 2], moe_w1[l // 2], moe_w3[l // 2], moe_w2[l // 2])

        u = _rmsnorm(h, norm_ple[l])
        h = h + jax.nn.sigmoid(u @ ple_gate[l]) * (p[l] @ ple_proj[l])
    return _rmsnorm(h, final_norm)
```

```python
import functools
import math

import jax
import jax.numpy as jnp
from jax import lax
from jax.experimental import pallas as pl
from jax.experimental.pallas import tpu as pltpu

F32 = jnp.float32
BF16 = jnp.bfloat16

D_MODEL = 1024
BATCH = 4
SEQ = 4096
TOKENS = BATCH * SEQ
DEPTH = 4
ATT_Q_HEADS = 8
ATT_KV_HEADS = 2
ATT_HEAD_DIM = 64
ATT_GROUP = ATT_Q_HEADS // ATT_KV_HEADS
WINDOW = 128
ATT_BLOCK = 128
REL_BUCKETS = 32
REL_MAX_DIST = 128
HG_HEADS = 4
HG_DK = 128
HG_DV = 128
ATT_Q_W = ATT_Q_HEADS * ATT_HEAD_DIM
ATT_KV_W = ATT_KV_HEADS * ATT_HEAD_DIM
HG_K_W = HG_HEADS * HG_DK
HG_V_W = HG_HEADS * HG_DV
IN_SPLITS = (ATT_Q_W, ATT_KV_W, ATT_KV_W, HG_K_W, HG_K_W, HG_V_W, HG_V_W, D_MODEL, D_MODEL)
IN_WIDTH = sum(IN_SPLITS)
FFN_DENSE = 2816
N_EXPERTS = 8
TOP_K = 2
FFN_EXPERT = 3584
PLE_DIM = 256
EPS = 1e-6

V7X_VMEM_LIMIT_BYTES = 52 * 1024 * 1024
ROW_TILE = 512
HG_TILE = 128
DENSE_F_TILE = FFN_DENSE // 2
MOE_ROW_TILE = 512
MOE_F_TILE = FFN_EXPERT // 2
MOE_NF = FFN_EXPERT // MOE_F_TILE
MOE_TILES = (TOKENS * TOP_K) // MOE_ROW_TILE + N_EXPERTS - 1
MOE_ROWS = MOE_TILES * MOE_ROW_TILE
COL_CHUNK = 512


def _cparams(sem):
    return pltpu.CompilerParams(dimension_semantics=sem, vmem_limit_bytes=V7X_VMEM_LIMIT_BYTES)


def _rms(x, g):
    return x * lax.rsqrt(jnp.mean(x * x, axis=-1, keepdims=True) + EPS) * g


def _sigmoid(x):
    return 1.0 / (1.0 + jnp.exp(-x))


def _dot(a, b):
    return jnp.dot(a, b, preferred_element_type=F32)


def _dot_nt(a, b):
    return lax.dot_general(a, b, (((1,), (1,)), ((), ())), preferred_element_type=F32)


def _resident(shape):
    nd = len(shape)
    return pl.BlockSpec(shape, lambda *_: (0,) * nd)


IN_DTYPES = (BF16, BF16, BF16, BF16, F32, BF16, BF16, BF16, BF16)


def _in_proj_kernel(h_ref, g_ref, w_ref, *out_refs):
    xn = _rms(h_ref[...], g_ref[...]).astype(BF16)
    off = 0
    for o_ref, width in zip(out_refs, IN_SPLITS):
        for c in range(0, width, COL_CHUNK):
            cw = min(COL_CHUNK, width - c)
            o_ref[:, c:c + cw] = _dot(xn, w_ref[:, off + c:off + c + cw]).astype(o_ref.dtype)
        off += width


def _in_proj(h, g, w):
    row = lambda width: pl.BlockSpec((ROW_TILE, width), lambda i: (i, 0))
    return pl.pallas_call(
        _in_proj_kernel,
        grid=(TOKENS // ROW_TILE,),
        in_specs=[row(D_MODEL), _resident((1, D_MODEL)), _resident((D_MODEL, IN_WIDTH))],
        out_specs=[row(wd) for wd in IN_SPLITS],
        out_shape=[jax.ShapeDtypeStruct((TOKENS, wd), dt) for wd, dt in zip(IN_SPLITS, IN_DTYPES)],
        compiler_params=_cparams(("arbitrary",)),
        name="in_proj",
    )(h, g, w)


def _attn_kernel(sink_ref, q_ref, kc_ref, kp_ref, vc_ref, vp_ref, bias_ref, o_ref):
    n = pl.program_id(1)
    r = lax.broadcasted_iota(jnp.int32, (ATT_BLOCK, 2 * ATT_BLOCK), 0)
    j = lax.broadcasted_iota(jnp.int32, (ATT_BLOCK, 2 * ATT_BLOCK), 1)
    dist = r + ATT_BLOCK - j
    valid = (dist >= 0) & (dist < WINDOW) & ((j >= ATT_BLOCK) | (n > 0))
    kb = jnp.concatenate([kp_ref[...], kc_ref[...]], axis=0)
    vb = jnp.concatenate([vp_ref[...], vc_ref[...]], axis=0)
    scale = ATT_HEAD_DIM ** -0.5
    for h in range(ATT_Q_HEADS):
        g = h // ATT_GROUP
        hs = slice(h * ATT_HEAD_DIM, (h + 1) * ATT_HEAD_DIM)
        gs = slice(g * ATT_HEAD_DIM, (g + 1) * ATT_HEAD_DIM)
        s = _dot_nt(q_ref[:, hs], kb[:, gs]) * scale + bias_ref[h]
        s = jnp.where(valid, s, -1e30)
        sink = sink_ref[h]
        m = jnp.maximum(jnp.max(s, axis=-1, keepdims=True), sink)
        p = jnp.exp(s - m)
        denom = jnp.sum(p, axis=-1, keepdims=True) + jnp.exp(sink - m)
        o = _dot(p.astype(BF16), vb[:, gs]) / denom
        o_ref[:, hs] = o.astype(o_ref.dtype)


def _attention(aq, ak, av, sinks, band_bias):
    nb = SEQ // ATT_BLOCK
    cur = lambda b, n: (b * nb + n, 0)
    prev = lambda b, n: (jnp.maximum(b * nb + n - 1, 0), 0)
    return pl.pallas_call(
        _attn_kernel,
        grid=(BATCH, nb),
        in_specs=[
            pl.BlockSpec(memory_space=pltpu.SMEM),
            pl.BlockSpec((ATT_BLOCK, ATT_Q_W), cur),
            pl.BlockSpec((ATT_BLOCK, ATT_KV_W), cur),
            pl.BlockSpec((ATT_BLOCK, ATT_KV_W), prev),
            pl.BlockSpec((ATT_BLOCK, ATT_KV_W), cur),
            pl.BlockSpec((ATT_BLOCK, ATT_KV_W), prev),
            _resident((ATT_Q_HEADS, ATT_BLOCK, 2 * ATT_BLOCK)),
        ],
        out_specs=pl.BlockSpec((ATT_BLOCK, ATT_Q_W), cur),
        out_shape=jax.ShapeDtypeStruct((TOKENS, ATT_Q_W), BF16),
        compiler_params=_cparams(("arbitrary", "arbitrary")),
        name="swa_attention",
    )(sinks, aq, ak, ak, av, av, band_bias)


def _cumsum_rows(x, t):
    shift = 1
    while shift < x.shape[0]:
        x = x + jnp.where(t >= shift, pltpu.roll(x, shift, 0), 0.0)
        shift *= 2
    return x


def _hgrn_kernel(q_ref, f_ref, v_ref, g_ref, lbl_ref, ng_ref, o_ref, st_ref, *, layer):
    @pl.when(pl.program_id(1) == 0)
    def _():
        st_ref[...] = jnp.zeros_like(st_ref)

    rows = [lbl_ref[i:i + 1, :] for i in range(DEPTH)]
    mx = functools.reduce(jnp.maximum, rows)
    ex = [jnp.exp(rw - mx) for rw in rows]
    tot = functools.reduce(lambda a, b: a + b, ex)
    lower = jnp.zeros_like(mx)
    for i in range(1, layer + 1):
        lower = lower + ex[i] / tot

    c = HG_TILE
    t = lax.broadcasted_iota(jnp.int32, (c, HG_DK), 0)
    ts = lax.broadcasted_iota(jnp.int32, (c, c), 0)
    ss = lax.broadcasted_iota(jnp.int32, (c, c), 1)
    same64 = (ts >> 6) == (ss >> 6)
    diag32 = ((ts >> 5) == (ss >> 5)) & (ss <= ts)
    ng = ng_ref[...]

    for h in range(HG_HEADS):
        sl = slice(h * HG_DK, (h + 1) * HG_DK)
        lb = lower[:, sl]
        f = lb + (1.0 - lb) * _sigmoid(f_ref[:, sl])
        k = 1.0 - f
        a = _cumsum_rows(jnp.log(f), t)
        a_last = a[c - 1:c, :]
        q = q_ref[:, sl].astype(F32)
        v = v_ref[:, sl]
        st = st_ref[h]

        o = _dot_nt((q * jnp.exp(a)).astype(BF16), st.astype(BF16))
        ke = (k * jnp.exp(a_last - a)).astype(BF16)
        vt = v.astype(F32).T.astype(BF16)
        st_ref[h] = st * jnp.exp(a_last) + _dot(vt, ke)

        m1 = a[63:64, :]
        q1 = jnp.where(t >= 64, q * jnp.exp(a - m1), 0.0)
        k1 = jnp.where(t < 64, k * jnp.exp(m1 - a), 0.0)
        p = _dot_nt(q1.astype(BF16), k1.astype(BF16))

        m2 = jnp.where(t < 64, a[31:32, :], a[95:96, :])
        second = (t & 63) >= 32
        q2 = jnp.where(second, q * jnp.exp(a - m2), 0.0)
        k2 = jnp.where(second, 0.0, k * jnp.exp(m2 - a))
        p = p + jnp.where(same64, _dot_nt(q2.astype(BF16), k2.astype(BF16)), 0.0)

        m3 = jnp.where(t < 64,
                       jnp.where(t < 32, a[15:16, :], a[47:48, :]),
                       jnp.where(t < 96, a[79:80, :], a[111:112, :]))
        q3 = q * jnp.exp(a - m3)
        k3 = k * jnp.exp(m3 - a)
        p = p + jnp.where(diag32, _dot_nt(q3.astype(BF16), k3.astype(BF16)), 0.0)

        o = o + _dot(p.astype(BF16), v)
        o = o * lax.rsqrt(jnp.mean(o * o, axis=-1, keepdims=True) + EPS) * ng
        gate = g_ref[:, sl].astype(F32)
        o_ref[:, sl] = (o * (gate * _sigmoid(gate))).astype(o_ref.dtype)


def _hgrn(hq, hf, hi, hg, lb_logits, norm_g, layer):
    nt = SEQ // HG_TILE
    blk = pl.BlockSpec((HG_TILE, HG_K_W), lambda b, s: (b * nt + s, 0))
    return pl.pallas_call(
        functools.partial(_hgrn_kernel, layer=layer),
        grid=(BATCH, nt),
        in_specs=[blk, blk, blk, blk, _resident((DEPTH, HG_K_W)), _resident((1, HG_DV))],
        out_specs=blk,
        out_shape=jax.ShapeDtypeStruct((TOKENS, HG_V_W), BF16),
        scratch_shapes=[pltpu.VMEM((HG_HEADS, HG_DV, HG_DK), F32)],
        compiler_params=_cparams(("arbitrary", "arbitrary")),
        name="hgrn2",
    )(hq, hf, hi, hg, lb_logits, norm_g)


def _merge_kernel(att_ref, hgo_ref, ga_ref, gb_ref, h_ref, wa_ref, wb_ref, wo_ref, *rest, moe):
    ya = _dot(att_ref[...], wa_ref[...])
    yb = _dot(hgo_ref[...], wb_ref[...])
    merged = _sigmoid(ga_ref[...].astype(F32)) * ya + _sigmoid(gb_ref[...].astype(F32)) * yb
    hn = h_ref[...] + _dot(merged.astype(BF16), wo_ref[...])
    if not moe:
        (ho_ref,) = rest
        ho_ref[...] = hn
        return
    gn_ref, rw_ref, ho_ref, u_ref, idx_ref, gate_ref = rest
    ho_ref[...] = hn
    u = _rms(hn, gn_ref[...])
    u_ref[...] = u
    ls = [jnp.sum(u * rw_ref[e:e + 1, :], axis=-1, keepdims=True) for e in range(N_EXPERTS)]
    m1 = functools.reduce(jnp.maximum, ls)
    i1 = jnp.full(m1.shape, N_EXPERTS, jnp.int32)
    for e in reversed(range(N_EXPERTS)):
        i1 = jnp.where(ls[e] == m1, e, i1)
    rest_ls = [jnp.where(i1 == e, -jnp.inf, ls[e]) for e in range(N_EXPERTS)]
    m2 = functools.reduce(jnp.maximum, rest_ls)
    i2 = jnp.full(m1.shape, N_EXPERTS, jnp.int32)
    for e in reversed(range(N_EXPERTS)):
        i2 = jnp.where((rest_ls[e] == m2) & (i1 != e), e, i2)
    e2 = jnp.exp(m2 - m1)
    den = 1.0 + e2
    idx_ref[...] = jnp.concatenate([i1, i2], axis=1)
    gate_ref[...] = jnp.concatenate([1.0 / den, e2 / den], axis=1)


def _merge(att, hgo, ga, gb, h, wa, wb, wo, gn=None, rw=None):
    moe = gn is not None
    row = lambda width: pl.BlockSpec((ROW_TILE, width), lambda i: (i, 0))
    in_specs = [row(ATT_Q_W), row(HG_V_W), row(D_MODEL), row(D_MODEL), row(D_MODEL),
                _resident((ATT_Q_W, D_MODEL)), _resident((HG_V_W, D_MODEL)),
                _resident((D_MODEL, D_MODEL))]
    args = [att, hgo, ga, gb, h, wa, wb, wo]
    out_specs = [row(D_MODEL)]
    out_shape = [jax.ShapeDtypeStruct((TOKENS, D_MODEL), F32)]
    if moe:
        in_specs += [_resident((1, D_MODEL)), _resident((N_EXPERTS, D_MODEL))]
        args += [gn, rw]
        out_specs += [row(D_MODEL), row(TOP_K), row(TOP_K)]
        out_shape += [jax.ShapeDtypeStruct((TOKENS, D_MODEL), F32),
                      jax.ShapeDtypeStruct((TOKENS, TOP_K), jnp.int32),
                      jax.ShapeDtypeStruct((TOKENS, TOP_K), F32)]
    return pl.pallas_call(
        functools.partial(_merge_kernel, moe=moe),
        grid=(TOKENS // ROW_TILE,),
        in_specs=in_specs,
        out_specs=out_specs,
        out_shape=out_shape,
        compiler_params=_cparams(("arbitrary",)),
        name="merge_moe" if moe else "merge",
    )(*args)


def _dense_ffn_kernel(h_ref, g_ref, w1_ref, w3_ref, w2_ref, o_ref, x_scr, acc_scr):
    f = pl.program_id(1)

    @pl.when(f == 0)
    def _():
        x_scr[...] = _rms(h_ref[...], g_ref[...]).astype(BF16)
        acc_scr[...] = jnp.zeros_like(acc_scr)

    x = x_scr[...]
    a = _dot(x, w1_ref[...])
    b = _dot(x, w3_ref[...])
    acc_scr[...] += _dot((a * _sigmoid(a) * b).astype(BF16), w2_ref[...])

    @pl.when(f == pl.num_programs(1) - 1)
    def _():
        o_ref[...] = h_ref[...] + acc_scr[...]


def _dense_ffn(h, g, w1, w3, w2):
    nf = FFN_DENSE // DENSE_F_TILE
    return pl.pallas_call(
        _dense_ffn_kernel,
        grid=(TOKENS // ROW_TILE, nf),
        in_specs=[
            pl.BlockSpec((ROW_TILE, D_MODEL), lambda i, f: (i, 0)),
            _resident((1, D_MODEL)),
            pl.BlockSpec((D_MODEL, DENSE_F_TILE), lambda i, f: (0, f)),
            pl.BlockSpec((D_MODEL, DENSE_F_TILE), lambda i, f: (0, f)),
            pl.BlockSpec((DENSE_F_TILE, D_MODEL), lambda i, f: (f, 0)),
        ],
        out_specs=pl.BlockSpec((ROW_TILE, D_MODEL), lambda i, f: (i, 0)),
        out_shape=jax.ShapeDtypeStruct((TOKENS, D_MODEL), F32),
        scratch_shapes=[pltpu.VMEM((ROW_TILE, D_MODEL), BF16), pltpu.VMEM((ROW_TILE, D_MODEL), F32)],
        compiler_params=_cparams(("arbitrary", "arbitrary")),
        name="dense_ffn",
    )(h, g, w1, w3, w2)


def _moe_kernel(te_ref, rows_ref, nu_ref, src_ref, dst_ref, gate_ref, u_hbm,
                w1_ref, w3_ref, w2_ref, y_hbm, xg, xb, acc, ysc, gsem, ssem):
    i = pl.program_id(0)
    f = pl.program_id(1)
    n_used = nu_ref[0]
    valid = i < n_used

    def wait_scatter(tile):
        def body(r, carry):
            pltpu.make_async_copy(ysc.at[pl.ds(0, 1)], y_hbm.at[pl.ds(0, 1)], ssem).wait()
            return carry
        lax.fori_loop(0, rows_ref[tile], body, 0)

    @pl.when((f == 0) & (i > 0) & valid)
    def _():
        wait_scatter(i - 1)

    @pl.when((f == 0) & valid)
    def _():
        def start(r, carry):
            tok = src_ref[0, 0, r]
            pltpu.make_async_copy(u_hbm.at[pl.ds(tok, 1)], xg.at[pl.ds(r, 1)], gsem).start()
            return carry
        lax.fori_loop(0, MOE_ROW_TILE, start, 0)

        def wait(r, carry):
            pltpu.make_async_copy(u_hbm.at[pl.ds(0, 1)], xg.at[pl.ds(0, 1)], gsem).wait()
            return carry
        lax.fori_loop(0, MOE_ROW_TILE, wait, 0)
        xb[...] = xg[...].astype(BF16)
        acc[...] = jnp.zeros_like(acc)

    @pl.when(valid)
    def _():
        x = xb[...]
        a = _dot(x, w1_ref[0])
        b = _dot(x, w3_ref[0])
        acc[...] += _dot((a * _sigmoid(a) * b).astype(BF16), w2_ref[0])

    @pl.when((f == MOE_NF - 1) & valid)
    def _():
        ysc[...] = acc[...] * gate_ref[...]

        def start(r, carry):
            d = dst_ref[0, 0, r]
            pltpu.make_async_copy(ysc.at[pl.ds(r, 1)], y_hbm.at[pl.ds(d, 1)], ssem).start()
            return carry
        lax.fori_loop(0, rows_ref[i], start, 0)

        @pl.when(i == n_used - 1)
        def _():
            wait_scatter(i)


def _moe_experts(u, tile_expert, tile_rows, n_used, src, dst, gate_sorted, w1, w3, w2):
    smem_blk = pl.BlockSpec((1, 1, MOE_ROW_TILE), lambda i, f, *_: (i, 0, 0), memory_space=pltpu.SMEM)
    grid_spec = pltpu.PrefetchScalarGridSpec(
        num_scalar_prefetch=3,
        grid=(MOE_TILES, MOE_NF),
        in_specs=[
            smem_blk,
            smem_blk,
            pl.BlockSpec((MOE_ROW_TILE, 1), lambda i, f, *_: (i, 0)),
            pl.BlockSpec(memory_space=pl.ANY),
            pl.BlockSpec((1, D_MODEL, MOE_F_TILE), lambda i, f, te, *_: (te[i], 0, f)),
            pl.BlockSpec((1, D_MODEL, MOE_F_TILE), lambda i, f, te, *_: (te[i], 0, f)),
            pl.BlockSpec((1, MOE_F_TILE, D_MODEL), lambda i, f, te, *_: (te[i], f, 0)),
        ],
        out_specs=pl.BlockSpec(memory_space=pl.ANY),
        scratch_shapes=[
            pltpu.VMEM((MOE_ROW_TILE, D_MODEL), F32),
            pltpu.VMEM((MOE_ROW_TILE, D_MODEL), BF16),
            pltpu.VMEM((MOE_ROW_TILE, D_MODEL), F32),
            pltpu.VMEM((MOE_ROW_TILE, D_MODEL), F32),
            pltpu.SemaphoreType.DMA(()),
            pltpu.SemaphoreType.DMA(()),
        ],
    )
    return pl.pallas_call(
        _moe_kernel,
        grid_spec=grid_spec,
        out_shape=jax.ShapeDtypeStruct((TOKENS * TOP_K, D_MODEL), F32),
        compiler_params=_cparams(("arbitrary", "arbitrary")),
        name="moe_experts",
    )(tile_expert, tile_rows, n_used, src, dst, gate_sorted, u, w1, w3, w2)


def _moe_plan(idx, gate):
    e_flat = idx.reshape(-1)
    onehot = (e_flat[:, None] == jnp.arange(N_EXPERTS, dtype=jnp.int32)[None, :]).astype(jnp.int32)
    csum = jnp.cumsum(onehot, axis=0)
    counts = csum[-1]
    rank = jnp.sum((csum - onehot) * onehot, axis=1)
    tiles_per = (counts + MOE_ROW_TILE - 1) // MOE_ROW_TILE
    tile_end = jnp.cumsum(tiles_per)
    tile_start = tile_end - tiles_per
    pos = jnp.sum(onehot * tile_start[None, :], axis=1) * MOE_ROW_TILE + rank
    n_pairs = TOKENS * TOP_K
    pair = jnp.arange(n_pairs, dtype=jnp.int32)
    src = jnp.zeros((MOE_ROWS,), jnp.int32).at[pos].set(pair // TOP_K)
    dst = jnp.zeros((MOE_ROWS,), jnp.int32).at[pos].set(pair)
    gate_sorted = jnp.zeros((MOE_ROWS,), F32).at[pos].set(gate.reshape(-1))
    tile_id = jnp.arange(MOE_TILES, dtype=jnp.int32)
    te = jnp.sum((tile_id[:, None] >= tile_end[None, :]).astype(jnp.int32), axis=1)
    n_used = tile_end[-1]
    te = jnp.minimum(te, N_EXPERTS - 1)
    te_c = jnp.where(tile_id < n_used, te, te[jnp.maximum(n_used - 1, 0)])
    rows_in = jnp.clip(counts[te] - (tile_id - tile_start[te]) * MOE_ROW_TILE, 0, MOE_ROW_TILE)
    rows_in = jnp.where(tile_id < n_used, rows_in, 0).astype(jnp.int32)
    return (te_c.astype(jnp.int32), rows_in, n_used.reshape(1).astype(jnp.int32),
            src.reshape(MOE_TILES, 1, MOE_ROW_TILE), dst.reshape(MOE_TILES, 1, MOE_ROW_TILE),
            gate_sorted.reshape(MOE_ROWS, 1))


def _ple_kernel(*refs, moe, final):
    refs = list(refs)
    h_ref = refs.pop(0)
    y_ref = refs.pop(0) if moe else None
    p_ref, g_ref, wg_ref, wp_ref = refs[:4]
    refs = refs[4:]
    fn_ref = refs.pop(0) if final else None
    (o_ref,) = refs
    h = h_ref[...]
    if moe:
        h = h + y_ref[:, :D_MODEL] + y_ref[:, D_MODEL:]
    u = _rms(h, g_ref[...]).astype(BF16)
    gate = _sigmoid(_dot(u, wg_ref[...]))
    h = h + gate * _dot(p_ref[...].astype(BF16), wp_ref[...])
    if final:
        h = _rms(h, fn_ref[...])
    o_ref[...] = h


def _ple(h, y, p, g, wg, wp, final_g):
    moe = y is not None
    final = final_g is not None
    row = lambda width: pl.BlockSpec((ROW_TILE, width), lambda i: (i, 0))
    in_specs = [row(D_MODEL)]
    args = [h]
    if moe:
        in_specs.append(row(TOP_K * D_MODEL))
        args.append(y)
    in_specs += [row(PLE_DIM), _resident((1, D_MODEL)), _resident((D_MODEL, D_MODEL)),
                 _resident((PLE_DIM, D_MODEL))]
    args += [p, g, wg, wp]
    if final:
        in_specs.append(_resident((1, D_MODEL)))
        args.append(final_g)
    return pl.pallas_call(
        functools.partial(_ple_kernel, moe=moe, final=final),
        grid=(TOKENS // ROW_TILE,),
        in_specs=in_specs,
        out_specs=row(D_MODEL),
        out_shape=jax.ShapeDtypeStruct((TOKENS, D_MODEL), F32),
        compiler_params=_cparams(("arbitrary",)),
        name="ple",
    )(*args)


def _t5_bucket(dist):
    max_exact = REL_BUCKETS // 2
    d = jnp.maximum(dist, 0)
    large = max_exact + (jnp.log(jnp.maximum(d, 1).astype(jnp.float32) / max_exact)
                         / math.log(REL_MAX_DIST / max_exact)
                         * (REL_BUCKETS - max_exact)).astype(jnp.int32)
    large = jnp.minimum(large, REL_BUCKETS - 1)
    return jnp.where(d < max_exact, d, large)


def kernel(x, p, w_in, sinks, rel_bias, lb_logits, hgrn_norm, w_branch_a, w_branch_b, w_out,
           norm_mix, norm_ffn, norm_ple, dense_w1, dense_w3, dense_w2, router_w, moe_w1,
           moe_w3, moe_w2, ple_proj, ple_gate, final_norm):
    qi = jnp.arange(ATT_BLOCK)[:, None]
    kj = jnp.arange(2 * ATT_BLOCK)[None, :]
    band_bias = jnp.transpose(rel_bias[_t5_bucket(qi + ATT_BLOCK - kj)], (2, 0, 1)).astype(F32)

    bf = lambda w: w.astype(BF16)
    h = x.reshape(TOKENS, D_MODEL)
    pt = p.reshape(DEPTH, TOKENS, PLE_DIM)
    vec = lambda g: g.reshape(1, -1)

    for l in range(DEPTH):
        moe = l % 2 == 1
        aq, ak, av, hq, hf, hi, hg, ga, gb = _in_proj(h, vec(norm_mix[l]), bf(w_in[l]))
        att = _attention(aq, ak, av, sinks[l], band_bias)
        hgo = _hgrn(hq, hf, hi, hg, lb_logits, vec(hgrn_norm[l]), l)
        wa, wb, wo = bf(w_branch_a[l]), bf(w_branch_b[l]), bf(w_out[l])
        final_g = vec(final_norm) if l == DEPTH - 1 else None
        if not moe:
            (h,) = _merge(att, hgo, ga, gb, h, wa, wb, wo)
            h = _dense_ffn(h, vec(norm_ffn[l]), bf(dense_w1[l // 2]), bf(dense_w3[l // 2]),
                           bf(dense_w2[l // 2]))
            y = None
        else:
            h, u, idx, gate = _merge(att, hgo, ga, gb, h, wa, wb, wo,
                                     vec(norm_ffn[l]), router_w[l // 2].T)
            plan = _moe_plan(idx, gate)
            y = _moe_experts(u, *plan, bf(moe_w1[l // 2]), bf(moe_w3[l // 2]), bf(moe_w2[l // 2]))
            y = y.reshape(TOKENS, TOP_K * D_MODEL)
        h = _ple(h, y, pt[l], vec(norm_ple[l]), bf(ple_gate[l]), bf(ple_proj[l]), final_g)
    return h.reshape(BATCH, SEQ, D_MODEL)
```

```python
import functools
import math

import jax
import jax.numpy as jnp
from jax import lax
from jax.experimental import pallas as pl
from jax.experimental.pallas import tpu as pltpu

F32 = jnp.float32
BF16 = jnp.bfloat16

D_MODEL = 1024
BATCH = 4
SEQ = 4096
TOKENS = BATCH * SEQ
DEPTH = 4
ATT_Q_HEADS = 8
ATT_KV_HEADS = 2
ATT_HEAD_DIM = 64
ATT_GROUP = ATT_Q_HEADS // ATT_KV_HEADS
WINDOW = 128
ATT_BLOCK = 128
REL_BUCKETS = 32
REL_MAX_DIST = 128
HG_HEADS = 4
HG_DK = 128
HG_DV = 128
ATT_Q_W = ATT_Q_HEADS * ATT_HEAD_DIM
ATT_KV_W = ATT_KV_HEADS * ATT_HEAD_DIM
HG_K_W = HG_HEADS * HG_DK
HG_V_W = HG_HEADS * HG_DV
IN_SPLITS = (ATT_Q_W, ATT_KV_W, ATT_KV_W, HG_K_W, HG_K_W, HG_V_W, HG_V_W, D_MODEL, D_MODEL)
IN_WIDTH = sum(IN_SPLITS)
FFN_DENSE = 2816
N_EXPERTS = 8
TOP_K = 2
FFN_EXPERT = 3584
PLE_DIM = 256
EPS = 1e-6

V7X_VMEM_LIMIT_BYTES = 52 * 1024 * 1024
ROW_TILE = 512
HG_TILE = 128
DENSE_F_TILE = FFN_DENSE // 2
MOE_ROW_TILE = 512
DISPATCH_TILE = 1024
MOE_F_TILE = FFN_EXPERT // 2
MOE_NF = FFN_EXPERT // MOE_F_TILE
MOE_TILES = (TOKENS * TOP_K) // MOE_ROW_TILE + N_EXPERTS - 1
MOE_ROWS = MOE_TILES * MOE_ROW_TILE
COL_CHUNK = 512


def _cparams(sem):
    return pltpu.CompilerParams(dimension_semantics=sem, vmem_limit_bytes=V7X_VMEM_LIMIT_BYTES)


def _rms(x, g):
    return x * lax.rsqrt(jnp.mean(x * x, axis=-1, keepdims=True) + EPS) * g


def _sigmoid(x):
    return 1.0 / (1.0 + jnp.exp(-x))


def _dot(a, b):
    return jnp.dot(a, b, preferred_element_type=F32)


def _dot_nt(a, b):
    return lax.dot_general(a, b, (((1,), (1,)), ((), ())), preferred_element_type=F32)


def _resident(shape):
    nd = len(shape)
    return pl.BlockSpec(shape, lambda *_: (0,) * nd)


IN_DTYPES = (BF16, BF16, BF16, BF16, F32, BF16, BF16, BF16, BF16)


def _in_proj_kernel(h_ref, g_ref, w_ref, *out_refs):
    xn = _rms(h_ref[...], g_ref[...]).astype(BF16)
    off = 0
    for o_ref, width in zip(out_refs, IN_SPLITS):
        for c in range(0, width, COL_CHUNK):
            cw = min(COL_CHUNK, width - c)
            o_ref[:, c:c + cw] = _dot(xn, w_ref[:, off + c:off + c + cw]).astype(o_ref.dtype)
        off += width


def _in_proj(h, g, w):
    row = lambda width: pl.BlockSpec((ROW_TILE, width), lambda i: (i, 0))
    return pl.pallas_call(
        _in_proj_kernel,
        grid=(TOKENS // ROW_TILE,),
        in_specs=[row(D_MODEL), _resident((1, D_MODEL)), _resident((D_MODEL, IN_WIDTH))],
        out_specs=[row(wd) for wd in IN_SPLITS],
        out_shape=[jax.ShapeDtypeStruct((TOKENS, wd), dt) for wd, dt in zip(IN_SPLITS, IN_DTYPES)],
        compiler_params=_cparams(("arbitrary",)),
        name="in_proj",
    )(h, g, w)


def _attn_kernel(sink_ref, q_ref, kc_ref, kp_ref, vc_ref, vp_ref, bias_ref, o_ref):
    n = pl.program_id(1)
    r = lax.broadcasted_iota(jnp.int32, (ATT_BLOCK, 2 * ATT_BLOCK), 0)
    j = lax.broadcasted_iota(jnp.int32, (ATT_BLOCK, 2 * ATT_BLOCK), 1)
    dist = r + ATT_BLOCK - j
    valid = (dist >= 0) & (dist < WINDOW) & ((j >= ATT_BLOCK) | (n > 0))
    kb = jnp.concatenate([kp_ref[...], kc_ref[...]], axis=0)
    vb = jnp.concatenate([vp_ref[...], vc_ref[...]], axis=0)
    scale = ATT_HEAD_DIM ** -0.5
    for h in range(ATT_Q_HEADS):
        g = h // ATT_GROUP
        hs = slice(h * ATT_HEAD_DIM, (h + 1) * ATT_HEAD_DIM)
        gs = slice(g * ATT_HEAD_DIM, (g + 1) * ATT_HEAD_DIM)
        s = _dot_nt(q_ref[:, hs], kb[:, gs]) * scale + bias_ref[h]
        s = jnp.where(valid, s, -1e30)
        sink = sink_ref[h]
        m = jnp.maximum(jnp.max(s, axis=-1, keepdims=True), sink)
        p = jnp.exp(s - m)
        denom = jnp.sum(p, axis=-1, keepdims=True) + jnp.exp(sink - m)
        o = _dot(p.astype(BF16), vb[:, gs]) / denom
        o_ref[:, hs] = o.astype(o_ref.dtype)


def _attention(aq, ak, av, sinks, band_bias):
    nb = SEQ // ATT_BLOCK
    cur = lambda b, n: (b * nb + n, 0)
    prev = lambda b, n: (jnp.maximum(b * nb + n - 1, 0), 0)
    return pl.pallas_call(
        _attn_kernel,
        grid=(BATCH, nb),
        in_specs=[
            pl.BlockSpec(memory_space=pltpu.SMEM),
            pl.BlockSpec((ATT_BLOCK, ATT_Q_W), cur),
            pl.BlockSpec((ATT_BLOCK, ATT_KV_W), cur),
            pl.BlockSpec((ATT_BLOCK, ATT_KV_W), prev),
            pl.BlockSpec((ATT_BLOCK, ATT_KV_W), cur),
            pl.BlockSpec((ATT_BLOCK, ATT_KV_W), prev),
            _resident((ATT_Q_HEADS, ATT_BLOCK, 2 * ATT_BLOCK)),
        ],
        out_specs=pl.BlockSpec((ATT_BLOCK, ATT_Q_W), cur),
        out_shape=jax.ShapeDtypeStruct((TOKENS, ATT_Q_W), BF16),
        compiler_params=_cparams(("arbitrary", "arbitrary")),
        name="swa_attention",
    )(sinks, aq, ak, ak, av, av, band_bias)


def _cumsum_rows(x, t):
    shift = 1
    while shift < x.shape[0]:
        x = x + jnp.where(t >= shift, pltpu.roll(x, shift, 0), 0.0)
        shift *= 2
    return x


def _hgrn_kernel(q_ref, f_ref, v_ref, g_ref, lbl_ref, ng_ref, o_ref, st_ref, *, layer):
    @pl.when(pl.program_id(1) == 0)
    def _():
        st_ref[...] = jnp.zeros_like(st_ref)

    rows = [lbl_ref[i:i + 1, :] for i in range(DEPTH)]
    mx = functools.reduce(jnp.maximum, rows)
    ex = [jnp.exp(rw - mx) for rw in rows]
    tot = functools.reduce(lambda a, b: a + b, ex)
    lower = jnp.zeros_like(mx)
    for i in range(1, layer + 1):
        lower = lower + ex[i] / tot

    c = HG_TILE
    t = lax.broadcasted_iota(jnp.int32, (c, HG_DK), 0)
    ts = lax.broadcasted_iota(jnp.int32, (c, c), 0)
    ss = lax.broadcasted_iota(jnp.int32, (c, c), 1)
    same64 = (ts >> 6) == (ss >> 6)
    diag32 = ((ts >> 5) == (ss >> 5)) & (ss <= ts)
    ng = ng_ref[...]

    for h in range(HG_HEADS):
        sl = slice(h * HG_DK, (h + 1) * HG_DK)
        lb = lower[:, sl]
        f = lb + (1.0 - lb) * _sigmoid(f_ref[:, sl])
        k = 1.0 - f
        a = _cumsum_rows(jnp.log(f), t)
        a_last = a[c - 1:c, :]
        q = q_ref[:, sl].astype(F32)
        v = v_ref[:, sl]
        st = st_ref[h]

        o = _dot_nt((q * jnp.exp(a)).astype(BF16), st.astype(BF16))
        ke = (k * jnp.exp(a_last - a)).astype(BF16)
        vt = v.astype(F32).T.astype(BF16)
        st_ref[h] = st * jnp.exp(a_last) + _dot(vt, ke)

        m1 = a[63:64, :]
        q1 = jnp.where(t >= 64, q * jnp.exp(a - m1), 0.0)
        k1 = jnp.where(t < 64, k * jnp.exp(m1 - a), 0.0)
        p = _dot_nt(q1.astype(BF16), k1.astype(BF16))

        m2 = jnp.where(t < 64, a[31:32, :], a[95:96, :])
        second = (t & 63) >= 32
        q2 = jnp.where(second, q * jnp.exp(a - m2), 0.0)
        k2 = jnp.where(second, 0.0, k * jnp.exp(m2 - a))
        p = p + jnp.where(same64, _dot_nt(q2.astype(BF16), k2.astype(BF16)), 0.0)

        m3 = jnp.where(t < 64,
                       jnp.where(t < 32, a[15:16, :], a[47:48, :]),
                       jnp.where(t < 96, a[79:80, :], a[111:112, :]))
        q3 = q * jnp.exp(a - m3)
        k3 = k * jnp.exp(m3 - a)
        p = p + jnp.where(diag32, _dot_nt(q3.astype(BF16), k3.astype(BF16)), 0.0)

        o = o + _dot(p.astype(BF16), v)
        o = o * lax.rsqrt(jnp.mean(o * o, axis=-1, keepdims=True) + EPS) * ng
        gate = g_ref[:, sl].astype(F32)
        o_ref[:, sl] = (o * (gate * _sigmoid(gate))).astype(o_ref.dtype)


def _hgrn(hq, hf, hi, hg, lb_logits, norm_g, layer):
    nt = SEQ // HG_TILE
    blk = pl.BlockSpec((HG_TILE, HG_K_W), lambda b, s: (b * nt + s, 0))
    return pl.pallas_call(
        functools.partial(_hgrn_kernel, layer=layer),
        grid=(BATCH, nt),
        in_specs=[blk, blk, blk, blk, _resident((DEPTH, HG_K_W)), _resident((1, HG_DV))],
        out_specs=blk,
        out_shape=jax.ShapeDtypeStruct((TOKENS, HG_V_W), BF16),
        scratch_shapes=[pltpu.VMEM((HG_HEADS, HG_DV, HG_DK), F32)],
        compiler_params=_cparams(("arbitrary", "arbitrary")),
        name="hgrn2",
    )(hq, hf, hi, hg, lb_logits, norm_g)


def _merge_kernel(att_ref, hgo_ref, ga_ref, gb_ref, h_ref, wa_ref, wb_ref, wo_ref, *rest, moe):
    ya = _dot(att_ref[...], wa_ref[...])
    yb = _dot(hgo_ref[...], wb_ref[...])
    merged = _sigmoid(ga_ref[...].astype(F32)) * ya + _sigmoid(gb_ref[...].astype(F32)) * yb
    hn = h_ref[...] + _dot(merged.astype(BF16), wo_ref[...])
    if not moe:
        (ho_ref,) = rest
        ho_ref[...] = hn
        return
    gn_ref, rw_ref, ho_ref, u_ref, idx_ref, gate_ref = rest
    ho_ref[...] = hn
    u = _rms(hn, gn_ref[...])
    bits = pltpu.bitcast(u.astype(BF16).astype(F32), jnp.uint32)
    u_ref[...] = (bits[:, :D_MODEL // 2] >> 16) | (bits[:, D_MODEL // 2:] & jnp.uint32(0xFFFF0000))
    ls = [jnp.sum(u * rw_ref[e:e + 1, :], axis=-1, keepdims=True) for e in range(N_EXPERTS)]
    m1 = functools.reduce(jnp.maximum, ls)
    i1 = jnp.full(m1.shape, N_EXPERTS, jnp.int32)
    for e in reversed(range(N_EXPERTS)):
        i1 = jnp.where(ls[e] == m1, e, i1)
    rest_ls = [jnp.where(i1 == e, -jnp.inf, ls[e]) for e in range(N_EXPERTS)]
    m2 = functools.reduce(jnp.maximum, rest_ls)
    i2 = jnp.full(m1.shape, N_EXPERTS, jnp.int32)
    for e in reversed(range(N_EXPERTS)):
        i2 = jnp.where((rest_ls[e] == m2) & (i1 != e), e, i2)
    e2 = jnp.exp(m2 - m1)
    den = 1.0 + e2
    idx_ref[...] = jnp.concatenate([i1, i2], axis=1)
    gate_ref[...] = jnp.concatenate([1.0 / den, e2 / den], axis=1)


def _merge(att, hgo, ga, gb, h, wa, wb, wo, gn=None, rw=None):
    moe = gn is not None
    row = lambda width: pl.BlockSpec((ROW_TILE, width), lambda i: (i, 0))
    in_specs = [row(ATT_Q_W), row(HG_V_W), row(D_MODEL), row(D_MODEL), row(D_MODEL),
                _resident((ATT_Q_W, D_MODEL)), _resident((HG_V_W, D_MODEL)),
                _resident((D_MODEL, D_MODEL))]
    args = [att, hgo, ga, gb, h, wa, wb, wo]
    out_specs = [row(D_MODEL)]
    out_shape = [jax.ShapeDtypeStruct((TOKENS, D_MODEL), F32)]
    if moe:
        in_specs += [_resident((1, D_MODEL)), _resident((N_EXPERTS, D_MODEL))]
        args += [gn, rw]
        out_specs += [row(D_MODEL // 2), row(TOP_K), row(TOP_K)]
        out_shape += [jax.ShapeDtypeStruct((TOKENS, D_MODEL // 2), jnp.uint32),
                      jax.ShapeDtypeStruct((TOKENS, TOP_K), jnp.int32),
                      jax.ShapeDtypeStruct((TOKENS, TOP_K), F32)]
    return pl.pallas_call(
        functools.partial(_merge_kernel, moe=moe),
        grid=(TOKENS // ROW_TILE,),
        in_specs=in_specs,
        out_specs=out_specs,
        out_shape=out_shape,
        compiler_params=_cparams(("arbitrary",)),
        name="merge_moe" if moe else "merge",
    )(*args)


def _dense_ffn_kernel(h_ref, g_ref, w1_ref, w3_ref, w2_ref, o_ref, x_scr, acc_scr):
    f = pl.program_id(1)

    @pl.when(f == 0)
    def _():
        x_scr[...] = _rms(h_ref[...], g_ref[...]).astype(BF16)
        acc_scr[...] = jnp.zeros_like(acc_scr)

    x = x_scr[...]
    a = _dot(x, w1_ref[...])
    b = _dot(x, w3_ref[...])
    acc_scr[...] += _dot((a * _sigmoid(a) * b).astype(BF16), w2_ref[...])

    @pl.when(f == pl.num_programs(1) - 1)
    def _():
        o_ref[...] = h_ref[...] + acc_scr[...]


def _dense_ffn(h, g, w1, w3, w2):
    nf = FFN_DENSE // DENSE_F_TILE
    return pl.pallas_call(
        _dense_ffn_kernel,
        grid=(TOKENS // ROW_TILE, nf),
        in_specs=[
            pl.BlockSpec((ROW_TILE, D_MODEL), lambda i, f: (i, 0)),
            _resident((1, D_MODEL)),
            pl.BlockSpec((D_MODEL, DENSE_F_TILE), lambda i, f: (0, f)),
            pl.BlockSpec((D_MODEL, DENSE_F_TILE), lambda i, f: (0, f)),
            pl.BlockSpec((DENSE_F_TILE, D_MODEL), lambda i, f: (f, 0)),
        ],
        out_specs=pl.BlockSpec((ROW_TILE, D_MODEL), lambda i, f: (i, 0)),
        out_shape=jax.ShapeDtypeStruct((TOKENS, D_MODEL), F32),
        scratch_shapes=[pltpu.VMEM((ROW_TILE, D_MODEL), BF16), pltpu.VMEM((ROW_TILE, D_MODEL), F32)],
        compiler_params=_cparams(("arbitrary", "arbitrary")),
        name="dense_ffn",
    )(h, g, w1, w3, w2)


def _dispatch_kernel(pos_ref, u_ref, xs_in_ref, xs_ref, sem):
    del xs_in_ref

    def start(r, carry):
        for k in range(TOP_K):
            d = pos_ref[0, 0, TOP_K * r + k]
            pltpu.make_async_copy(u_ref.at[pl.ds(r, 1)], xs_ref.at[pl.ds(d, 1)], sem).start()
        return carry
    lax.fori_loop(0, DISPATCH_TILE, start, 0, unroll=8)
    for k in range(TOP_K):
        pltpu.make_async_copy(u_ref, xs_ref.at[pl.ds(0, DISPATCH_TILE)], sem).wait()


def _dispatch(u_packed, pos):
    n = TOKENS // DISPATCH_TILE
    xs0 = jnp.zeros((MOE_ROWS, D_MODEL // 2), jnp.uint32)
    return pl.pallas_call(
        _dispatch_kernel,
        grid=(n,),
        in_specs=[
            pl.BlockSpec((1, 1, TOP_K * DISPATCH_TILE), lambda i: (i, 0, 0), memory_space=pltpu.SMEM),
            pl.BlockSpec((DISPATCH_TILE, D_MODEL // 2), lambda i: (i, 0)),
            pl.BlockSpec(memory_space=pl.ANY),
        ],
        out_specs=pl.BlockSpec(memory_space=pl.ANY),
        out_shape=jax.ShapeDtypeStruct((MOE_ROWS, D_MODEL // 2), jnp.uint32),
        scratch_shapes=[pltpu.SemaphoreType.DMA(())],
        input_output_aliases={2: 0},
        compiler_params=_cparams(("arbitrary",)),
        name="moe_dispatch",
    )(pos.reshape(n, 1, TOP_K * DISPATCH_TILE), u_packed, xs0)


def _moe_kernel(te_ref, nu_ref, x_ref, w1_ref, w3_ref, w2_ref, y_ref, xb, acc):
    i = pl.program_id(0)
    f = pl.program_id(1)
    valid = i < nu_ref[0]

    @pl.when((f == 0) & valid)
    def _():
        x = x_ref[...]
        half = D_MODEL // 2
        xb[:, :half] = pltpu.bitcast(x << 16, F32).astype(BF16)
        xb[:, half:] = pltpu.bitcast(x & jnp.uint32(0xFFFF0000), F32).astype(BF16)
        acc[...] = jnp.zeros_like(acc)

    @pl.when(valid)
    def _():
        x = xb[...]
        a = _dot(x, w1_ref[0])
        b = _dot(x, w3_ref[0])
        acc[...] += _dot((a * _sigmoid(a) * b).astype(BF16), w2_ref[0])

    @pl.when((f == MOE_NF - 1) & valid)
    def _():
        y_ref[...] = acc[...]

    @pl.when((f == MOE_NF - 1) & jnp.logical_not(valid))
    def _():
        y_ref[...] = jnp.zeros_like(y_ref)


def _moe_experts(xs, tile_expert, n_used, w1, w3, w2):
    row_blk = lambda i, f, te, nu: (jnp.minimum(i, nu[0] - 1), 0)
    f_blk = lambda i, f, nu: jnp.where(i < nu[0], f, MOE_NF - 1)
    grid_spec = pltpu.PrefetchScalarGridSpec(
        num_scalar_prefetch=2,
        grid=(MOE_TILES, MOE_NF),
        in_specs=[
            pl.BlockSpec((MOE_ROW_TILE, D_MODEL // 2), row_blk),
            pl.BlockSpec((1, D_MODEL, MOE_F_TILE), lambda i, f, te, nu: (te[i], 0, f_blk(i, f, nu))),
            pl.BlockSpec((1, D_MODEL, MOE_F_TILE), lambda i, f, te, nu: (te[i], 0, f_blk(i, f, nu))),
            pl.BlockSpec((1, MOE_F_TILE, D_MODEL), lambda i, f, te, nu: (te[i], f_blk(i, f, nu), 0)),
        ],
        out_specs=pl.BlockSpec((MOE_ROW_TILE, D_MODEL), lambda i, f, te, nu: (i, 0)),
        scratch_shapes=[
            pltpu.VMEM((MOE_ROW_TILE, D_MODEL), BF16),
            pltpu.VMEM((MOE_ROW_TILE, D_MODEL), F32),
        ],
    )
    return pl.pallas_call(
        _moe_kernel,
        grid_spec=grid_spec,
        out_shape=jax.ShapeDtypeStruct((MOE_ROWS, D_MODEL), F32),
        compiler_params=_cparams(("arbitrary", "arbitrary")),
        name="moe_experts",
    )(tile_expert, n_used, xs, w1, w3, w2)


def _moe_plan(idx):
    e_flat = idx.reshape(-1)
    onehot = (e_flat[:, None] == jnp.arange(N_EXPERTS, dtype=jnp.int32)[None, :]).astype(jnp.int32)
    csum = jnp.cumsum(onehot, axis=0)
    counts = csum[-1]
    rank = jnp.sum((csum - onehot) * onehot, axis=1)
    tiles_per = (counts + MOE_ROW_TILE - 1) // MOE_ROW_TILE
    tile_end = jnp.cumsum(tiles_per)
    tile_start = tile_end - tiles_per
    pos = jnp.sum(onehot * tile_start[None, :], axis=1) * MOE_ROW_TILE + rank
    tile_id = jnp.arange(MOE_TILES, dtype=jnp.int32)
    te = jnp.sum((tile_id[:, None] >= tile_end[None, :]).astype(jnp.int32), axis=1)
    n_used = tile_end[-1]
    te = jnp.minimum(te, N_EXPERTS - 1)
    te = jnp.where(tile_id < n_used, te, te[jnp.maximum(n_used - 1, 0)])
    return pos.astype(jnp.int32), te.astype(jnp.int32), n_used.reshape(1).astype(jnp.int32)


def _ple_kernel(*refs, moe, final):
    refs = list(refs)
    if moe:
        pos_ref, posn_ref, gate_ref, y_hbm = refs[:4]
        refs = refs[4:]
    h_ref, p_ref, g_ref, wg_ref, wp_ref = refs[:5]
    refs = refs[5:]
    fn_ref = refs.pop(0) if final else None
    o_ref = refs.pop(0)
    h = h_ref[...]
    if moe:
        yg, sems = refs
        i = pl.program_id(0)
        n = pl.num_programs(0)

        def gather(idx_ref, slot):
            def start(r, carry):
                for k in range(TOP_K):
                    s = idx_ref[0, 0, TOP_K * r + k]
                    pltpu.make_async_copy(y_hbm.at[pl.ds(s, 1)], yg.at[slot, k, pl.ds(r, 1)],
                                          sems.at[slot]).start()
                return carry
            lax.fori_loop(0, ROW_TILE, start, 0, unroll=8)

        @pl.when(i == 0)
        def _():
            gather(pos_ref, 0)

        @pl.when(i + 1 < n)
        def _():
            gather(posn_ref, (i + 1) % 2)

        slot = i % 2
        for k in range(TOP_K):
            pltpu.make_async_copy(y_hbm.at[pl.ds(0, ROW_TILE)], yg.at[slot, k], sems.at[slot]).wait()
        gate = gate_ref[...]
        for k in range(TOP_K):
            h = h + gate[:, k:k + 1] * yg[slot, k]
    u = _rms(h, g_ref[...]).astype(BF16)
    emb_gate = _sigmoid(_dot(u, wg_ref[...]))
    h = h + emb_gate * _dot(p_ref[...].astype(BF16), wp_ref[...])
    if final:
        h = _rms(h, fn_ref[...])
    o_ref[...] = h


def _ple(h, moe_in, p, g, wg, wp, final_g):
    moe = moe_in is not None
    final = final_g is not None
    n = TOKENS // ROW_TILE
    row = lambda width: pl.BlockSpec((ROW_TILE, width), lambda i: (i, 0))
    in_specs, args, scratch = [], [], []
    if moe:
        pos, gate, y = moe_in
        pos3 = pos.reshape(n, 1, TOP_K * ROW_TILE)
        smem = lambda imap: pl.BlockSpec((1, 1, TOP_K * ROW_TILE), imap, memory_space=pltpu.SMEM)
        in_specs += [smem(lambda i: (i, 0, 0)), smem(lambda i: (jnp.minimum(i + 1, n - 1), 0, 0)),
                     row(TOP_K), pl.BlockSpec(memory_space=pl.ANY)]
        args += [pos3, pos3, gate, y]
        scratch = [pltpu.VMEM((2, TOP_K, ROW_TILE, D_MODEL), F32), pltpu.SemaphoreType.DMA((2,))]
    in_specs += [row(D_MODEL), row(PLE_DIM), _resident((1, D_MODEL)), _resident((D_MODEL, D_MODEL)),
                 _resident((PLE_DIM, D_MODEL))]
    args += [h, p, g, wg, wp]
    if final:
        in_specs.append(_resident((1, D_MODEL)))
        args.append(final_g)
    return pl.pallas_call(
        functools.partial(_ple_kernel, moe=moe, final=final),
        grid=(n,),
        in_specs=in_specs,
        out_specs=row(D_MODEL),
        out_shape=jax.ShapeDtypeStruct((TOKENS, D_MODEL), F32),
        scratch_shapes=scratch,
        compiler_params=_cparams(("arbitrary",)),
        name="ple_moe" if moe else "ple",
    )(*args)


def _t5_bucket(dist):
    max_exact = REL_BUCKETS // 2
    d = jnp.maximum(dist, 0)
    large = max_exact + (jnp.log(jnp.maximum(d, 1).astype(jnp.float32) / max_exact)
                         / math.log(REL_MAX_DIST / max_exact)
                         * (REL_BUCKETS - max_exact)).astype(jnp.int32)
    large = jnp.minimum(large, REL_BUCKETS - 1)
    return jnp.where(d < max_exact, d, large)


def kernel(x, p, w_in, sinks, rel_bias, lb_logits, hgrn_norm, w_branch_a, w_branch_b, w_out,
           norm_mix, norm_ffn, norm_ple, dense_w1, dense_w3, dense_w2, router_w, moe_w1,
           moe_w3, moe_w2, ple_proj, ple_gate, final_norm):
    qi = jnp.arange(ATT_BLOCK)[:, None]
    kj = jnp.arange(2 * ATT_BLOCK)[None, :]
    band_bias = jnp.transpose(rel_bias[_t5_bucket(qi + ATT_BLOCK - kj)], (2, 0, 1)).astype(F32)

    bf = lambda w: w.astype(BF16)
    h = x.reshape(TOKENS, D_MODEL)
    pt = p.reshape(DEPTH, TOKENS, PLE_DIM)
    vec = lambda g: g.reshape(1, -1)

    for l in range(DEPTH):
        moe = l % 2 == 1
        aq, ak, av, hq, hf, hi, hg, ga, gb = _in_proj(h, vec(norm_mix[l]), bf(w_in[l]))
        att = _attention(aq, ak, av, sinks[l], band_bias)
        hgo = _hgrn(hq, hf, hi, hg, lb_logits, vec(hgrn_norm[l]), l)
        wa, wb, wo = bf(w_branch_a[l]), bf(w_branch_b[l]), bf(w_out[l])
        final_g = vec(final_norm) if l == DEPTH - 1 else None
        if not moe:
            (h,) = _merge(att, hgo, ga, gb, h, wa, wb, wo)
            h = _dense_ffn(h, vec(norm_ffn[l]), bf(dense_w1[l // 2]), bf(dense_w3[l // 2]),
                           bf(dense_w2[l // 2]))
            y = None
        else:
            h, u, idx, gate = _merge(att, hgo, ga, gb, h, wa, wb, wo,
                                     vec(norm_ffn[l]), router_w[l // 2].T)
            pos, tile_expert, n_used = _moe_plan(idx)
            xs = _dispatch(u, pos)
            y = _moe_experts(xs, tile_expert, n_used, bf(moe_w1[l // 2]), bf(moe_w3[l // 2]),
                             bf(moe_w2[l // 2]))
            y = (pos, gate, y)
        h = _ple(h, y, pt[l], vec(norm_ple[l]), bf(ple_gate[l]), bf(ple_proj[l]), final_g)
    return h.reshape(BATCH, SEQ, D_MODEL)
```

```python
import functools
import math

import jax
import jax.numpy as jnp
from jax import lax
from jax.experimental import pallas as pl
from jax.experimental.pallas import tpu as pltpu

F32 = jnp.float32
BF16 = jnp.bfloat16

D_MODEL = 1024
BATCH = 4
SEQ = 4096
TOKENS = BATCH * SEQ
DEPTH = 4
ATT_Q_HEADS = 8
ATT_KV_HEADS = 2
ATT_HEAD_DIM = 64
ATT_GROUP = ATT_Q_HEADS // ATT_KV_HEADS
WINDOW = 128
ATT_BLOCK = 128
REL_BUCKETS = 32
REL_MAX_DIST = 128
HG_HEADS = 4
HG_DK = 128
HG_DV = 128
ATT_Q_W = ATT_Q_HEADS * ATT_HEAD_DIM
ATT_KV_W = ATT_KV_HEADS * ATT_HEAD_DIM
HG_K_W = HG_HEADS * HG_DK
HG_V_W = HG_HEADS * HG_DV
IN_SPLITS = (ATT_Q_W, ATT_KV_W, ATT_KV_W, HG_K_W, HG_K_W, HG_V_W, HG_V_W, D_MODEL, D_MODEL)
IN_WIDTH = sum(IN_SPLITS)
FFN_DENSE = 2816
N_EXPERTS = 8
TOP_K = 2
FFN_EXPERT = 3584
PLE_DIM = 256
EPS = 1e-6

V7X_VMEM_LIMIT_BYTES = 52 * 1024 * 1024
ROW_TILE = 512
HG_TILE = 128
DENSE_F_TILE = FFN_DENSE // 2
MOE_ROW_TILE = 512
DISPATCH_TILE = 256
DISPATCH_STEPS = TOKENS // DISPATCH_TILE
DISPATCH_PARTS = 4
W13_ROWS = N_EXPERTS * D_MODEL // DISPATCH_STEPS
W2_ROWS = N_EXPERTS * FFN_EXPERT // DISPATCH_STEPS
MOE_F_TILE = FFN_EXPERT // 2
MOE_NF = FFN_EXPERT // MOE_F_TILE
MOE_TILES = (TOKENS * TOP_K) // MOE_ROW_TILE + N_EXPERTS - 1
MOE_ROWS = MOE_TILES * MOE_ROW_TILE
COL_CHUNK = 512


def _cparams(sem):
    return pltpu.CompilerParams(dimension_semantics=sem, vmem_limit_bytes=V7X_VMEM_LIMIT_BYTES)


def _rms(x, g):
    return x * lax.rsqrt(jnp.mean(x * x, axis=-1, keepdims=True) + EPS) * g


def _sigmoid(x):
    return 1.0 / (1.0 + jnp.exp(-x))


def _dot(a, b):
    return jnp.dot(a, b, preferred_element_type=F32)


def _dot_nt(a, b):
    return lax.dot_general(a, b, (((1,), (1,)), ((), ())), preferred_element_type=F32)


def _resident(shape):
    nd = len(shape)
    return pl.BlockSpec(shape, lambda *_: (0,) * nd)


IN_DTYPES = (BF16, BF16, BF16, BF16, F32, BF16, BF16, BF16, BF16)


def _in_proj_kernel(h_ref, g_ref, w_ref, *out_refs):
    xn = _rms(h_ref[...], g_ref[...]).astype(BF16)
    off = 0
    for o_ref, width in zip(out_refs, IN_SPLITS):
        for c in range(0, width, COL_CHUNK):
            cw = min(COL_CHUNK, width - c)
            o_ref[:, c:c + cw] = _dot(xn, w_ref[:, off + c:off + c + cw]).astype(o_ref.dtype)
        off += width


def _in_proj(h, g, w):
    row = lambda width: pl.BlockSpec((ROW_TILE, width), lambda i: (i, 0))
    return pl.pallas_call(
        _in_proj_kernel,
        grid=(TOKENS // ROW_TILE,),
        in_specs=[row(D_MODEL), _resident((1, D_MODEL)), _resident((D_MODEL, IN_WIDTH))],
        out_specs=[row(wd) for wd in IN_SPLITS],
        out_shape=[jax.ShapeDtypeStruct((TOKENS, wd), dt) for wd, dt in zip(IN_SPLITS, IN_DTYPES)],
        compiler_params=_cparams(("arbitrary",)),
        name="in_proj",
    )(h, g, w)


def _attn_kernel(sink_ref, q_ref, kc_ref, kp_ref, vc_ref, vp_ref, bias_ref, o_ref):
    n = pl.program_id(1)
    r = lax.broadcasted_iota(jnp.int32, (ATT_BLOCK, 2 * ATT_BLOCK), 0)
    j = lax.broadcasted_iota(jnp.int32, (ATT_BLOCK, 2 * ATT_BLOCK), 1)
    dist = r + ATT_BLOCK - j
    valid = (dist >= 0) & (dist < WINDOW) & ((j >= ATT_BLOCK) | (n > 0))
    kb = jnp.concatenate([kp_ref[...], kc_ref[...]], axis=0)
    vb = jnp.concatenate([vp_ref[...], vc_ref[...]], axis=0)
    scale = ATT_HEAD_DIM ** -0.5
    for h in range(ATT_Q_HEADS):
        g = h // ATT_GROUP
        hs = slice(h * ATT_HEAD_DIM, (h + 1) * ATT_HEAD_DIM)
        gs = slice(g * ATT_HEAD_DIM, (g + 1) * ATT_HEAD_DIM)
        s = _dot_nt(q_ref[:, hs], kb[:, gs]) * scale + bias_ref[h]
        s = jnp.where(valid, s, -1e30)
        sink = sink_ref[h]
        m = jnp.maximum(jnp.max(s, axis=-1, keepdims=True), sink)
        p = jnp.exp(s - m)
        denom = jnp.sum(p, axis=-1, keepdims=True) + jnp.exp(sink - m)
        o = _dot(p.astype(BF16), vb[:, gs]) / denom
        o_ref[:, hs] = o.astype(o_ref.dtype)


def _attention(aq, ak, av, sinks, band_bias):
    nb = SEQ // ATT_BLOCK
    cur = lambda b, n: (b * nb + n, 0)
    prev = lambda b, n: (jnp.maximum(b * nb + n - 1, 0), 0)
    return pl.pallas_call(
        _attn_kernel,
        grid=(BATCH, nb),
        in_specs=[
            pl.BlockSpec(memory_space=pltpu.SMEM),
            pl.BlockSpec((ATT_BLOCK, ATT_Q_W), cur),
            pl.BlockSpec((ATT_BLOCK, ATT_KV_W), cur),
            pl.BlockSpec((ATT_BLOCK, ATT_KV_W), prev),
            pl.BlockSpec((ATT_BLOCK, ATT_KV_W), cur),
            pl.BlockSpec((ATT_BLOCK, ATT_KV_W), prev),
            _resident((ATT_Q_HEADS, ATT_BLOCK, 2 * ATT_BLOCK)),
        ],
        out_specs=pl.BlockSpec((ATT_BLOCK, ATT_Q_W), cur),
        out_shape=jax.ShapeDtypeStruct((TOKENS, ATT_Q_W), BF16),
        compiler_params=_cparams(("arbitrary", "arbitrary")),
        name="swa_attention",
    )(sinks, aq, ak, ak, av, av, band_bias)


def _cumsum_rows(x, t):
    shift = 1
    while shift < x.shape[0]:
        x = x + jnp.where(t >= shift, pltpu.roll(x, shift, 0), 0.0)
        shift *= 2
    return x


def _hgrn_kernel(q_ref, f_ref, v_ref, g_ref, lbl_ref, ng_ref, o_ref, st_ref, *, layer):
    @pl.when(pl.program_id(1) == 0)
    def _():
        st_ref[...] = jnp.zeros_like(st_ref)

    rows = [lbl_ref[i:i + 1, :] for i in range(DEPTH)]
    mx = functools.reduce(jnp.maximum, rows)
    ex = [jnp.exp(rw - mx) for rw in rows]
    tot = functools.reduce(lambda a, b: a + b, ex)
    lower = jnp.zeros_like(mx)
    for i in range(1, layer + 1):
        lower = lower + ex[i] / tot

    c = HG_TILE
    t = lax.broadcasted_iota(jnp.int32, (c, HG_DK), 0)
    ts = lax.broadcasted_iota(jnp.int32, (c, c), 0)
    ss = lax.broadcasted_iota(jnp.int32, (c, c), 1)
    same64 = (ts >> 6) == (ss >> 6)
    diag32 = ((ts >> 5) == (ss >> 5)) & (ss <= ts)
    ng = ng_ref[...]

    for h in range(HG_HEADS):
        sl = slice(h * HG_DK, (h + 1) * HG_DK)
        lb = lower[:, sl]
        f = lb + (1.0 - lb) * _sigmoid(f_ref[:, sl])
        k = 1.0 - f
        a = _cumsum_rows(jnp.log(f), t)
        a_last = a[c - 1:c, :]
        q = q_ref[:, sl].astype(F32)
        v = v_ref[:, sl]
        st = st_ref[h]

        o = _dot_nt((q * jnp.exp(a)).astype(BF16), st.astype(BF16))
        ke = (k * jnp.exp(a_last - a)).astype(BF16)
        vt = v.astype(F32).T.astype(BF16)
        st_ref[h] = st * jnp.exp(a_last) + _dot(vt, ke)

        m1 = a[63:64, :]
        q1 = jnp.where(t >= 64, q * jnp.exp(a - m1), 0.0)
        k1 = jnp.where(t < 64, k * jnp.exp(m1 - a), 0.0)
        p = _dot_nt(q1.astype(BF16), k1.astype(BF16))

        m2 = jnp.where(t < 64, a[31:32, :], a[95:96, :])
        second = (t & 63) >= 32
        q2 = jnp.where(second, q * jnp.exp(a - m2), 0.0)
        k2 = jnp.where(second, 0.0, k * jnp.exp(m2 - a))
        p = p + jnp.where(same64, _dot_nt(q2.astype(BF16), k2.astype(BF16)), 0.0)

        m3 = jnp.where(t < 64,
                       jnp.where(t < 32, a[15:16, :], a[47:48, :]),
                       jnp.where(t < 96, a[79:80, :], a[111:112, :]))
        q3 = q * jnp.exp(a - m3)
        k3 = k * jnp.exp(m3 - a)
        p = p + jnp.where(diag32, _dot_nt(q3.astype(BF16), k3.astype(BF16)), 0.0)

        o = o + _dot(p.astype(BF16), v)
        o = o * lax.rsqrt(jnp.mean(o * o, axis=-1, keepdims=True) + EPS) * ng
        gate = g_ref[:, sl].astype(F32)
        o_ref[:, sl] = (o * (gate * _sigmoid(gate))).astype(o_ref.dtype)


def _hgrn(hq, hf, hi, hg, lb_logits, norm_g, layer):
    nt = SEQ // HG_TILE
    blk = pl.BlockSpec((HG_TILE, HG_K_W), lambda b, s: (b * nt + s, 0))
    return pl.pallas_call(
        functools.partial(_hgrn_kernel, layer=layer),
        grid=(BATCH, nt),
        in_specs=[blk, blk, blk, blk, _resident((DEPTH, HG_K_W)), _resident((1, HG_DV))],
        out_specs=blk,
        out_shape=jax.ShapeDtypeStruct((TOKENS, HG_V_W), BF16),
        scratch_shapes=[pltpu.VMEM((HG_HEADS, HG_DV, HG_DK), F32)],
        compiler_params=_cparams(("arbitrary", "arbitrary")),
        name="hgrn2",
    )(hq, hf, hi, hg, lb_logits, norm_g)


def _merge_kernel(att_ref, hgo_ref, ga_ref, gb_ref, h_ref, wa_ref, wb_ref, wo_ref, *rest, moe):
    ya = _dot(att_ref[...], wa_ref[...])
    yb = _dot(hgo_ref[...], wb_ref[...])
    merged = _sigmoid(ga_ref[...].astype(F32)) * ya + _sigmoid(gb_ref[...].astype(F32)) * yb
    hn = h_ref[...] + _dot(merged.astype(BF16), wo_ref[...])
    if not moe:
        (ho_ref,) = rest
        ho_ref[...] = hn
        return
    gn_ref, rw_ref, ho_ref, u_ref, idx_ref, gate_ref = rest
    ho_ref[...] = hn
    u = _rms(hn, gn_ref[...])
    bits = pltpu.bitcast(u.astype(BF16).astype(F32), jnp.uint32)
    u_ref[...] = (bits[:, :D_MODEL // 2] >> 16) | (bits[:, D_MODEL // 2:] & jnp.uint32(0xFFFF0000))
    ls = [jnp.sum(u * rw_ref[e:e + 1, :], axis=-1, keepdims=True) for e in range(N_EXPERTS)]
    m1 = functools.reduce(jnp.maximum, ls)
    i1 = jnp.full(m1.shape, N_EXPERTS, jnp.int32)
    for e in reversed(range(N_EXPERTS)):
        i1 = jnp.where(ls[e] == m1, e, i1)
    rest_ls = [jnp.where(i1 == e, -jnp.inf, ls[e]) for e in range(N_EXPERTS)]
    m2 = functools.reduce(jnp.maximum, rest_ls)
    i2 = jnp.full(m1.shape, N_EXPERTS, jnp.int32)
    for e in reversed(range(N_EXPERTS)):
        i2 = jnp.where((rest_ls[e] == m2) & (i1 != e), e, i2)
    e2 = jnp.exp(m2 - m1)
    den = 1.0 + e2
    idx_ref[...] = jnp.concatenate([i1, i2], axis=1)
    gate_ref[...] = jnp.concatenate([1.0 / den, e2 / den], axis=1)


def _merge(att, hgo, ga, gb, h, wa, wb, wo, gn=None, rw=None):
    moe = gn is not None
    row = lambda width: pl.BlockSpec((ROW_TILE, width), lambda i: (i, 0))
    in_specs = [row(ATT_Q_W), row(HG_V_W), row(D_MODEL), row(D_MODEL), row(D_MODEL),
                _resident((ATT_Q_W, D_MODEL)), _resident((HG_V_W, D_MODEL)),
                _resident((D_MODEL, D_MODEL))]
    args = [att, hgo, ga, gb, h, wa, wb, wo]
    out_specs = [row(D_MODEL)]
    out_shape = [jax.ShapeDtypeStruct((TOKENS, D_MODEL), F32)]
    if moe:
        in_specs += [_resident((1, D_MODEL)), _resident((N_EXPERTS, D_MODEL))]
        args += [gn, rw]
        out_specs += [row(D_MODEL // 2), row(TOP_K), row(TOP_K)]
        out_shape += [jax.ShapeDtypeStruct((TOKENS, D_MODEL // 2), jnp.uint32),
                      jax.ShapeDtypeStruct((TOKENS, TOP_K), jnp.int32),
                      jax.ShapeDtypeStruct((TOKENS, TOP_K), F32)]
    return pl.pallas_call(
        functools.partial(_merge_kernel, moe=moe),
        grid=(TOKENS // ROW_TILE,),
        in_specs=in_specs,
        out_specs=out_specs,
        out_shape=out_shape,
        compiler_params=_cparams(("arbitrary",)),
        name="merge_moe" if moe else "merge",
    )(*args)


def _dense_ffn_kernel(h_ref, g_ref, w1_ref, w3_ref, w2_ref, o_ref, x_scr, acc_scr):
    f = pl.program_id(1)

    @pl.when(f == 0)
    def _():
        x_scr[...] = _rms(h_ref[...], g_ref[...]).astype(BF16)
        acc_scr[...] = jnp.zeros_like(acc_scr)

    x = x_scr[...]
    a = _dot(x, w1_ref[...])
    b = _dot(x, w3_ref[...])
    acc_scr[...] += _dot((a * _sigmoid(a) * b).astype(BF16), w2_ref[...])

    @pl.when(f == pl.num_programs(1) - 1)
    def _():
        o_ref[...] = h_ref[...] + acc_scr[...]


def _dense_ffn(h, g, w1, w3, w2):
    nf = FFN_DENSE // DENSE_F_TILE
    return pl.pallas_call(
        _dense_ffn_kernel,
        grid=(TOKENS // ROW_TILE, nf),
        in_specs=[
            pl.BlockSpec((ROW_TILE, D_MODEL), lambda i, f: (i, 0)),
            _resident((1, D_MODEL)),
            pl.BlockSpec((D_MODEL, DENSE_F_TILE), lambda i, f: (0, f)),
            pl.BlockSpec((D_MODEL, DENSE_F_TILE), lambda i, f: (0, f)),
            pl.BlockSpec((DENSE_F_TILE, D_MODEL), lambda i, f: (f, 0)),
        ],
        out_specs=pl.BlockSpec((ROW_TILE, D_MODEL), lambda i, f: (i, 0)),
        out_shape=jax.ShapeDtypeStruct((TOKENS, D_MODEL), F32),
        scratch_shapes=[pltpu.VMEM((ROW_TILE, D_MODEL), BF16), pltpu.VMEM((ROW_TILE, D_MODEL), F32)],
        compiler_params=_cparams(("arbitrary", "arbitrary")),
        name="dense_ffn",
    )(h, g, w1, w3, w2)


def _dispatch_kernel(pos_ref, u_ref, w1_ref, w3_ref, w2_ref, xs_in_ref,
                     xs_ref, w1o_ref, w3o_ref, w2o_ref, sem):
    del xs_in_ref
    tok = DISPATCH_TILE // DISPATCH_PARTS
    r13 = W13_ROWS // DISPATCH_PARTS
    r2 = W2_ROWS // DISPATCH_PARTS
    for part in range(DISPATCH_PARTS):
        s13 = slice(part * r13, (part + 1) * r13)
        s2 = slice(part * r2, (part + 1) * r2)
        w1o_ref[s13, :] = w1_ref[s13, :].astype(BF16)
        w3o_ref[s13, :] = w3_ref[s13, :].astype(BF16)
        w2o_ref[s2, :] = w2_ref[s2, :].astype(BF16)
        for r in range(part * tok, (part + 1) * tok):
            for k in range(TOP_K):
                d = pos_ref[0, 0, TOP_K * r + k]
                pltpu.make_async_copy(u_ref.at[pl.ds(r, 1)], xs_ref.at[pl.ds(d, 1)], sem).start()
    for k in range(TOP_K):
        pltpu.make_async_copy(u_ref, xs_ref.at[pl.ds(0, DISPATCH_TILE)], sem).wait()


def _dispatch(u_packed, pos, w1, w3, w2):
    n = DISPATCH_STEPS
    xs0 = jnp.zeros((MOE_ROWS, D_MODEL // 2), jnp.uint32)
    w13_blk = pl.BlockSpec((W13_ROWS, FFN_EXPERT), lambda i: (i, 0))
    w2_blk = pl.BlockSpec((W2_ROWS, D_MODEL), lambda i: (i, 0))
    xs, w1b, w3b, w2b = pl.pallas_call(
        _dispatch_kernel,
        grid=(n,),
        in_specs=[
            pl.BlockSpec((1, 1, TOP_K * DISPATCH_TILE), lambda i: (i, 0, 0), memory_space=pltpu.SMEM),
            pl.BlockSpec((DISPATCH_TILE, D_MODEL // 2), lambda i: (i, 0)),
            w13_blk, w13_blk, w2_blk,
            pl.BlockSpec(memory_space=pl.ANY),
        ],
        out_specs=[pl.BlockSpec(memory_space=pl.ANY), w13_blk, w13_blk, w2_blk],
        out_shape=[jax.ShapeDtypeStruct((MOE_ROWS, D_MODEL // 2), jnp.uint32),
                   jax.ShapeDtypeStruct((N_EXPERTS * D_MODEL, FFN_EXPERT), BF16),
                   jax.ShapeDtypeStruct((N_EXPERTS * D_MODEL, FFN_EXPERT), BF16),
                   jax.ShapeDtypeStruct((N_EXPERTS * FFN_EXPERT, D_MODEL), BF16)],
        scratch_shapes=[pltpu.SemaphoreType.DMA(())],
        input_output_aliases={5: 0},
        compiler_params=_cparams(("arbitrary",)),
        name="moe_dispatch",
    )(pos.reshape(n, 1, TOP_K * DISPATCH_TILE), u_packed,
      w1.reshape(N_EXPERTS * D_MODEL, FFN_EXPERT), w3.reshape(N_EXPERTS * D_MODEL, FFN_EXPERT),
      w2.reshape(N_EXPERTS * FFN_EXPERT, D_MODEL), xs0)
    return (xs, w1b.reshape(N_EXPERTS, D_MODEL, FFN_EXPERT), w3b.reshape(N_EXPERTS, D_MODEL, FFN_EXPERT),
            w2b.reshape(N_EXPERTS, FFN_EXPERT, D_MODEL))


def _moe_kernel(te_ref, nu_ref, x_ref, w1_ref, w3_ref, w2_ref, y_ref, xb, acc):
    i = pl.program_id(0)
    f = pl.program_id(1)
    valid = i < nu_ref[0]

    @pl.when((f == 0) & valid)
    def _():
        x = x_ref[...]
        half = D_MODEL // 2
        xb[:, :half] = pltpu.bitcast(x << 16, F32).astype(BF16)
        xb[:, half:] = pltpu.bitcast(x & jnp.uint32(0xFFFF0000), F32).astype(BF16)
        acc[...] = jnp.zeros_like(acc)

    @pl.when(valid)
    def _():
        x = xb[...]
        a = _dot(x, w1_ref[0])
        b = _dot(x, w3_ref[0])
        acc[...] += _dot((a * _sigmoid(a) * b).astype(BF16), w2_ref[0])

    @pl.when((f == MOE_NF - 1) & valid)
    def _():
        y_ref[...] = acc[...]

    @pl.when((f == MOE_NF - 1) & jnp.logical_not(valid))
    def _():
        y_ref[...] = jnp.zeros_like(y_ref)


def _moe_experts(xs, tile_expert, n_used, w1, w3, w2):
    row_blk = lambda i, f, te, nu: (jnp.minimum(i, nu[0] - 1), 0)
    f_blk = lambda i, f, nu: jnp.where(i < nu[0], f, MOE_NF - 1)
    grid_spec = pltpu.PrefetchScalarGridSpec(
        num_scalar_prefetch=2,
        grid=(MOE_TILES, MOE_NF),
        in_specs=[
            pl.BlockSpec((MOE_ROW_TILE, D_MODEL // 2), row_blk),
            pl.BlockSpec((1, D_MODEL, MOE_F_TILE), lambda i, f, te, nu: (te[i], 0, f_blk(i, f, nu))),
            pl.BlockSpec((1, D_MODEL, MOE_F_TILE), lambda i, f, te, nu: (te[i], 0, f_blk(i, f, nu))),
            pl.BlockSpec((1, MOE_F_TILE, D_MODEL), lambda i, f, te, nu: (te[i], f_blk(i, f, nu), 0)),
        ],
        out_specs=pl.BlockSpec((MOE_ROW_TILE, D_MODEL), lambda i, f, te, nu: (i, 0)),
        scratch_shapes=[
            pltpu.VMEM((MOE_ROW_TILE, D_MODEL), BF16),
            pltpu.VMEM((MOE_ROW_TILE, D_MODEL), F32),
        ],
    )
    return pl.pallas_call(
        _moe_kernel,
        grid_spec=grid_spec,
        out_shape=jax.ShapeDtypeStruct((MOE_ROWS, D_MODEL), F32),
        compiler_params=_cparams(("arbitrary", "arbitrary")),
        name="moe_experts",
    )(tile_expert, n_used, xs, w1, w3, w2)


def _moe_plan(idx):
    e_flat = idx.reshape(-1)
    onehot = (e_flat[:, None] == jnp.arange(N_EXPERTS, dtype=jnp.int32)[None, :]).astype(jnp.int32)
    csum = jnp.cumsum(onehot, axis=0)
    counts = csum[-1]
    rank = jnp.sum((csum - onehot) * onehot, axis=1)
    tiles_per = (counts + MOE_ROW_TILE - 1) // MOE_ROW_TILE
    tile_end = jnp.cumsum(tiles_per)
    tile_start = tile_end - tiles_per
    pos = jnp.sum(onehot * tile_start[None, :], axis=1) * MOE_ROW_TILE + rank
    tile_id = jnp.arange(MOE_TILES, dtype=jnp.int32)
    te = jnp.sum((tile_id[:, None] >= tile_end[None, :]).astype(jnp.int32), axis=1)
    n_used = tile_end[-1]
    te = jnp.minimum(te, N_EXPERTS - 1)
    te = jnp.where(tile_id < n_used, te, te[jnp.maximum(n_used - 1, 0)])
    return pos.astype(jnp.int32), te.astype(jnp.int32), n_used.reshape(1).astype(jnp.int32)


def _ple_kernel(*refs, moe, final):
    refs = list(refs)
    if moe:
        pos_ref, posn_ref, gate_ref, y_hbm = refs[:4]
        refs = refs[4:]
    h_ref, p_ref, g_ref, wg_ref, wp_ref = refs[:5]
    refs = refs[5:]
    fn_ref = refs.pop(0) if final else None
    o_ref = refs.pop(0)
    h = h_ref[...]
    if moe:
        yg, sems = refs
        i = pl.program_id(0)
        n = pl.num_programs(0)

        def start_row(idx_ref, slot, r):
            for k in range(TOP_K):
                s = idx_ref[0, 0, TOP_K * r + k]
                pltpu.make_async_copy(y_hbm.at[pl.ds(s, 1)], yg.at[slot, k, pl.ds(r, 1)],
                                      sems.at[slot]).start()

        def wait_rows(slot):
            for k in range(TOP_K):
                pltpu.make_async_copy(y_hbm.at[pl.ds(0, ROW_TILE)], yg.at[slot, k], sems.at[slot]).wait()

        @pl.when(i == 0)
        def _():
            def body(r, carry):
                start_row(pos_ref, 0, r)
                return carry
            lax.fori_loop(0, ROW_TILE, body, 0, unroll=8)

        slot = i % 2
        wait_rows(slot)
        gate = gate_ref[...]
        for k in range(TOP_K):
            h = h + gate[:, k:k + 1] * yg[slot, k]
    u = _rms(h, g_ref[...]).astype(BF16)
    emb_gate = _sigmoid(_dot(u, wg_ref[...]))
    h = h + emb_gate * _dot(p_ref[...].astype(BF16), wp_ref[...])
    if final:
        h = _rms(h, fn_ref[...])
    o_ref[...] = h
    if moe:
        for r in range(ROW_TILE):
            start_row(posn_ref, (i + 1) % 2, r)

        @pl.when(i == n - 1)
        def _():
            wait_rows((i + 1) % 2)


def _ple(h, moe_in, p, g, wg, wp, final_g):
    moe = moe_in is not None
    final = final_g is not None
    n = TOKENS // ROW_TILE
    row = lambda width: pl.BlockSpec((ROW_TILE, width), lambda i: (i, 0))
    in_specs, args, scratch = [], [], []
    if moe:
        pos, gate, y = moe_in
        pos3 = pos.reshape(n, 1, TOP_K * ROW_TILE)
        smem = lambda imap: pl.BlockSpec((1, 1, TOP_K * ROW_TILE), imap, memory_space=pltpu.SMEM)
        in_specs += [smem(lambda i: (i, 0, 0)), smem(lambda i: (jnp.minimum(i + 1, n - 1), 0, 0)),
                     row(TOP_K), pl.BlockSpec(memory_space=pl.ANY)]
        args += [pos3, pos3, gate, y]
        scratch = [pltpu.VMEM((2, TOP_K, ROW_TILE, D_MODEL), F32), pltpu.SemaphoreType.DMA((2,))]
    in_specs += [row(D_MODEL), row(PLE_DIM), _resident((1, D_MODEL)), _resident((D_MODEL, D_MODEL)),
                 _resident((PLE_DIM, D_MODEL))]
    args += [h, p, g, wg, wp]
    if final:
        in_specs.append(_resident((1, D_MODEL)))
        args.append(final_g)
    return pl.pallas_call(
        functools.partial(_ple_kernel, moe=moe, final=final),
        grid=(n,),
        in_specs=in_specs,
        out_specs=row(D_MODEL),
        out_shape=jax.ShapeDtypeStruct((TOKENS, D_MODEL), F32),
        scratch_shapes=scratch,
        compiler_params=_cparams(("arbitrary",)),
        name="ple_moe" if moe else "ple",
    )(*args)


def _t5_bucket(dist):
    max_exact = REL_BUCKETS // 2
    d = jnp.maximum(dist, 0)
    large = max_exact + (jnp.log(jnp.maximum(d, 1).astype(jnp.float32) / max_exact)
                         / math.log(REL_MAX_DIST / max_exact)
                         * (REL_BUCKETS - max_exact)).astype(jnp.int32)
    large = jnp.minimum(large, REL_BUCKETS - 1)
    return jnp.where(d < max_exact, d, large)


def kernel(x, p, w_in, sinks, rel_bias, lb_logits, hgrn_norm, w_branch_a, w_branch_b, w_out,
           norm_mix, norm_ffn, norm_ple, dense_w1, dense_w3, dense_w2, router_w, moe_w1,
           moe_w3, moe_w2, ple_proj, ple_gate, final_norm):
    qi = jnp.arange(ATT_BLOCK)[:, None]
    kj = jnp.arange(2 * ATT_BLOCK)[None, :]
    bucket = _t5_bucket(qi + ATT_BLOCK - kj)[None]
    band_bias = jnp.zeros((ATT_Q_HEADS, ATT_BLOCK, 2 * ATT_BLOCK), F32)
    for b in range(REL_BUCKETS):
        band_bias = jnp.where(bucket == b, rel_bias[b].astype(F32)[:, None, None], band_bias)

    bf = lambda w: w.astype(BF16)
    h = x.reshape(TOKENS, D_MODEL)
    pt = p.reshape(DEPTH, TOKENS, PLE_DIM)
    vec = lambda g: g.reshape(1, -1)

    for l in range(DEPTH):
        moe = l % 2 == 1
        aq, ak, av, hq, hf, hi, hg, ga, gb = _in_proj(h, vec(norm_mix[l]), bf(w_in[l]))
        att = _attention(aq, ak, av, sinks[l], band_bias)
        hgo = _hgrn(hq, hf, hi, hg, lb_logits, vec(hgrn_norm[l]), l)
        wa, wb, wo = bf(w_branch_a[l]), bf(w_branch_b[l]), bf(w_out[l])
        final_g = vec(final_norm) if l == DEPTH - 1 else None
        if not moe:
            (h,) = _merge(att, hgo, ga, gb, h, wa, wb, wo)
            h = _dense_ffn(h, vec(norm_ffn[l]), bf(dense_w1[l // 2]), bf(dense_w3[l // 2]),
                           bf(dense_w2[l // 2]))
            y = None
        else:
            h, u, idx, gate = _merge(att, hgo, ga, gb, h, wa, wb, wo,
                                     vec(norm_ffn[l]), router_w[l // 2].T)
            pos, tile_expert, n_used = _moe_plan(idx)
            xs, w1b, w3b, w2b = _dispatch(u, pos, moe_w1[l // 2], moe_w3[l // 2], moe_w2[l // 2])
            y = _moe_experts(xs, tile_expert, n_used, w1b, w3b, w2b)
            y = (pos, gate, y)
        h = _ple(h, y, pt[l], vec(norm_ple[l]), bf(ple_gate[l]), bf(ple_proj[l]), final_g)
    return h.reshape(BATCH, SEQ, D_MODEL)
```

```python
import functools
import math

import jax
import jax.numpy as jnp
from jax import lax
from jax.experimental import pallas as pl
from jax.experimental.pallas import tpu as pltpu

F32 = jnp.float32
BF16 = jnp.bfloat16

D_MODEL = 1024
BATCH = 4
SEQ = 4096
TOKENS = BATCH * SEQ
DEPTH = 4
ATT_Q_HEADS = 8
ATT_KV_HEADS = 2
ATT_HEAD_DIM = 64
ATT_GROUP = ATT_Q_HEADS // ATT_KV_HEADS
WINDOW = 128
ATT_BLOCK = 128
REL_BUCKETS = 32
REL_MAX_DIST = 128
HG_HEADS = 4
HG_DK = 128
HG_DV = 128
ATT_Q_W = ATT_Q_HEADS * ATT_HEAD_DIM
ATT_KV_W = ATT_KV_HEADS * ATT_HEAD_DIM
HG_K_W = HG_HEADS * HG_DK
HG_V_W = HG_HEADS * HG_DV
IN_SPLITS = (ATT_Q_W, ATT_KV_W, ATT_KV_W, HG_K_W, HG_K_W, HG_V_W, HG_V_W, D_MODEL, D_MODEL)
IN_WIDTH = sum(IN_SPLITS)
FFN_DENSE = 2816
N_EXPERTS = 8
TOP_K = 2
FFN_EXPERT = 3584
PLE_DIM = 256
EPS = 1e-6

V7X_VMEM_LIMIT_BYTES = 52 * 1024 * 1024
ROW_TILE = 512
HG_TILE = 128
DENSE_F_TILE = FFN_DENSE // 2
MOE_ROW_TILE = 512
DISPATCH_TILE = 256
DISPATCH_STEPS = TOKENS // DISPATCH_TILE
DISPATCH_PARTS = 4
assert HG_TILE == ATT_BLOCK
MIXER_STEPS = TOKENS // ATT_BLOCK
W13_ROWS = N_EXPERTS * D_MODEL // MIXER_STEPS
W2_ROWS = N_EXPERTS * FFN_EXPERT // DISPATCH_STEPS
MOE_F_TILE = FFN_EXPERT // 2
MOE_NF = FFN_EXPERT // MOE_F_TILE
MOE_TILES = (TOKENS * TOP_K) // MOE_ROW_TILE + N_EXPERTS - 1
MOE_ROWS = MOE_TILES * MOE_ROW_TILE
COL_CHUNK = 512


def _cparams(sem):
    return pltpu.CompilerParams(dimension_semantics=sem, vmem_limit_bytes=V7X_VMEM_LIMIT_BYTES)


def _rms(x, g):
    return x * lax.rsqrt(jnp.mean(x * x, axis=-1, keepdims=True) + EPS) * g


def _sigmoid(x):
    return 1.0 / (1.0 + jnp.exp(-x))


def _dot(a, b):
    return jnp.dot(a, b, preferred_element_type=F32)


def _dot_nt(a, b):
    return lax.dot_general(a, b, (((1,), (1,)), ((), ())), preferred_element_type=F32)


def _resident(shape):
    nd = len(shape)
    return pl.BlockSpec(shape, lambda *_: (0,) * nd)


def _cast_rider(w2d, first_block, rows, n_steps, step_of):
    cols = w2d.shape[1]
    in_spec = pl.BlockSpec((rows, cols), lambda *ids: (first_block + step_of(*ids), 0))
    out_spec = pl.BlockSpec((rows, cols), lambda *ids: (step_of(*ids), 0))
    return in_spec, out_spec, jax.ShapeDtypeStruct((rows * n_steps, cols), BF16)


IN_DTYPES = (BF16, BF16, BF16, BF16, F32, BF16, BF16, BF16, BF16)


def _in_proj_kernel(h_ref, g_ref, w_ref, *out_refs):
    xn = _rms(h_ref[...], g_ref[...]).astype(BF16)
    off = 0
    for o_ref, width in zip(out_refs, IN_SPLITS):
        for c in range(0, width, COL_CHUNK):
            cw = min(COL_CHUNK, width - c)
            o_ref[:, c:c + cw] = _dot(xn, w_ref[:, off + c:off + c + cw]).astype(o_ref.dtype)
        off += width


def _in_proj(h, g, w):
    row = lambda width: pl.BlockSpec((ROW_TILE, width), lambda i: (i, 0))
    return pl.pallas_call(
        _in_proj_kernel,
        grid=(TOKENS // ROW_TILE,),
        in_specs=[row(D_MODEL), _resident((1, D_MODEL)), _resident((D_MODEL, IN_WIDTH))],
        out_specs=[row(wd) for wd in IN_SPLITS],
        out_shape=[jax.ShapeDtypeStruct((TOKENS, wd), dt) for wd, dt in zip(IN_SPLITS, IN_DTYPES)],
        compiler_params=_cparams(("arbitrary",)),
        name="in_proj",
    )(h, g, w)


def _attn_kernel(sink_ref, q_ref, kc_ref, kp_ref, vc_ref, vp_ref, bias_ref, *rest):
    if len(rest) == 3:
        wi_ref, o_ref, wo_ref = rest
        wo_ref[...] = wi_ref[...].astype(BF16)
    else:
        (o_ref,) = rest
    n = pl.program_id(1)
    r = lax.broadcasted_iota(jnp.int32, (ATT_BLOCK, 2 * ATT_BLOCK), 0)
    j = lax.broadcasted_iota(jnp.int32, (ATT_BLOCK, 2 * ATT_BLOCK), 1)
    dist = r + ATT_BLOCK - j
    valid = (dist >= 0) & (dist < WINDOW) & ((j >= ATT_BLOCK) | (n > 0))
    kb = jnp.concatenate([kp_ref[...], kc_ref[...]], axis=0)
    vb = jnp.concatenate([vp_ref[...], vc_ref[...]], axis=0)
    scale = ATT_HEAD_DIM ** -0.5
    for h in range(ATT_Q_HEADS):
        g = h // ATT_GROUP
        hs = slice(h * ATT_HEAD_DIM, (h + 1) * ATT_HEAD_DIM)
        gs = slice(g * ATT_HEAD_DIM, (g + 1) * ATT_HEAD_DIM)
        s = _dot_nt(q_ref[:, hs], kb[:, gs]) * scale + bias_ref[h]
        s = jnp.where(valid, s, -1e30)
        sink = sink_ref[h]
        m = jnp.maximum(jnp.max(s, axis=-1, keepdims=True), sink)
        p = jnp.exp(s - m)
        denom = jnp.sum(p, axis=-1, keepdims=True) + jnp.exp(sink - m)
        o = _dot(p.astype(BF16), vb[:, gs]) / denom
        o_ref[:, hs] = o.astype(o_ref.dtype)


def _attention(aq, ak, av, sinks, band_bias, rider=None):
    nb = SEQ // ATT_BLOCK
    cur = lambda b, n: (b * nb + n, 0)
    prev = lambda b, n: (jnp.maximum(b * nb + n - 1, 0), 0)
    in_specs = [
        pl.BlockSpec(memory_space=pltpu.SMEM),
        pl.BlockSpec((ATT_BLOCK, ATT_Q_W), cur),
        pl.BlockSpec((ATT_BLOCK, ATT_KV_W), cur),
        pl.BlockSpec((ATT_BLOCK, ATT_KV_W), prev),
        pl.BlockSpec((ATT_BLOCK, ATT_KV_W), cur),
        pl.BlockSpec((ATT_BLOCK, ATT_KV_W), prev),
        _resident((ATT_Q_HEADS, ATT_BLOCK, 2 * ATT_BLOCK)),
    ]
    args = [sinks, aq, ak, ak, av, av, band_bias]
    out_specs = [pl.BlockSpec((ATT_BLOCK, ATT_Q_W), cur)]
    out_shape = [jax.ShapeDtypeStruct((TOKENS, ATT_Q_W), BF16)]
    if rider is not None:
        w2d, first_block, rows = rider
        r_in, r_out, r_shape = _cast_rider(w2d, first_block, rows, BATCH * nb, lambda b, n: b * nb + n)
        in_specs.append(r_in)
        args.append(w2d)
        out_specs.append(r_out)
        out_shape.append(r_shape)
    return pl.pallas_call(
        _attn_kernel,
        grid=(BATCH, nb),
        in_specs=in_specs,
        out_specs=out_specs,
        out_shape=out_shape,
        compiler_params=_cparams(("arbitrary", "arbitrary")),
        name="swa_attention",
    )(*args)


def _cumsum_rows(x, t):
    shift = 1
    while shift < x.shape[0]:
        x = x + jnp.where(t >= shift, pltpu.roll(x, shift, 0), 0.0)
        shift *= 2
    return x


def _hgrn_kernel(q_ref, f_ref, v_ref, g_ref, lbl_ref, ng_ref, *rest, layer):
    if len(rest) == 4:
        wi_ref, o_ref, wo_ref, st_ref = rest
        wo_ref[...] = wi_ref[...].astype(BF16)
    else:
        o_ref, st_ref = rest

    @pl.when(pl.program_id(1) == 0)
    def _():
        st_ref[...] = jnp.zeros_like(st_ref)

    rows = [lbl_ref[i:i + 1, :] for i in range(DEPTH)]
    mx = functools.reduce(jnp.maximum, rows)
    ex = [jnp.exp(rw - mx) for rw in rows]
    tot = functools.reduce(lambda a, b: a + b, ex)
    lower = jnp.zeros_like(mx)
    for i in range(1, layer + 1):
        lower = lower + ex[i] / tot

    c = HG_TILE
    t = lax.broadcasted_iota(jnp.int32, (c, HG_DK), 0)
    ts = lax.broadcasted_iota(jnp.int32, (c, c), 0)
    ss = lax.broadcasted_iota(jnp.int32, (c, c), 1)
    same64 = (ts >> 6) == (ss >> 6)
    diag32 = ((ts >> 5) == (ss >> 5)) & (ss <= ts)
    ng = ng_ref[...]

    for h in range(HG_HEADS):
        sl = slice(h * HG_DK, (h + 1) * HG_DK)
        lb = lower[:, sl]
        f = lb + (1.0 - lb) * _sigmoid(f_ref[:, sl])
        k = 1.0 - f
        a = _cumsum_rows(jnp.log(f), t)
        a_last = a[c - 1:c, :]
        q = q_ref[:, sl].astype(F32)
        v = v_ref[:, sl]
        st = st_ref[h]

        o = _dot_nt((q * jnp.exp(a)).astype(BF16), st.astype(BF16))
        ke = (k * jnp.exp(a_last - a)).astype(BF16)
        vt = v.astype(F32).T.astype(BF16)
        st_ref[h] = st * jnp.exp(a_last) + _dot(vt, ke)

        m1 = a[63:64, :]
        q1 = jnp.where(t >= 64, q * jnp.exp(a - m1), 0.0)
        k1 = jnp.where(t < 64, k * jnp.exp(m1 - a), 0.0)
        p = _dot_nt(q1.astype(BF16), k1.astype(BF16))

        m2 = jnp.where(t < 64, a[31:32, :], a[95:96, :])
        second = (t & 63) >= 32
        q2 = jnp.where(second, q * jnp.exp(a - m2), 0.0)
        k2 = jnp.where(second, 0.0, k * jnp.exp(m2 - a))
        p = p + jnp.where(same64, _dot_nt(q2.astype(BF16), k2.astype(BF16)), 0.0)

        m3 = jnp.where(t < 64,
                       jnp.where(t < 32, a[15:16, :], a[47:48, :]),
                       jnp.where(t < 96, a[79:80, :], a[111:112, :]))
        q3 = q * jnp.exp(a - m3)
        k3 = k * jnp.exp(m3 - a)
        p = p + jnp.where(diag32, _dot_nt(q3.astype(BF16), k3.astype(BF16)), 0.0)

        o = o + _dot(p.astype(BF16), v)
        o = o * lax.rsqrt(jnp.mean(o * o, axis=-1, keepdims=True) + EPS) * ng
        gate = g_ref[:, sl].astype(F32)
        o_ref[:, sl] = (o * (gate * _sigmoid(gate))).astype(o_ref.dtype)


def _hgrn(hq, hf, hi, hg, lb_logits, norm_g, layer, rider=None):
    nt = SEQ // HG_TILE
    blk = pl.BlockSpec((HG_TILE, HG_K_W), lambda b, s: (b * nt + s, 0))
    in_specs = [blk, blk, blk, blk, _resident((DEPTH, HG_K_W)), _resident((1, HG_DV))]
    args = [hq, hf, hi, hg, lb_logits, norm_g]
    out_specs = [blk]
    out_shape = [jax.ShapeDtypeStruct((TOKENS, HG_V_W), BF16)]
    if rider is not None:
        w2d, first_block, rows = rider
        r_in, r_out, r_shape = _cast_rider(w2d, first_block, rows, BATCH * nt, lambda b, s: b * nt + s)
        in_specs.append(r_in)
        args.append(w2d)
        out_specs.append(r_out)
        out_shape.append(r_shape)
    return pl.pallas_call(
        functools.partial(_hgrn_kernel, layer=layer),
        grid=(BATCH, nt),
        in_specs=in_specs,
        out_specs=out_specs,
        out_shape=out_shape,
        scratch_shapes=[pltpu.VMEM((HG_HEADS, HG_DV, HG_DK), F32)],
        compiler_params=_cparams(("arbitrary", "arbitrary")),
        name="hgrn2",
    )(*args)


def _merge_kernel(att_ref, hgo_ref, ga_ref, gb_ref, h_ref, wa_ref, wb_ref, wo_ref, *rest, moe):
    ya = _dot(att_ref[...], wa_ref[...])
    yb = _dot(hgo_ref[...], wb_ref[...])
    merged = _sigmoid(ga_ref[...].astype(F32)) * ya + _sigmoid(gb_ref[...].astype(F32)) * yb
    hn = h_ref[...] + _dot(merged.astype(BF16), wo_ref[...])
    if not moe:
        (ho_ref,) = rest
        ho_ref[...] = hn
        return
    gn_ref, rw_ref, ho_ref, u_ref, idx_ref, gate_ref = rest
    ho_ref[...] = hn
    u = _rms(hn, gn_ref[...])
    bits = pltpu.bitcast(u.astype(BF16).astype(F32), jnp.uint32)
    u_ref[...] = (bits[:, :D_MODEL // 2] >> 16) | (bits[:, D_MODEL // 2:] & jnp.uint32(0xFFFF0000))
    ls = [jnp.sum(u * rw_ref[e:e + 1, :], axis=-1, keepdims=True) for e in range(N_EXPERTS)]
    m1 = functools.reduce(jnp.maximum, ls)
    i1 = jnp.full(m1.shape, N_EXPERTS, jnp.int32)
    for e in reversed(range(N_EXPERTS)):
        i1 = jnp.where(ls[e] == m1, e, i1)
    rest_ls = [jnp.where(i1 == e, -jnp.inf, ls[e]) for e in range(N_EXPERTS)]
    m2 = functools.reduce(jnp.maximum, rest_ls)
    i2 = jnp.full(m1.shape, N_EXPERTS, jnp.int32)
    for e in reversed(range(N_EXPERTS)):
        i2 = jnp.where((rest_ls[e] == m2) & (i1 != e), e, i2)
    e2 = jnp.exp(m2 - m1)
    den = 1.0 + e2
    idx_ref[...] = jnp.concatenate([i1, i2], axis=1)
    gate_ref[...] = jnp.concatenate([1.0 / den, e2 / den], axis=1)


def _merge(att, hgo, ga, gb, h, wa, wb, wo, gn=None, rw=None):
    moe = gn is not None
    row = lambda width: pl.BlockSpec((ROW_TILE, width), lambda i: (i, 0))
    in_specs = [row(ATT_Q_W), row(HG_V_W), row(D_MODEL), row(D_MODEL), row(D_MODEL),
                _resident((ATT_Q_W, D_MODEL)), _resident((HG_V_W, D_MODEL)),
                _resident((D_MODEL, D_MODEL))]
    args = [att, hgo, ga, gb, h, wa, wb, wo]
    out_specs = [row(D_MODEL)]
    out_shape = [jax.ShapeDtypeStruct((TOKENS, D_MODEL), F32)]
    if moe:
        in_specs += [_resident((1, D_MODEL)), _resident((N_EXPERTS, D_MODEL))]
        args += [gn, rw]
        out_specs += [row(D_MODEL // 2), row(TOP_K), row(TOP_K)]
        out_shape += [jax.ShapeDtypeStruct((TOKENS, D_MODEL // 2), jnp.uint32),
                      jax.ShapeDtypeStruct((TOKENS, TOP_K), jnp.int32),
                      jax.ShapeDtypeStruct((TOKENS, TOP_K), F32)]
    return pl.pallas_call(
        functools.partial(_merge_kernel, moe=moe),
        grid=(TOKENS // ROW_TILE,),
        in_specs=in_specs,
        out_specs=out_specs,
        out_shape=out_shape,
        compiler_params=_cparams(("arbitrary",)),
        name="merge_moe" if moe else "merge",
    )(*args)


def _dense_ffn_kernel(h_ref, g_ref, w1_ref, w3_ref, w2_ref, o_ref, x_scr, acc_scr):
    f = pl.program_id(1)

    @pl.when(f == 0)
    def _():
        x_scr[...] = _rms(h_ref[...], g_ref[...]).astype(BF16)
        acc_scr[...] = jnp.zeros_like(acc_scr)

    x = x_scr[...]
    a = _dot(x, w1_ref[...])
    b = _dot(x, w3_ref[...])
    acc_scr[...] += _dot((a * _sigmoid(a) * b).astype(BF16), w2_ref[...])

    @pl.when(f == pl.num_programs(1) - 1)
    def _():
        o_ref[...] = h_ref[...] + acc_scr[...]


def _dense_ffn(h, g, w1, w3, w2):
    nf = FFN_DENSE // DENSE_F_TILE
    return pl.pallas_call(
        _dense_ffn_kernel,
        grid=(TOKENS // ROW_TILE, nf),
        in_specs=[
            pl.BlockSpec((ROW_TILE, D_MODEL), lambda i, f: (i, 0)),
            _resident((1, D_MODEL)),
            pl.BlockSpec((D_MODEL, DENSE_F_TILE), lambda i, f: (0, f)),
            pl.BlockSpec((D_MODEL, DENSE_F_TILE), lambda i, f: (0, f)),
            pl.BlockSpec((DENSE_F_TILE, D_MODEL), lambda i, f: (f, 0)),
        ],
        out_specs=pl.BlockSpec((ROW_TILE, D_MODEL), lambda i, f: (i, 0)),
        out_shape=jax.ShapeDtypeStruct((TOKENS, D_MODEL), F32),
        scratch_shapes=[pltpu.VMEM((ROW_TILE, D_MODEL), BF16), pltpu.VMEM((ROW_TILE, D_MODEL), F32)],
        compiler_params=_cparams(("arbitrary", "arbitrary")),
        name="dense_ffn",
    )(h, g, w1, w3, w2)


def _dispatch_kernel(pos_ref, u_ref, w2_ref, xs_in_ref, xs_ref, w2o_ref, sem):
    del xs_in_ref
    tok = DISPATCH_TILE // DISPATCH_PARTS
    r2 = W2_ROWS // DISPATCH_PARTS
    for part in range(DISPATCH_PARTS):
        s2 = slice(part * r2, (part + 1) * r2)
        w2o_ref[s2, :] = w2_ref[s2, :].astype(BF16)
        for r in range(part * tok, (part + 1) * tok):
            for k in range(TOP_K):
                d = pos_ref[0, 0, TOP_K * r + k]
                pltpu.make_async_copy(u_ref.at[pl.ds(r, 1)], xs_ref.at[pl.ds(d, 1)], sem).start()
    for k in range(TOP_K):
        pltpu.make_async_copy(u_ref, xs_ref.at[pl.ds(0, DISPATCH_TILE)], sem).wait()


def _dispatch(u_packed, pos, w2_all, moe_layer):
    n = DISPATCH_STEPS
    xs0 = jnp.zeros((MOE_ROWS, D_MODEL // 2), jnp.uint32)
    w2_in, w2_out, w2_shape = _cast_rider(w2_all, moe_layer * n, W2_ROWS, n, lambda i: i)
    return pl.pallas_call(
        _dispatch_kernel,
        grid=(n,),
        in_specs=[
            pl.BlockSpec((1, 1, TOP_K * DISPATCH_TILE), lambda i: (i, 0, 0), memory_space=pltpu.SMEM),
            pl.BlockSpec((DISPATCH_TILE, D_MODEL // 2), lambda i: (i, 0)),
            w2_in,
            pl.BlockSpec(memory_space=pl.ANY),
        ],
        out_specs=[pl.BlockSpec(memory_space=pl.ANY), w2_out],
        out_shape=[jax.ShapeDtypeStruct((MOE_ROWS, D_MODEL // 2), jnp.uint32), w2_shape],
        scratch_shapes=[pltpu.SemaphoreType.DMA(())],
        input_output_aliases={3: 0},
        compiler_params=_cparams(("arbitrary",)),
        name="moe_dispatch",
    )(pos.reshape(n, 1, TOP_K * DISPATCH_TILE), u_packed, w2_all, xs0)


def _moe_kernel(te_ref, nu_ref, x_ref, w1_ref, w3_ref, w2_ref, y_ref, xb, acc):
    i = pl.program_id(0)
    f = pl.program_id(1)
    valid = i < nu_ref[0]

    @pl.when((f == 0) & valid)
    def _():
        x = x_ref[...]
        half = D_MODEL // 2
        xb[:, :half] = pltpu.bitcast(x << 16, F32).astype(BF16)
        xb[:, half:] = pltpu.bitcast(x & jnp.uint32(0xFFFF0000), F32).astype(BF16)
        acc[...] = jnp.zeros_like(acc)

    @pl.when(valid)
    def _():
        x = xb[...]
        a = _dot(x, w1_ref[0])
        b = _dot(x, w3_ref[0])
        acc[...] += _dot((a * _sigmoid(a) * b).astype(BF16), w2_ref[0])

    @pl.when((f == MOE_NF - 1) & valid)
    def _():
        y_ref[...] = acc[...]

    @pl.when((f == MOE_NF - 1) & jnp.logical_not(valid))
    def _():
        y_ref[...] = jnp.zeros_like(y_ref)


def _moe_experts(xs, tile_expert, n_used, w1, w3, w2):
    row_blk = lambda i, f, te, nu: (jnp.minimum(i, nu[0] - 1), 0)
    f_blk = lambda i, f, nu: jnp.where(i < nu[0], f, MOE_NF - 1)
    grid_spec = pltpu.PrefetchScalarGridSpec(
        num_scalar_prefetch=2,
        grid=(MOE_TILES, MOE_NF),
        in_specs=[
            pl.BlockSpec((MOE_ROW_TILE, D_MODEL // 2), row_blk),
            pl.BlockSpec((1, D_MODEL, MOE_F_TILE), lambda i, f, te, nu: (te[i], 0, f_blk(i, f, nu))),
            pl.BlockSpec((1, D_MODEL, MOE_F_TILE), lambda i, f, te, nu: (te[i], 0, f_blk(i, f, nu))),
            pl.BlockSpec((1, MOE_F_TILE, D_MODEL), lambda i, f, te, nu: (te[i], f_blk(i, f, nu), 0)),
        ],
        out_specs=pl.BlockSpec((MOE_ROW_TILE, D_MODEL), lambda i, f, te, nu: (i, 0)),
        scratch_shapes=[
            pltpu.VMEM((MOE_ROW_TILE, D_MODEL), BF16),
            pltpu.VMEM((MOE_ROW_TILE, D_MODEL), F32),
        ],
    )
    return pl.pallas_call(
        _moe_kernel,
        grid_spec=grid_spec,
        out_shape=jax.ShapeDtypeStruct((MOE_ROWS, D_MODEL), F32),
        compiler_params=_cparams(("arbitrary", "arbitrary")),
        name="moe_experts",
    )(tile_expert, n_used, xs, w1, w3, w2)


def _moe_plan(idx):
    e_flat = idx.reshape(-1)
    onehot = (e_flat[:, None] == jnp.arange(N_EXPERTS, dtype=jnp.int32)[None, :]).astype(jnp.int32)
    csum = jnp.cumsum(onehot, axis=0)
    counts = csum[-1]
    rank = jnp.sum((csum - onehot) * onehot, axis=1)
    tiles_per = (counts + MOE_ROW_TILE - 1) // MOE_ROW_TILE
    tile_end = jnp.cumsum(tiles_per)
    tile_start = tile_end - tiles_per
    pos = jnp.sum(onehot * tile_start[None, :], axis=1) * MOE_ROW_TILE + rank
    tile_id = jnp.arange(MOE_TILES, dtype=jnp.int32)
    te = jnp.sum((tile_id[:, None] >= tile_end[None, :]).astype(jnp.int32), axis=1)
    n_used = tile_end[-1]
    te = jnp.minimum(te, N_EXPERTS - 1)
    te = jnp.where(tile_id < n_used, te, te[jnp.maximum(n_used - 1, 0)])
    return pos.astype(jnp.int32), te.astype(jnp.int32), n_used.reshape(1).astype(jnp.int32)


def _ple_kernel(*refs, moe, final):
    refs = list(refs)
    if moe:
        pos_ref, posn_ref, gate_ref, y_hbm = refs[:4]
        refs = refs[4:]
    h_ref, p_ref, g_ref, wg_ref, wp_ref = refs[:5]
    refs = refs[5:]
    fn_ref = refs.pop(0) if final else None
    o_ref = refs.pop(0)
    h = h_ref[...]
    if moe:
        yg, sems = refs
        i = pl.program_id(0)
        n = pl.num_programs(0)

        def start_row(idx_ref, slot, r):
            for k in range(TOP_K):
                s = idx_ref[0, 0, TOP_K * r + k]
                pltpu.make_async_copy(y_hbm.at[pl.ds(s, 1)], yg.at[slot, k, pl.ds(r, 1)],
                                      sems.at[slot]).start()

        def wait_rows(slot):
            for k in range(TOP_K):
                pltpu.make_async_copy(y_hbm.at[pl.ds(0, ROW_TILE)], yg.at[slot, k], sems.at[slot]).wait()

        @pl.when(i == 0)
        def _():
            def body(r, carry):
                start_row(pos_ref, 0, r)
                return carry
            lax.fori_loop(0, ROW_TILE, body, 0, unroll=8)

        slot = i % 2
        wait_rows(slot)
        gate = gate_ref[...]
        for k in range(TOP_K):
            h = h + gate[:, k:k + 1] * yg[slot, k]
    u = _rms(h, g_ref[...]).astype(BF16)
    emb_gate = _sigmoid(_dot(u, wg_ref[...]))
    h = h + emb_gate * _dot(p_ref[...].astype(BF16), wp_ref[...])
    if final:
        h = _rms(h, fn_ref[...])
    o_ref[...] = h
    if moe:
        for r in range(ROW_TILE):
            start_row(posn_ref, (i + 1) % 2, r)

        @pl.when(i == n - 1)
        def _():
            wait_rows((i + 1) % 2)


def _ple(h, moe_in, p, g, wg, wp, final_g):
    moe = moe_in is not None
    final = final_g is not None
    n = TOKENS // ROW_TILE
    row = lambda width: pl.BlockSpec((ROW_TILE, width), lambda i: (i, 0))
    in_specs, args, scratch = [], [], []
    if moe:
        pos, gate, y = moe_in
        pos3 = pos.reshape(n, 1, TOP_K * ROW_TILE)
        smem = lambda imap: pl.BlockSpec((1, 1, TOP_K * ROW_TILE), imap, memory_space=pltpu.SMEM)
        in_specs += [smem(lambda i: (i, 0, 0)), smem(lambda i: (jnp.minimum(i + 1, n - 1), 0, 0)),
                     row(TOP_K), pl.BlockSpec(memory_space=pl.ANY)]
        args += [pos3, pos3, gate, y]
        scratch = [pltpu.VMEM((2, TOP_K, ROW_TILE, D_MODEL), F32), pltpu.SemaphoreType.DMA((2,))]
    in_specs += [row(D_MODEL), row(PLE_DIM), _resident((1, D_MODEL)), _resident((D_MODEL, D_MODEL)),
                 _resident((PLE_DIM, D_MODEL))]
    args += [h, p, g, wg, wp]
    if final:
        in_specs.append(_resident((1, D_MODEL)))
        args.append(final_g)
    return pl.pallas_call(
        functools.partial(_ple_kernel, moe=moe, final=final),
        grid=(n,),
        in_specs=in_specs,
        out_specs=row(D_MODEL),
        out_shape=jax.ShapeDtypeStruct((TOKENS, D_MODEL), F32),
        scratch_shapes=scratch,
        compiler_params=_cparams(("arbitrary",)),
        name="ple_moe" if moe else "ple",
    )(*args)


def _t5_bucket(dist):
    max_exact = REL_BUCKETS // 2
    d = jnp.maximum(dist, 0)
    large = max_exact + (jnp.log(jnp.maximum(d, 1).astype(jnp.float32) / max_exact)
                         / math.log(REL_MAX_DIST / max_exact)
                         * (REL_BUCKETS - max_exact)).astype(jnp.int32)
    large = jnp.minimum(large, REL_BUCKETS - 1)
    return jnp.where(d < max_exact, d, large)


def kernel(x, p, w_in, sinks, rel_bias, lb_logits, hgrn_norm, w_branch_a, w_branch_b, w_out,
           norm_mix, norm_ffn, norm_ple, dense_w1, dense_w3, dense_w2, router_w, moe_w1,
           moe_w3, moe_w2, ple_proj, ple_gate, final_norm):
    qi = jnp.arange(ATT_BLOCK)[:, None]
    kj = jnp.arange(2 * ATT_BLOCK)[None, :]
    bucket = _t5_bucket(qi + ATT_BLOCK - kj)[None]
    band_bias = jnp.zeros((ATT_Q_HEADS, ATT_BLOCK, 2 * ATT_BLOCK), F32)
    for b in range(REL_BUCKETS):
        band_bias = jnp.where(bucket == b, rel_bias[b].astype(F32)[:, None, None], band_bias)

    bf = lambda w: w.astype(BF16)
    h = x.reshape(TOKENS, D_MODEL)
    pt = p.reshape(DEPTH, TOKENS, PLE_DIM)
    vec = lambda g: g.reshape(1, -1)
    n_moe = moe_w1.shape[0]
    w1_all = moe_w1.reshape(n_moe * N_EXPERTS * D_MODEL, FFN_EXPERT)
    w3_all = moe_w3.reshape(n_moe * N_EXPERTS * D_MODEL, FFN_EXPERT)
    w2_all = moe_w2.reshape(n_moe * N_EXPERTS * FFN_EXPERT, D_MODEL)

    for l in range(DEPTH):
        moe = l % 2 == 1
        aq, ak, av, hq, hf, hi, hg, ga, gb = _in_proj(h, vec(norm_mix[l]), bf(w_in[l]))
        if moe:
            first = (l // 2) * MIXER_STEPS
            att, w1b = _attention(aq, ak, av, sinks[l], band_bias, (w1_all, first, W13_ROWS))
            hgo, w3b = _hgrn(hq, hf, hi, hg, lb_logits, vec(hgrn_norm[l]), l, (w3_all, first, W13_ROWS))
        else:
            (att,) = _attention(aq, ak, av, sinks[l], band_bias)
            (hgo,) = _hgrn(hq, hf, hi, hg, lb_logits, vec(hgrn_norm[l]), l)
        wa, wb, wo = bf(w_branch_a[l]), bf(w_branch_b[l]), bf(w_out[l])
        final_g = vec(final_norm) if l == DEPTH - 1 else None
        if not moe:
            (h,) = _merge(att, hgo, ga, gb, h, wa, wb, wo)
            h = _dense_ffn(h, vec(norm_ffn[l]), bf(dense_w1[l // 2]), bf(dense_w3[l // 2]),
                           bf(dense_w2[l // 2]))
            y = None
        else:
            h, u, idx, gate = _merge(att, hgo, ga, gb, h, wa, wb, wo,
                                     vec(norm_ffn[l]), router_w[l // 2].T)
            pos, tile_expert, n_used = _moe_plan(idx)
            xs, w2b = _dispatch(u, pos, w2_all, l // 2)
            y = _moe_experts(xs, tile_expert, n_used,
                             w1b.reshape(N_EXPERTS, D_MODEL, FFN_EXPERT),
                             w3b.reshape(N_EXPERTS, D_MODEL, FFN_EXPERT),
                             w2b.reshape(N_EXPERTS, FFN_EXPERT, D_MODEL))
            y = (pos, gate, y)
        h = _ple(h, y, pt[l], vec(norm_ple[l]), bf(ple_gate[l]), bf(ple_proj[l]), final_g)
    return h.reshape(BATCH, SEQ, D_MODEL)
```

```python
import functools
import math

import jax
import jax.numpy as jnp
from jax import lax
from jax.experimental import pallas as pl
from jax.experimental.pallas import tpu as pltpu

F32 = jnp.float32
BF16 = jnp.bfloat16

D_MODEL = 1024
BATCH = 4
SEQ = 4096
TOKENS = BATCH * SEQ
DEPTH = 4
ATT_Q_HEADS = 8
ATT_KV_HEADS = 2
ATT_HEAD_DIM = 64
ATT_GROUP = ATT_Q_HEADS // ATT_KV_HEADS
WINDOW = 128
ATT_BLOCK = 128
REL_BUCKETS = 32
REL_MAX_DIST = 128
HG_HEADS = 4
HG_DK = 128
HG_DV = 128
ATT_Q_W = ATT_Q_HEADS * ATT_HEAD_DIM
ATT_KV_W = ATT_KV_HEADS * ATT_HEAD_DIM
HG_K_W = HG_HEADS * HG_DK
HG_V_W = HG_HEADS * HG_DV
IN_SPLITS = (ATT_Q_W, ATT_KV_W, ATT_KV_W, HG_K_W, HG_K_W, HG_V_W, HG_V_W, D_MODEL, D_MODEL)
IN_WIDTH = sum(IN_SPLITS)
FFN_DENSE = 2816
N_EXPERTS = 8
TOP_K = 2
FFN_EXPERT = 3584
PLE_DIM = 256
EPS = 1e-6

V7X_VMEM_LIMIT_BYTES = 52 * 1024 * 1024
ROW_TILE = 512
HG_TILE = 128
DENSE_F_TILE = FFN_DENSE // 2
MOE_ROW_TILE = 512
DISPATCH_TILE = 512
DISPATCH_STEPS = TOKENS // DISPATCH_TILE
DISPATCH_PARTS = 8
assert HG_TILE == ATT_BLOCK
MIXER_STEPS = TOKENS // ATT_BLOCK
W13_ROWS = N_EXPERTS * D_MODEL // MIXER_STEPS
W2_ROWS = N_EXPERTS * FFN_EXPERT // DISPATCH_STEPS
MOE_F_TILE = FFN_EXPERT // 2
MOE_NF = FFN_EXPERT // MOE_F_TILE
MOE_TILES = (TOKENS * TOP_K) // MOE_ROW_TILE + N_EXPERTS - 1
MOE_ROWS = MOE_TILES * MOE_ROW_TILE
COL_CHUNK = 512


def _cparams(sem):
    return pltpu.CompilerParams(dimension_semantics=sem, vmem_limit_bytes=V7X_VMEM_LIMIT_BYTES)


def _rms(x, g):
    return x * lax.rsqrt(jnp.mean(x * x, axis=-1, keepdims=True) + EPS) * g


def _sigmoid(x):
    return 1.0 / (1.0 + jnp.exp(-x))


def _dot(a, b):
    return jnp.dot(a, b, preferred_element_type=F32)


def _dot_nt(a, b):
    return lax.dot_general(a, b, (((1,), (1,)), ((), ())), preferred_element_type=F32)


def _resident(shape):
    nd = len(shape)
    return pl.BlockSpec(shape, lambda *_: (0,) * nd)


def _cast_rider(w2d, first_block, rows, n_steps, step_of):
    cols = w2d.shape[1]
    in_spec = pl.BlockSpec((rows, cols), lambda *ids: (first_block + step_of(*ids), 0))
    out_spec = pl.BlockSpec((rows, cols), lambda *ids: (step_of(*ids), 0))
    return in_spec, out_spec, jax.ShapeDtypeStruct((rows * n_steps, cols), BF16)


IN_DTYPES = (BF16, BF16, BF16, BF16, F32, BF16, BF16, BF16, BF16)
ATT_SCALE = ATT_HEAD_DIM ** -0.5


def _in_proj_kernel(h_ref, g_ref, w_ref, *out_refs):
    xn = _rms(h_ref[...], g_ref[...]).astype(BF16)
    main_refs, (ak_sw_ref, av_sw_ref) = out_refs[:len(IN_SPLITS)], out_refs[len(IN_SPLITS):]
    off = 0
    for idx, (o_ref, width) in enumerate(zip(main_refs, IN_SPLITS)):
        for c in range(0, width, COL_CHUNK):
            cw = min(COL_CHUNK, width - c)
            z = _dot(xn, w_ref[:, off + c:off + c + cw])
            if idx == 0:
                z = z * ATT_SCALE
            o_ref[:, c:c + cw] = z.astype(o_ref.dtype)
            if idx in (1, 2):
                sw_ref = ak_sw_ref if idx == 1 else av_sw_ref
                sw_ref[...] = jnp.concatenate([z[:, ATT_HEAD_DIM:], z[:, :ATT_HEAD_DIM]], axis=1).astype(sw_ref.dtype)
        off += width


def _in_proj(h, g, w):
    row = lambda width: pl.BlockSpec((ROW_TILE, width), lambda i: (i, 0))
    widths = IN_SPLITS + (ATT_KV_W, ATT_KV_W)
    dtypes = IN_DTYPES + (BF16, BF16)
    return pl.pallas_call(
        _in_proj_kernel,
        grid=(TOKENS // ROW_TILE,),
        in_specs=[row(D_MODEL), _resident((1, D_MODEL)), _resident((D_MODEL, IN_WIDTH))],
        out_specs=[row(wd) for wd in widths],
        out_shape=[jax.ShapeDtypeStruct((TOKENS, wd), dt) for wd, dt in zip(widths, dtypes)],
        compiler_params=_cparams(("arbitrary",)),
        name="in_proj",
    )(h, g, w)


def _block_diag(a, a_sw, g):
    lane = lax.broadcasted_iota(jnp.int32, a.shape, 1)
    low = lane < ATT_HEAD_DIM
    own, other = (a, a_sw) if g == 0 else (a_sw, a)
    zero = jnp.zeros_like(a)
    first = jnp.where(low, own, zero)
    second = jnp.where(low, zero, other)
    cat = jnp.concatenate
    return cat([cat([first, zero], 1), cat([second, zero], 1),
                cat([zero, first], 1), cat([zero, second], 1)], 0)


def _attn_kernel(sink_ref, q_ref, kc_ref, kp_ref, kcs_ref, kps_ref, vc_ref, vp_ref, vcs_ref, vps_ref,
                 bias_ref, *rest):
    if len(rest) == 3:
        wi_ref, o_ref, wo_ref = rest
        wo_ref[...] = wi_ref[...].astype(BF16)
    else:
        (o_ref,) = rest
    n = pl.program_id(1)
    band = 2 * ATT_BLOCK
    r = lax.broadcasted_iota(jnp.int32, (ATT_BLOCK, band), 0)
    j = lax.broadcasted_iota(jnp.int32, (ATT_BLOCK, band), 1)
    dist = r + ATT_BLOCK - j
    valid = (dist >= 0) & (dist < WINDOW) & ((j >= ATT_BLOCK) | (n > 0))
    cat = jnp.concatenate
    k, ks = cat([kp_ref[...], kc_ref[...]], 0), cat([kps_ref[...], kcs_ref[...]], 0)
    v, vs = cat([vp_ref[...], vc_ref[...]], 0), cat([vps_ref[...], vcs_ref[...]], 0)
    gw = ATT_GROUP * ATT_HEAD_DIM
    lane = lax.broadcasted_iota(jnp.int32, (ATT_BLOCK, gw), 1)
    for g in range(ATT_KV_HEADS):
        s_all = _dot_nt(q_ref[:, g * gw:(g + 1) * gw], _block_diag(k, ks, g))
        ps, dens = [], []
        for hh in range(ATT_GROUP):
            h = g * ATT_GROUP + hh
            s = s_all[:, hh * band:(hh + 1) * band] + bias_ref[h]
            s = jnp.where(valid, s, -1e30)
            sink = sink_ref[h]
            m = jnp.maximum(jnp.max(s, axis=-1, keepdims=True), sink)
            p = jnp.exp(s - m)
            dens.append(jnp.sum(p, axis=-1, keepdims=True) + jnp.exp(sink - m))
            ps.append(p.astype(BF16))
        o = _dot(cat(ps, 1), _block_diag(v, vs, g))
        den = dens[ATT_GROUP - 1]
        for hh in reversed(range(ATT_GROUP - 1)):
            den = jnp.where(lane < (hh + 1) * ATT_HEAD_DIM, dens[hh], den)
        o_ref[:, g * gw:(g + 1) * gw] = (o / den).astype(o_ref.dtype)


def _attention(aq, ak, av, ak_sw, av_sw, sinks, band_bias, rider=None):
    nb = SEQ // ATT_BLOCK
    cur = lambda b, n: (b * nb + n, 0)
    prev = lambda b, n: (jnp.maximum(b * nb + n - 1, 0), 0)
    kv_cur = pl.BlockSpec((ATT_BLOCK, ATT_KV_W), cur)
    kv_prev = pl.BlockSpec((ATT_BLOCK, ATT_KV_W), prev)
    in_specs = [
        pl.BlockSpec(memory_space=pltpu.SMEM),
        pl.BlockSpec((ATT_BLOCK, ATT_Q_W), cur),
        kv_cur, kv_prev, kv_cur, kv_prev, kv_cur, kv_prev, kv_cur, kv_prev,
        _resident((ATT_Q_HEADS, ATT_BLOCK, 2 * ATT_BLOCK)),
    ]
    args = [sinks, aq, ak, ak, ak_sw, ak_sw, av, av, av_sw, av_sw, band_bias]
    out_specs = [pl.BlockSpec((ATT_BLOCK, ATT_Q_W), cur)]
    out_shape = [jax.ShapeDtypeStruct((TOKENS, ATT_Q_W), BF16)]
    if rider is not None:
        w2d, first_block, rows = rider
        r_in, r_out, r_shape = _cast_rider(w2d, first_block, rows, BATCH * nb, lambda b, n: b * nb + n)
        in_specs.append(r_in)
        args.append(w2d)
        out_specs.append(r_out)
        out_shape.append(r_shape)
    return pl.pallas_call(
        _attn_kernel,
        grid=(BATCH, nb),
        in_specs=in_specs,
        out_specs=out_specs,
        out_shape=out_shape,
        compiler_params=_cparams(("arbitrary", "arbitrary")),
        name="swa_attention",
    )(*args)


def _cumsum_rows(x, t):
    shift = 1
    while shift < x.shape[0]:
        x = x + jnp.where(t >= shift, pltpu.roll(x, shift, 0), 0.0)
        shift *= 2
    return x


def _hgrn_kernel(q_ref, f_ref, v_ref, g_ref, lbl_ref, ng_ref, *rest, layer):
    if len(rest) == 4:
        wi_ref, o_ref, wo_ref, st_ref = rest
        wo_ref[...] = wi_ref[...].astype(BF16)
    else:
        o_ref, st_ref = rest

    @pl.when(pl.program_id(1) == 0)
    def _():
        st_ref[...] = jnp.zeros_like(st_ref)

    rows = [lbl_ref[i:i + 1, :] for i in range(DEPTH)]
    mx = functools.reduce(jnp.maximum, rows)
    ex = [jnp.exp(rw - mx) for rw in rows]
    tot = functools.reduce(lambda a, b: a + b, ex)
    lower = jnp.zeros_like(mx)
    for i in range(1, layer + 1):
        lower = lower + ex[i] / tot

    c = HG_TILE
    t = lax.broadcasted_iota(jnp.int32, (c, HG_DK), 0)
    ts = lax.broadcasted_iota(jnp.int32, (c, c), 0)
    ss = lax.broadcasted_iota(jnp.int32, (c, c), 1)
    same64 = (ts >> 6) == (ss >> 6)
    diag32 = ((ts >> 5) == (ss >> 5)) & (ss <= ts)
    ng = ng_ref[...]

    for h in range(HG_HEADS):
        sl = slice(h * HG_DK, (h + 1) * HG_DK)
        lb = lower[:, sl]
        f = lb + (1.0 - lb) * _sigmoid(f_ref[:, sl])
        k = 1.0 - f
        a = _cumsum_rows(jnp.log(f), t)
        a_last = a[c - 1:c, :]
        q = q_ref[:, sl].astype(F32)
        v = v_ref[:, sl]
        st = st_ref[h]

        o = _dot_nt((q * jnp.exp(a)).astype(BF16), st.astype(BF16))
        ke = (k * jnp.exp(a_last - a)).astype(BF16)
        vt = v.astype(F32).T.astype(BF16)
        st_ref[h] = st * jnp.exp(a_last) + _dot(vt, ke)

        m1 = a[63:64, :]
        q1 = jnp.where(t >= 64, q * jnp.exp(a - m1), 0.0)
        k1 = jnp.where(t < 64, k * jnp.exp(m1 - a), 0.0)
        p = _dot_nt(q1.astype(BF16), k1.astype(BF16))

        m2 = jnp.where(t < 64, a[31:32, :], a[95:96, :])
        second = (t & 63) >= 32
        q2 = jnp.where(second, q * jnp.exp(a - m2), 0.0)
        k2 = jnp.where(second, 0.0, k * jnp.exp(m2 - a))
        p = p + jnp.where(same64, _dot_nt(q2.astype(BF16), k2.astype(BF16)), 0.0)

        m3 = jnp.where(t < 64,
                       jnp.where(t < 32, a[15:16, :], a[47:48, :]),
                       jnp.where(t < 96, a[79:80, :], a[111:112, :]))
        q3 = q * jnp.exp(a - m3)
        k3 = k * jnp.exp(m3 - a)
        p = p + jnp.where(diag32, _dot_nt(q3.astype(BF16), k3.astype(BF16)), 0.0)

        o = o + _dot(p.astype(BF16), v)
        o = o * lax.rsqrt(jnp.mean(o * o, axis=-1, keepdims=True) + EPS) * ng
        gate = g_ref[:, sl].astype(F32)
        o_ref[:, sl] = (o * (gate * _sigmoid(gate))).astype(o_ref.dtype)


def _hgrn(hq, hf, hi, hg, lb_logits, norm_g, layer, rider=None):
    nt = SEQ // HG_TILE
    blk = pl.BlockSpec((HG_TILE, HG_K_W), lambda b, s: (b * nt + s, 0))
    in_specs = [blk, blk, blk, blk, _resident((DEPTH, HG_K_W)), _resident((1, HG_DV))]
    args = [hq, hf, hi, hg, lb_logits, norm_g]
    out_specs = [blk]
    out_shape = [jax.ShapeDtypeStruct((TOKENS, HG_V_W), BF16)]
    if rider is not None:
        w2d, first_block, rows = rider
        r_in, r_out, r_shape = _cast_rider(w2d, first_block, rows, BATCH * nt, lambda b, s: b * nt + s)
        in_specs.append(r_in)
        args.append(w2d)
        out_specs.append(r_out)
        out_shape.append(r_shape)
    return pl.pallas_call(
        functools.partial(_hgrn_kernel, layer=layer),
        grid=(BATCH, nt),
        in_specs=in_specs,
        out_specs=out_specs,
        out_shape=out_shape,
        scratch_shapes=[pltpu.VMEM((HG_HEADS, HG_DV, HG_DK), F32)],
        compiler_params=_cparams(("arbitrary", "arbitrary")),
        name="hgrn2",
    )(*args)


def _merge_kernel(att_ref, hgo_ref, ga_ref, gb_ref, h_ref, wa_ref, wb_ref, wo_ref, *rest, moe):
    ya = _dot(att_ref[...], wa_ref[...])
    yb = _dot(hgo_ref[...], wb_ref[...])
    merged = _sigmoid(ga_ref[...].astype(F32)) * ya + _sigmoid(gb_ref[...].astype(F32)) * yb
    hn = h_ref[...] + _dot(merged.astype(BF16), wo_ref[...])
    if not moe:
        (ho_ref,) = rest
        ho_ref[...] = hn
        return
    gn_ref, rw_ref, ho_ref, u_ref, idx_ref, gate_ref = rest
    ho_ref[...] = hn
    u = _rms(hn, gn_ref[...])
    bits = pltpu.bitcast(u.astype(BF16).astype(F32), jnp.uint32)
    u_ref[...] = (bits[:, :D_MODEL // 2] >> 16) | (bits[:, D_MODEL // 2:] & jnp.uint32(0xFFFF0000))
    ls = [jnp.sum(u * rw_ref[e:e + 1, :], axis=-1, keepdims=True) for e in range(N_EXPERTS)]
    m1 = functools.reduce(jnp.maximum, ls)
    i1 = jnp.full(m1.shape, N_EXPERTS, jnp.int32)
    for e in reversed(range(N_EXPERTS)):
        i1 = jnp.where(ls[e] == m1, e, i1)
    rest_ls = [jnp.where(i1 == e, -jnp.inf, ls[e]) for e in range(N_EXPERTS)]
    m2 = functools.reduce(jnp.maximum, rest_ls)
    i2 = jnp.full(m1.shape, N_EXPERTS, jnp.int32)
    for e in reversed(range(N_EXPERTS)):
        i2 = jnp.where((rest_ls[e] == m2) & (i1 != e), e, i2)
    e2 = jnp.exp(m2 - m1)
    den = 1.0 + e2
    idx_ref[...] = jnp.concatenate([i1, i2], axis=1)
    gate_ref[...] = jnp.concatenate([1.0 / den, e2 / den], axis=1)


def _merge(att, hgo, ga, gb, h, wa, wb, wo, gn=None, rw=None):
    moe = gn is not None
    row = lambda width: pl.BlockSpec((ROW_TILE, width), lambda i: (i, 0))
    in_specs = [row(ATT_Q_W), row(HG_V_W), row(D_MODEL), row(D_MODEL), row(D_MODEL),
                _resident((ATT_Q_W, D_MODEL)), _resident((HG_V_W, D_MODEL)),
                _resident((D_MODEL, D_MODEL))]
    args = [att, hgo, ga, gb, h, wa, wb, wo]
    out_specs = [row(D_MODEL)]
    out_shape = [jax.ShapeDtypeStruct((TOKENS, D_MODEL), F32)]
    if moe:
        in_specs += [_resident((1, D_MODEL)), _resident((N_EXPERTS, D_MODEL))]
        args += [gn, rw]
        out_specs += [row(D_MODEL // 2), row(TOP_K), row(TOP_K)]
        out_shape += [jax.ShapeDtypeStruct((TOKENS, D_MODEL // 2), jnp.uint32),
                      jax.ShapeDtypeStruct((TOKENS, TOP_K), jnp.int32),
                      jax.ShapeDtypeStruct((TOKENS, TOP_K), F32)]
    return pl.pallas_call(
        functools.partial(_merge_kernel, moe=moe),
        grid=(TOKENS // ROW_TILE,),
        in_specs=in_specs,
        out_specs=out_specs,
        out_shape=out_shape,
        compiler_params=_cparams(("arbitrary",)),
        name="merge_moe" if moe else "merge",
    )(*args)


def _dense_ffn_kernel(h_ref, g_ref, w1_ref, w3_ref, w2_ref, o_ref, x_scr, acc_scr):
    f = pl.program_id(1)

    @pl.when(f == 0)
    def _():
        x_scr[...] = _rms(h_ref[...], g_ref[...]).astype(BF16)
        acc_scr[...] = jnp.zeros_like(acc_scr)

    x = x_scr[...]
    a = _dot(x, w1_ref[...])
    b = _dot(x, w3_ref[...])
    acc_scr[...] += _dot((a * _sigmoid(a) * b).astype(BF16), w2_ref[...])

    @pl.when(f == pl.num_programs(1) - 1)
    def _():
        o_ref[...] = h_ref[...] + acc_scr[...]


def _dense_ffn(h, g, w1, w3, w2):
    nf = FFN_DENSE // DENSE_F_TILE
    return pl.pallas_call(
        _dense_ffn_kernel,
        grid=(TOKENS // ROW_TILE, nf),
        in_specs=[
            pl.BlockSpec((ROW_TILE, D_MODEL), lambda i, f: (i, 0)),
            _resident((1, D_MODEL)),
            pl.BlockSpec((D_MODEL, DENSE_F_TILE), lambda i, f: (0, f)),
            pl.BlockSpec((D_MODEL, DENSE_F_TILE), lambda i, f: (0, f)),
            pl.BlockSpec((DENSE_F_TILE, D_MODEL), lambda i, f: (f, 0)),
        ],
        out_specs=pl.BlockSpec((ROW_TILE, D_MODEL), lambda i, f: (i, 0)),
        out_shape=jax.ShapeDtypeStruct((TOKENS, D_MODEL), F32),
        scratch_shapes=[pltpu.VMEM((ROW_TILE, D_MODEL), BF16), pltpu.VMEM((ROW_TILE, D_MODEL), F32)],
        compiler_params=_cparams(("arbitrary", "arbitrary")),
        name="dense_ffn",
    )(h, g, w1, w3, w2)


def _dispatch_kernel(pos_ref, u_ref, w2_ref, xs_in_ref, xs_ref, w2o_ref, sem):
    del xs_in_ref
    tok = DISPATCH_TILE // DISPATCH_PARTS
    r2 = W2_ROWS // DISPATCH_PARTS
    for part in range(DISPATCH_PARTS):
        s2 = slice(part * r2, (part + 1) * r2)
        w2o_ref[s2, :] = w2_ref[s2, :].astype(BF16)
        for r in range(part * tok, (part + 1) * tok):
            for k in range(TOP_K):
                d = pos_ref[0, 0, TOP_K * r + k]
                pltpu.make_async_copy(u_ref.at[pl.ds(r, 1)], xs_ref.at[pl.ds(d, 1)], sem).start()

    for k in range(TOP_K):
        pltpu.make_async_copy(u_ref, xs_ref.at[pl.ds(0, DISPATCH_TILE)], sem).wait()


def _dispatch(u_packed, pos, w2_all, moe_layer):
    n = DISPATCH_STEPS
    xs0 = jnp.zeros((MOE_ROWS, D_MODEL // 2), jnp.uint32)
    w2_in, w2_out, w2_shape = _cast_rider(w2_all, moe_layer * n, W2_ROWS, n, lambda i: i)
    return pl.pallas_call(
        _dispatch_kernel,
        grid=(n,),
        in_specs=[
            pl.BlockSpec((1, 1, TOP_K * DISPATCH_TILE), lambda i: (i, 0, 0), memory_space=pltpu.SMEM),
            pl.BlockSpec((DISPATCH_TILE, D_MODEL // 2), lambda i: (i, 0)),
            w2_in,
            pl.BlockSpec(memory_space=pl.ANY),
        ],
        out_specs=[pl.BlockSpec(memory_space=pl.ANY), w2_out],
        out_shape=[jax.ShapeDtypeStruct((MOE_ROWS, D_MODEL // 2), jnp.uint32), w2_shape],
        scratch_shapes=[pltpu.SemaphoreType.DMA(())],
        input_output_aliases={3: 0},
        compiler_params=_cparams(("arbitrary",)),
        name="moe_dispatch",
    )(pos.reshape(n, 1, TOP_K * DISPATCH_TILE), u_packed, w2_all, xs0)


def _moe_kernel(te_ref, nu_ref, x_ref, w1_ref, w3_ref, w2_ref, y_ref, xb, acc):
    i = pl.program_id(0)
    f = pl.program_id(1)
    valid = i < nu_ref[0]

    @pl.when((f == 0) & valid)
    def _():
        x = x_ref[...]
        half = D_MODEL // 2
        xb[:, :half] = pltpu.bitcast(x << 16, F32).astype(BF16)
        xb[:, half:] = pltpu.bitcast(x & jnp.uint32(0xFFFF0000), F32).astype(BF16)
        acc[...] = jnp.zeros_like(acc)

    @pl.when(valid)
    def _():
        x = xb[...]
        a = _dot(x, w1_ref[0])
        b = _dot(x, w3_ref[0])
        acc[...] += _dot((a * _sigmoid(a) * b).astype(BF16), w2_ref[0])

    @pl.when((f == MOE_NF - 1) & valid)
    def _():
        y_ref[...] = acc[...]

    @pl.when((f == MOE_NF - 1) & jnp.logical_not(valid))
    def _():
        y_ref[...] = jnp.zeros_like(y_ref)


def _moe_experts(xs, tile_expert, n_used, w1, w3, w2):
    row_blk = lambda i, f, te, nu: (jnp.minimum(i, nu[0] - 1), 0)
    f_blk = lambda i, f, nu: jnp.where(i < nu[0], f, MOE_NF - 1)
    grid_spec = pltpu.PrefetchScalarGridSpec(
        num_scalar_prefetch=2,
        grid=(MOE_TILES, MOE_NF),
        in_specs=[
            pl.BlockSpec((MOE_ROW_TILE, D_MODEL // 2), row_blk),
            pl.BlockSpec((1, D_MODEL, MOE_F_TILE), lambda i, f, te, nu: (te[i], 0, f_blk(i, f, nu))),
            pl.BlockSpec((1, D_MODEL, MOE_F_TILE), lambda i, f, te, nu: (te[i], 0, f_blk(i, f, nu))),
            pl.BlockSpec((1, MOE_F_TILE, D_MODEL), lambda i, f, te, nu: (te[i], f_blk(i, f, nu), 0)),
        ],
        out_specs=pl.BlockSpec((MOE_ROW_TILE, D_MODEL), lambda i, f, te, nu: (i, 0)),
        scratch_shapes=[
            pltpu.VMEM((MOE_ROW_TILE, D_MODEL), BF16),
            pltpu.VMEM((MOE_ROW_TILE, D_MODEL), F32),
        ],
    )
    return pl.pallas_call(
        _moe_kernel,
        grid_spec=grid_spec,
        out_shape=jax.ShapeDtypeStruct((MOE_ROWS, D_MODEL), F32),
        compiler_params=_cparams(("arbitrary", "arbitrary")),
        name="moe_experts",
    )(tile_expert, n_used, xs, w1, w3, w2)


def _moe_plan(idx):
    e_flat = idx.reshape(-1)
    onehot = (e_flat[:, None] == jnp.arange(N_EXPERTS, dtype=jnp.int32)[None, :]).astype(jnp.int32)
    csum = jnp.cumsum(onehot, axis=0)
    counts = csum[-1]
    rank = jnp.sum((csum - onehot) * onehot, axis=1)
    tiles_per = (counts + MOE_ROW_TILE - 1) // MOE_ROW_TILE
    tile_end = jnp.cumsum(tiles_per)
    tile_start = tile_end - tiles_per
    pos = jnp.sum(onehot * tile_start[None, :], axis=1) * MOE_ROW_TILE + rank
    tile_id = jnp.arange(MOE_TILES, dtype=jnp.int32)
    te = jnp.sum((tile_id[:, None] >= tile_end[None, :]).astype(jnp.int32), axis=1)
    n_used = tile_end[-1]
    te = jnp.minimum(te, N_EXPERTS - 1)
    te = jnp.where(tile_id < n_used, te, te[jnp.maximum(n_used - 1, 0)])
    return pos.astype(jnp.int32), te.astype(jnp.int32), n_used.reshape(1).astype(jnp.int32)


def _ple_tile(h, p, g_ref, wg_ref, wp_ref, fn_ref):
    u = _rms(h, g_ref[...]).astype(BF16)
    emb_gate = _sigmoid(_dot(u, wg_ref[...]))
    h = h + emb_gate * _dot(p.astype(BF16), wp_ref[...])
    if fn_ref is not None:
        h = _rms(h, fn_ref[...])
    return h


def _ple_kernel(h_ref, p_ref, g_ref, wg_ref, wp_ref, *rest, final):
    fn_ref, o_ref = rest if final else (None,) + rest
    o_ref[...] = _ple_tile(h_ref[...], p_ref[...], g_ref, wg_ref, wp_ref, fn_ref)


def _ple_moe_kernel(pos_ref, posn_ref, gate_ref, y_hbm, h_ref, p_ref, g_ref, wg_ref, wp_ref, *rest, final):
    if final:
        fn_ref, o_ref, yg, sems = rest
    else:
        fn_ref, (o_ref, yg, sems) = None, rest
    j = pl.program_id(0)

    def start_row(idx_ref, slot, r):
        for k in range(TOP_K):
            s = idx_ref[0, slot, TOP_K * r + k]
            pltpu.make_async_copy(y_hbm.at[pl.ds(s, 1)], yg.at[slot, k, pl.ds(r, 1)], sems.at[slot]).start()

    def wait_rows(slot):
        for k in range(TOP_K):
            pltpu.make_async_copy(y_hbm.at[pl.ds(0, ROW_TILE)], yg.at[slot, k], sems.at[slot]).wait()

    @pl.when(j == 0)
    def _():
        for slot in range(2):
            def body(r, carry, slot=slot):
                start_row(pos_ref, slot, r)
                return carry
            lax.fori_loop(0, ROW_TILE, body, 0, unroll=8)

    for slot in range(2):
        rows = pl.ds(slot * ROW_TILE, ROW_TILE)
        wait_rows(slot)
        h = h_ref[rows, :]
        gate = gate_ref[rows, :]
        for k in range(TOP_K):
            h = h + gate[:, k:k + 1] * yg[slot, k]
        o_ref[rows, :] = _ple_tile(h, p_ref[rows, :], g_ref, wg_ref, wp_ref, fn_ref)
        for r in range(ROW_TILE):
            start_row(posn_ref, slot, r)

    @pl.when(j == pl.num_programs(0) - 1)
    def _():
        for slot in range(2):
            wait_rows(slot)


def _ple(h, moe_in, p, g, wg, wp, final_g):
    moe = moe_in is not None
    final = final_g is not None
    tile = 2 * ROW_TILE if moe else ROW_TILE
    n = TOKENS // tile
    row = lambda width: pl.BlockSpec((tile, width), lambda i: (i, 0))
    in_specs, args, scratch = [], [], []
    if moe:
        pos, gate, y = moe_in
        pos3 = pos.reshape(n, 2, TOP_K * ROW_TILE)
        smem = lambda imap: pl.BlockSpec((1, 2, TOP_K * ROW_TILE), imap, memory_space=pltpu.SMEM)
        in_specs += [smem(lambda i: (i, 0, 0)), smem(lambda i: (jnp.minimum(i + 1, n - 1), 0, 0)),
                     row(TOP_K), pl.BlockSpec(memory_space=pl.ANY)]
        args += [pos3, pos3, gate, y]
        scratch = [pltpu.VMEM((2, TOP_K, ROW_TILE, D_MODEL), F32), pltpu.SemaphoreType.DMA((2,))]
    in_specs += [row(D_MODEL), row(PLE_DIM), _resident((1, D_MODEL)), _resident((D_MODEL, D_MODEL)),
                 _resident((PLE_DIM, D_MODEL))]
    args += [h, p, g, wg, wp]
    if final:
        in_specs.append(_resident((1, D_MODEL)))
        args.append(final_g)
    return pl.pallas_call(
        functools.partial(_ple_moe_kernel if moe else _ple_kernel, final=final),
        grid=(n,),
        in_specs=in_specs,
        out_specs=row(D_MODEL),
        out_shape=jax.ShapeDtypeStruct((TOKENS, D_MODEL), F32),
        scratch_shapes=scratch,
        compiler_params=_cparams(("arbitrary",)),
        name="ple_moe" if moe else "ple",
    )(*args)


def _t5_bucket(dist):
    max_exact = REL_BUCKETS // 2
    d = jnp.maximum(dist, 0)
    large = max_exact + (jnp.log(jnp.maximum(d, 1).astype(jnp.float32) / max_exact)
                         / math.log(REL_MAX_DIST / max_exact)
                         * (REL_BUCKETS - max_exact)).astype(jnp.int32)
    large = jnp.minimum(large, REL_BUCKETS - 1)
    return jnp.where(d < max_exact, d, large)


def kernel(x, p, w_in, sinks, rel_bias, lb_logits, hgrn_norm, w_branch_a, w_branch_b, w_out,
           norm_mix, norm_ffn, norm_ple, dense_w1, dense_w3, dense_w2, router_w, moe_w1,
           moe_w3, moe_w2, ple_proj, ple_gate, final_norm):
    qi = jnp.arange(ATT_BLOCK)[:, None]
    kj = jnp.arange(2 * ATT_BLOCK)[None, :]
    bucket = _t5_bucket(qi + ATT_BLOCK - kj)[None]
    band_bias = jnp.zeros((ATT_Q_HEADS, ATT_BLOCK, 2 * ATT_BLOCK), F32)
    for b in range(REL_BUCKETS):
        band_bias = jnp.where(bucket == b, rel_bias[b].astype(F32)[:, None, None], band_bias)

    bf = lambda w: w.astype(BF16)
    h = x.reshape(TOKENS, D_MODEL)
    pt = p.reshape(DEPTH, TOKENS, PLE_DIM)
    vec = lambda g: g.reshape(1, -1)
    n_moe = moe_w1.shape[0]
    w1_all = moe_w1.reshape(n_moe * N_EXPERTS * D_MODEL, FFN_EXPERT)
    w3_all = moe_w3.reshape(n_moe * N_EXPERTS * D_MODEL, FFN_EXPERT)
    w2_all = moe_w2.reshape(n_moe * N_EXPERTS * FFN_EXPERT, D_MODEL)

    for l in range(DEPTH):
        moe = l % 2 == 1
        aq, ak, av, hq, hf, hi, hg, ga, gb, ak_sw, av_sw = _in_proj(h, vec(norm_mix[l]), bf(w_in[l]))
        if moe:
            first = (l // 2) * MIXER_STEPS
            att, w1b = _attention(aq, ak, av, ak_sw, av_sw, sinks[l], band_bias, (w1_all, first, W13_ROWS))
            hgo, w3b = _hgrn(hq, hf, hi, hg, lb_logits, vec(hgrn_norm[l]), l, (w3_all, first, W13_ROWS))
        else:
            (att,) = _attention(aq, ak, av, ak_sw, av_sw, sinks[l], band_bias)
            (hgo,) = _hgrn(hq, hf, hi, hg, lb_logits, vec(hgrn_norm[l]), l)
        wa, wb, wo = bf(w_branch_a[l]), bf(w_branch_b[l]), bf(w_out[l])
        final_g = vec(final_norm) if l == DEPTH - 1 else None
        if not moe:
            (h,) = _merge(att, hgo, ga, gb, h, wa, wb, wo)
            h = _dense_ffn(h, vec(norm_ffn[l]), bf(dense_w1[l // 2]), bf(dense_w3[l // 2]),
                           bf(dense_w2[l // 2]))
            y = None
        else:
            h, u, idx, gate = _merge(att, hgo, ga, gb, h, wa, wb, wo,
                                     vec(norm_ffn[l]), router_w[l // 2].T)
            pos, tile_expert, n_used = _moe_plan(idx)
            xs, w2b = _dispatch(u, pos, w2_all, l // 2)
            y = _moe_experts(xs, tile_expert, n_used,
                             w1b.reshape(N_EXPERTS, D_MODEL, FFN_EXPERT),
                             w3b.reshape(N_EXPERTS, D_MODEL, FFN_EXPERT),
                             w2b.reshape(N_EXPERTS, FFN_EXPERT, D_MODEL))
            y = (pos, gate, y)
        h = _ple(h, y, pt[l], vec(norm_ple[l]), bf(ple_gate[l]), bf(ple_proj[l]), final_g)
    return h.reshape(BATCH, SEQ, D_MODEL)
```

```python
import functools
import math

import jax
import jax.numpy as jnp
from jax import lax
from jax.experimental import pallas as pl
from jax.experimental.pallas import tpu as pltpu

F32 = jnp.float32
BF16 = jnp.bfloat16

D_MODEL = 1024
BATCH = 4
SEQ = 4096
TOKENS = BATCH * SEQ
DEPTH = 4
ATT_Q_HEADS = 8
ATT_KV_HEADS = 2
ATT_HEAD_DIM = 64
ATT_GROUP = ATT_Q_HEADS // ATT_KV_HEADS
WINDOW = 128
ATT_BLOCK = 128
REL_BUCKETS = 32
REL_MAX_DIST = 128
HG_HEADS = 4
HG_DK = 128
HG_DV = 128
ATT_Q_W = ATT_Q_HEADS * ATT_HEAD_DIM
ATT_KV_W = ATT_KV_HEADS * ATT_HEAD_DIM
HG_K_W = HG_HEADS * HG_DK
HG_V_W = HG_HEADS * HG_DV
IN_SPLITS = (ATT_Q_W, ATT_KV_W, ATT_KV_W, HG_K_W, HG_K_W, HG_V_W, HG_V_W, D_MODEL, D_MODEL)
IN_WIDTH = sum(IN_SPLITS)
FFN_DENSE = 2816
N_EXPERTS = 8
TOP_K = 2
FFN_EXPERT = 3584
PLE_DIM = 256
EPS = 1e-6

V7X_VMEM_LIMIT_BYTES = 52 * 1024 * 1024
ROW_TILE = 512
HG_TILE = 128
DENSE_F_TILE = FFN_DENSE // 2
MOE_ROW_TILE = 512
DISPATCH_TILE = 512
DISPATCH_STEPS = TOKENS // DISPATCH_TILE
DISPATCH_PARTS = 8
assert HG_TILE == ATT_BLOCK
MIXER_STEPS = TOKENS // ATT_BLOCK
W13_ROWS = N_EXPERTS * D_MODEL // MIXER_STEPS
W2_ROWS = N_EXPERTS * FFN_EXPERT // DISPATCH_STEPS
MOE_F_TILE = FFN_EXPERT // 2
MOE_NF = FFN_EXPERT // MOE_F_TILE
MOE_TILES = (TOKENS * TOP_K) // MOE_ROW_TILE + N_EXPERTS - 1
MOE_ROWS = MOE_TILES * MOE_ROW_TILE
COL_CHUNK = 512


def _cparams(sem):
    return pltpu.CompilerParams(dimension_semantics=sem, vmem_limit_bytes=V7X_VMEM_LIMIT_BYTES)


def _rms(x, g):
    return x * lax.rsqrt(jnp.mean(x * x, axis=-1, keepdims=True) + EPS) * g


def _sigmoid(x):
    return 1.0 / (1.0 + jnp.exp(-x))


def _dot(a, b):
    return jnp.dot(a, b, preferred_element_type=F32)


def _dot_nt(a, b):
    return lax.dot_general(a, b, (((1,), (1,)), ((), ())), preferred_element_type=F32)


def _resident(shape):
    nd = len(shape)
    return pl.BlockSpec(shape, lambda *_: (0,) * nd)


def _cast_rider(w2d, first_block, rows, n_steps, step_of):
    cols = w2d.shape[1]
    in_spec = pl.BlockSpec((rows, cols), lambda *ids: (first_block + step_of(*ids), 0))
    out_spec = pl.BlockSpec((rows, cols), lambda *ids: (step_of(*ids), 0))
    return in_spec, out_spec, jax.ShapeDtypeStruct((rows * n_steps, cols), BF16)


IN_DTYPES = (BF16, BF16, BF16, BF16, F32, BF16, BF16, BF16, BF16)
ATT_SCALE = ATT_HEAD_DIM ** -0.5


def _in_proj_kernel(h_ref, g_ref, w_ref, *out_refs):
    xn = _rms(h_ref[...], g_ref[...]).astype(BF16)
    main_refs, (ak_sw_ref, av_sw_ref) = out_refs[:len(IN_SPLITS)], out_refs[len(IN_SPLITS):]
    off = 0
    for idx, (o_ref, width) in enumerate(zip(main_refs, IN_SPLITS)):
        for c in range(0, width, COL_CHUNK):
            cw = min(COL_CHUNK, width - c)
            z = _dot(xn, w_ref[:, off + c:off + c + cw])
            if idx == 0:
                z = z * ATT_SCALE
            o_ref[:, c:c + cw] = z.astype(o_ref.dtype)
            if idx in (1, 2):
                sw_ref = ak_sw_ref if idx == 1 else av_sw_ref
                sw_ref[...] = jnp.concatenate([z[:, ATT_HEAD_DIM:], z[:, :ATT_HEAD_DIM]], axis=1).astype(sw_ref.dtype)
        off += width


def _in_proj(h, g, w):
    row = lambda width: pl.BlockSpec((ROW_TILE, width), lambda i: (i, 0))
    widths = IN_SPLITS + (ATT_KV_W, ATT_KV_W)
    dtypes = IN_DTYPES + (BF16, BF16)
    return pl.pallas_call(
        _in_proj_kernel,
        grid=(TOKENS // ROW_TILE,),
        in_specs=[row(D_MODEL), _resident((1, D_MODEL)), _resident((D_MODEL, IN_WIDTH))],
        out_specs=[row(wd) for wd in widths],
        out_shape=[jax.ShapeDtypeStruct((TOKENS, wd), dt) for wd, dt in zip(widths, dtypes)],
        compiler_params=_cparams(("arbitrary",)),
        name="in_proj",
    )(h, g, w)


def _block_diag(a, a_sw, g):
    lane = lax.broadcasted_iota(jnp.int32, a.shape, 1)
    low = lane < ATT_HEAD_DIM
    own, other = (a, a_sw) if g == 0 else (a_sw, a)
    zero = jnp.zeros_like(a)
    first = jnp.where(low, own, zero)
    second = jnp.where(low, zero, other)
    cat = jnp.concatenate
    return cat([cat([first, zero], 1), cat([second, zero], 1),
                cat([zero, first], 1), cat([zero, second], 1)], 0)


def _attn_block(sink_ref, q_ref, kc_ref, kp_ref, kcs_ref, kps_ref, vc_ref, vp_ref, vcs_ref, vps_ref,
                bias_ref):
    n = pl.program_id(1)
    band = 2 * ATT_BLOCK
    r = lax.broadcasted_iota(jnp.int32, (ATT_BLOCK, band), 0)
    j = lax.broadcasted_iota(jnp.int32, (ATT_BLOCK, band), 1)
    dist = r + ATT_BLOCK - j
    valid = (dist >= 0) & (dist < WINDOW) & ((j >= ATT_BLOCK) | (n > 0))
    cat = jnp.concatenate
    k, ks = cat([kp_ref[...], kc_ref[...]], 0), cat([kps_ref[...], kcs_ref[...]], 0)
    v, vs = cat([vp_ref[...], vc_ref[...]], 0), cat([vps_ref[...], vcs_ref[...]], 0)
    gw = ATT_GROUP * ATT_HEAD_DIM
    lane = lax.broadcasted_iota(jnp.int32, (ATT_BLOCK, gw), 1)
    outs = []
    for g in range(ATT_KV_HEADS):
        s_all = _dot_nt(q_ref[:, g * gw:(g + 1) * gw], _block_diag(k, ks, g))
        ps, dens = [], []
        for hh in range(ATT_GROUP):
            h = g * ATT_GROUP + hh
            s = s_all[:, hh * band:(hh + 1) * band] + bias_ref[h]
            s = jnp.where(valid, s, -1e30)
            sink = sink_ref[h]
            m = jnp.maximum(jnp.max(s, axis=-1, keepdims=True), sink)
            p = jnp.exp(s - m)
            dens.append(jnp.sum(p, axis=-1, keepdims=True) + jnp.exp(sink - m))
            ps.append(p.astype(BF16))
        o = _dot(cat(ps, 1), _block_diag(v, vs, g))
        den = dens[ATT_GROUP - 1]
        for hh in reversed(range(ATT_GROUP - 1)):
            den = jnp.where(lane < (hh + 1) * ATT_HEAD_DIM, dens[hh], den)
        outs.append(o / den)
    return cat(outs, 1)


def _cumsum_rows(x, t):
    shift = 1
    while shift < x.shape[0]:
        x = x + jnp.where(t >= shift, pltpu.roll(x, shift, 0), 0.0)
        shift *= 2
    return x


def _hgrn_tile(q_ref, f_ref, v_ref, g_ref, lbl_ref, ng_ref, st_ref, layer):
    rows = [lbl_ref[i:i + 1, :] for i in range(DEPTH)]
    mx = functools.reduce(jnp.maximum, rows)
    ex = [jnp.exp(rw - mx) for rw in rows]
    tot = functools.reduce(lambda a, b: a + b, ex)
    lower = jnp.zeros_like(mx)
    for i in range(1, layer + 1):
        lower = lower + ex[i] / tot

    c = HG_TILE
    t = lax.broadcasted_iota(jnp.int32, (c, HG_DK), 0)
    ts = lax.broadcasted_iota(jnp.int32, (c, c), 0)
    ss = lax.broadcasted_iota(jnp.int32, (c, c), 1)
    same64 = (ts >> 6) == (ss >> 6)
    diag32 = ((ts >> 5) == (ss >> 5)) & (ss <= ts)
    ng = ng_ref[...]

    outs = []
    for h in range(HG_HEADS):
        sl = slice(h * HG_DK, (h + 1) * HG_DK)
        lb = lower[:, sl]
        f = lb + (1.0 - lb) * _sigmoid(f_ref[:, sl])
        k = 1.0 - f
        a = _cumsum_rows(jnp.log(f), t)
        a_last = a[c - 1:c, :]
        q = q_ref[:, sl].astype(F32)
        v = v_ref[:, sl]
        st = st_ref[h]

        o = _dot_nt((q * jnp.exp(a)).astype(BF16), st.astype(BF16))
        ke = (k * jnp.exp(a_last - a)).astype(BF16)
        vt = v.astype(F32).T.astype(BF16)
        st_ref[h] = st * jnp.exp(a_last) + _dot(vt, ke)

        m1 = a[63:64, :]
        q1 = jnp.where(t >= 64, q * jnp.exp(a - m1), 0.0)
        k1 = jnp.where(t < 64, k * jnp.exp(m1 - a), 0.0)
        p = _dot_nt(q1.astype(BF16), k1.astype(BF16))

        m2 = jnp.where(t < 64, a[31:32, :], a[95:96, :])
        second = (t & 63) >= 32
        q2 = jnp.where(second, q * jnp.exp(a - m2), 0.0)
        k2 = jnp.where(second, 0.0, k * jnp.exp(m2 - a))
        p = p + jnp.where(same64, _dot_nt(q2.astype(BF16), k2.astype(BF16)), 0.0)

        m3 = jnp.where(t < 64,
                       jnp.where(t < 32, a[15:16, :], a[47:48, :]),
                       jnp.where(t < 96, a[79:80, :], a[111:112, :]))
        q3 = q * jnp.exp(a - m3)
        k3 = k * jnp.exp(m3 - a)
        p = p + jnp.where(diag32, _dot_nt(q3.astype(BF16), k3.astype(BF16)), 0.0)

        o = o + _dot(p.astype(BF16), v)
        o = o * lax.rsqrt(jnp.mean(o * o, axis=-1, keepdims=True) + EPS) * ng
        gate = g_ref[:, sl].astype(F32)
        outs.append(o * (gate * _sigmoid(gate)))
    return jnp.concatenate(outs, 1)


def _route(u, rw_ref):
    ls = [jnp.sum(u * rw_ref[e:e + 1, :], axis=-1, keepdims=True) for e in range(N_EXPERTS)]
    m1 = functools.reduce(jnp.maximum, ls)
    i1 = jnp.full(m1.shape, N_EXPERTS, jnp.int32)
    for e in reversed(range(N_EXPERTS)):
        i1 = jnp.where(ls[e] == m1, e, i1)
    rest_ls = [jnp.where(i1 == e, -jnp.inf, ls[e]) for e in range(N_EXPERTS)]
    m2 = functools.reduce(jnp.maximum, rest_ls)
    i2 = jnp.full(m1.shape, N_EXPERTS, jnp.int32)
    for e in reversed(range(N_EXPERTS)):
        i2 = jnp.where((rest_ls[e] == m2) & (i1 != e), e, i2)
    e2 = jnp.exp(m2 - m1)
    den = 1.0 + e2
    return jnp.concatenate([i1, i2], axis=1), jnp.concatenate([1.0 / den, e2 / den], axis=1)


N_ATT_IN = 11
N_HGRN_IN = 6
N_MERGE_IN = 6


def _mixer_kernel(*refs, layer, moe):
    refs = list(refs)
    att_in = refs[:N_ATT_IN]
    hg_in = refs[N_ATT_IN:N_ATT_IN + N_HGRN_IN]
    ga_ref, gb_ref, h_ref, wa_ref, wb_ref, wo_ref = refs[N_ATT_IN + N_HGRN_IN:N_ATT_IN + N_HGRN_IN + N_MERGE_IN]
    rest = refs[N_ATT_IN + N_HGRN_IN + N_MERGE_IN:]
    if moe:
        (gn_ref, rw_ref, w1_ref, w3_ref, ho_ref, u_ref, idx_ref, gate_ref, w1o_ref, w3o_ref, st_ref) = rest
        w1o_ref[...] = w1_ref[...].astype(BF16)
        w3o_ref[...] = w3_ref[...].astype(BF16)
    else:
        ho_ref, st_ref = rest

    @pl.when(pl.program_id(1) == 0)
    def _():
        st_ref[...] = jnp.zeros_like(st_ref)

    att = _attn_block(*att_in)
    hgo = _hgrn_tile(*hg_in, st_ref, layer)
    ya = _dot(att.astype(BF16), wa_ref[...])
    yb = _dot(hgo.astype(BF16), wb_ref[...])
    merged = _sigmoid(ga_ref[...].astype(F32)) * ya + _sigmoid(gb_ref[...].astype(F32)) * yb
    hn = h_ref[...] + _dot(merged.astype(BF16), wo_ref[...])
    ho_ref[...] = hn
    if moe:
        u = _rms(hn, gn_ref[...])
        bits = pltpu.bitcast(u.astype(BF16).astype(F32), jnp.uint32)
        u_ref[...] = (bits[:, :D_MODEL // 2] >> 16) | (bits[:, D_MODEL // 2:] & jnp.uint32(0xFFFF0000))
        idx_ref[...], gate_ref[...] = _route(u, rw_ref)


def _mixer(proj, h, sinks, band_bias, lb_logits, norm_g, wa, wb, wo, layer, moe_in=None):
    aq, ak, av, hq, hf, hi, hg, ga, gb, ak_sw, av_sw = proj
    moe = moe_in is not None
    nb = SEQ // ATT_BLOCK
    step = lambda b, n: b * nb + n
    cur = lambda b, n: (step(b, n), 0)
    prev = lambda b, n: (jnp.maximum(step(b, n) - 1, 0), 0)
    blk = lambda width: pl.BlockSpec((ATT_BLOCK, width), cur)
    kv_cur, kv_prev = blk(ATT_KV_W), pl.BlockSpec((ATT_BLOCK, ATT_KV_W), prev)
    in_specs = [
        pl.BlockSpec(memory_space=pltpu.SMEM), blk(ATT_Q_W),
        kv_cur, kv_prev, kv_cur, kv_prev, kv_cur, kv_prev, kv_cur, kv_prev,
        _resident((ATT_Q_HEADS, ATT_BLOCK, 2 * ATT_BLOCK)),
        blk(HG_K_W), blk(HG_K_W), blk(HG_V_W), blk(HG_V_W), _resident((DEPTH, HG_K_W)), _resident((1, HG_DV)),
        blk(D_MODEL), blk(D_MODEL), blk(D_MODEL),
        _resident((ATT_Q_W, D_MODEL)), _resident((HG_V_W, D_MODEL)), _resident((D_MODEL, D_MODEL)),
    ]
    args = [sinks, aq, ak, ak, ak_sw, ak_sw, av, av, av_sw, av_sw, band_bias,
            hq, hf, hi, hg, lb_logits, norm_g, ga, gb, h, wa, wb, wo]
    assert len(in_specs) == N_ATT_IN + N_HGRN_IN + N_MERGE_IN
    out_specs = [blk(D_MODEL)]
    out_shape = [jax.ShapeDtypeStruct((TOKENS, D_MODEL), F32)]
    if moe:
        gn, rw, w1_all, w3_all, first_block = moe_in
        in_specs += [_resident((1, D_MODEL)), _resident((N_EXPERTS, D_MODEL))]
        args += [gn, rw]
        out_specs += [blk(D_MODEL // 2), blk(TOP_K), blk(TOP_K)]
        out_shape += [jax.ShapeDtypeStruct((TOKENS, D_MODEL // 2), jnp.uint32),
                      jax.ShapeDtypeStruct((TOKENS, TOP_K), jnp.int32),
                      jax.ShapeDtypeStruct((TOKENS, TOP_K), F32)]
        for w_all in (w1_all, w3_all):
            r_in, r_out, r_shape = _cast_rider(w_all, first_block, W13_ROWS, MIXER_STEPS, step)
            in_specs.append(r_in)
            args.append(w_all)
            out_specs.append(r_out)
            out_shape.append(r_shape)
    return pl.pallas_call(
        functools.partial(_mixer_kernel, layer=layer, moe=moe),
        grid=(BATCH, nb),
        in_specs=in_specs,
        out_specs=out_specs,
        out_shape=out_shape,
        scratch_shapes=[pltpu.VMEM((HG_HEADS, HG_DV, HG_DK), F32)],
        compiler_params=_cparams(("arbitrary", "arbitrary")),
        name="mixer_moe" if moe else "mixer",
    )(*args)


def _dense_ffn_kernel(h_ref, g_ref, w1_ref, w3_ref, w2_ref, o_ref, x_scr, acc_scr):
    f = pl.program_id(1)

    @pl.when(f == 0)
    def _():
        x_scr[...] = _rms(h_ref[...], g_ref[...]).astype(BF16)
        acc_scr[...] = jnp.zeros_like(acc_scr)

    x = x_scr[...]
    a = _dot(x, w1_ref[...])
    b = _dot(x, w3_ref[...])
    acc_scr[...] += _dot((a * _sigmoid(a) * b).astype(BF16), w2_ref[...])

    @pl.when(f == pl.num_programs(1) - 1)
    def _():
        o_ref[...] = h_ref[...] + acc_scr[...]


def _dense_ffn(h, g, w1, w3, w2):
    nf = FFN_DENSE // DENSE_F_TILE
    return pl.pallas_call(
        _dense_ffn_kernel,
        grid=(TOKENS // ROW_TILE, nf),
        in_specs=[
            pl.BlockSpec((ROW_TILE, D_MODEL), lambda i, f: (i, 0)),
            _resident((1, D_MODEL)),
            pl.BlockSpec((D_MODEL, DENSE_F_TILE), lambda i, f: (0, f)),
            pl.BlockSpec((D_MODEL, DENSE_F_TILE), lambda i, f: (0, f)),
            pl.BlockSpec((DENSE_F_TILE, D_MODEL), lambda i, f: (f, 0)),
        ],
        out_specs=pl.BlockSpec((ROW_TILE, D_MODEL), lambda i, f: (i, 0)),
        out_shape=jax.ShapeDtypeStruct((TOKENS, D_MODEL), F32),
        scratch_shapes=[pltpu.VMEM((ROW_TILE, D_MODEL), BF16), pltpu.VMEM((ROW_TILE, D_MODEL), F32)],
        compiler_params=_cparams(("arbitrary", "arbitrary")),
        name="dense_ffn",
    )(h, g, w1, w3, w2)


def _dispatch_kernel(pos_ref, u_ref, w2_ref, xs_in_ref, xs_ref, w2o_ref, sem):
    del xs_in_ref
    tok = DISPATCH_TILE // DISPATCH_PARTS
    r2 = W2_ROWS // DISPATCH_PARTS
    for part in range(DISPATCH_PARTS):
        s2 = slice(part * r2, (part + 1) * r2)
        w2o_ref[s2, :] = w2_ref[s2, :].astype(BF16)
        for r in range(part * tok, (part + 1) * tok):
            for k in range(TOP_K):
                d = pos_ref[0, 0, TOP_K * r + k]
                pltpu.make_async_copy(u_ref.at[pl.ds(r, 1)], xs_ref.at[pl.ds(d, 1)], sem).start()

    for k in range(TOP_K):
        pltpu.make_async_copy(u_ref, xs_ref.at[pl.ds(0, DISPATCH_TILE)], sem).wait()


def _dispatch(u_packed, pos, w2_all, moe_layer):
    n = DISPATCH_STEPS
    xs0 = jnp.zeros((MOE_ROWS, D_MODEL // 2), jnp.uint32)
    w2_in, w2_out, w2_shape = _cast_rider(w2_all, moe_layer * n, W2_ROWS, n, lambda i: i)
    return pl.pallas_call(
        _dispatch_kernel,
        grid=(n,),
        in_specs=[
            pl.BlockSpec((1, 1, TOP_K * DISPATCH_TILE), lambda i: (i, 0, 0), memory_space=pltpu.SMEM),
            pl.BlockSpec((DISPATCH_TILE, D_MODEL // 2), lambda i: (i, 0)),
            w2_in,
            pl.BlockSpec(memory_space=pl.ANY),
        ],
        out_specs=[pl.BlockSpec(memory_space=pl.ANY), w2_out],
        out_shape=[jax.ShapeDtypeStruct((MOE_ROWS, D_MODEL // 2), jnp.uint32), w2_shape],
        scratch_shapes=[pltpu.SemaphoreType.DMA(())],
        input_output_aliases={3: 0},
        compiler_params=_cparams(("arbitrary",)),
        name="moe_dispatch",
    )(pos.reshape(n, 1, TOP_K * DISPATCH_TILE), u_packed, w2_all, xs0)


def _moe_kernel(te_ref, nu_ref, x_ref, w1_ref, w3_ref, w2_ref, y_ref, xb, acc):
    i = pl.program_id(0)
    f = pl.program_id(1)
    valid = i < nu_ref[0]

    @pl.when((f == 0) & valid)
    def _():
        x = x_ref[...]
        half = D_MODEL // 2
        xb[:, :half] = pltpu.bitcast(x << 16, F32).astype(BF16)
        xb[:, half:] = pltpu.bitcast(x & jnp.uint32(0xFFFF0000), F32).astype(BF16)
        acc[...] = jnp.zeros_like(acc)

    @pl.when(valid)
    def _():
        x = xb[...]
        a = _dot(x, w1_ref[0])
        b = _dot(x, w3_ref[0])
        acc[...] += _dot((a * _sigmoid(a) * b).astype(BF16), w2_ref[0])

    @pl.when((f == MOE_NF - 1) & valid)
    def _():
        y_ref[...] = acc[...]

    @pl.when((f == MOE_NF - 1) & jnp.logical_not(valid))
    def _():
        y_ref[...] = jnp.zeros_like(y_ref)


def _moe_experts(xs, tile_expert, n_used, w1, w3, w2):
    row_blk = lambda i, f, te, nu: (jnp.minimum(i, nu[0] - 1), 0)
    f_blk = lambda i, f, nu: jnp.where(i < nu[0], f, MOE_NF - 1)
    grid_spec = pltpu.PrefetchScalarGridSpec(
        num_scalar_prefetch=2,
        grid=(MOE_TILES, MOE_NF),
        in_specs=[
            pl.BlockSpec((MOE_ROW_TILE, D_MODEL // 2), row_blk),
            pl.BlockSpec((1, D_MODEL, MOE_F_TILE), lambda i, f, te, nu: (te[i], 0, f_blk(i, f, nu))),
            pl.BlockSpec((1, D_MODEL, MOE_F_TILE), lambda i, f, te, nu: (te[i], 0, f_blk(i, f, nu))),
            pl.BlockSpec((1, MOE_F_TILE, D_MODEL), lambda i, f, te, nu: (te[i], f_blk(i, f, nu), 0)),
        ],
        out_specs=pl.BlockSpec((MOE_ROW_TILE, D_MODEL), lambda i, f, te, nu: (i, 0)),
        scratch_shapes=[
            pltpu.VMEM((MOE_ROW_TILE, D_MODEL), BF16),
            pltpu.VMEM((MOE_ROW_TILE, D_MODEL), F32),
        ],
    )
    return pl.pallas_call(
        _moe_kernel,
        grid_spec=grid_spec,
        out_shape=jax.ShapeDtypeStruct((MOE_ROWS, D_MODEL), F32),
        compiler_params=_cparams(("arbitrary", "arbitrary")),
        name="moe_experts",
    )(tile_expert, n_used, xs, w1, w3, w2)


def _moe_plan(idx):
    e_flat = idx.reshape(-1)
    onehot = (e_flat[:, None] == jnp.arange(N_EXPERTS, dtype=jnp.int32)[None, :]).astype(jnp.int32)
    csum = jnp.cumsum(onehot, axis=0)
    counts = csum[-1]
    rank = jnp.sum((csum - onehot) * onehot, axis=1)
    tiles_per = (counts + MOE_ROW_TILE - 1) // MOE_ROW_TILE
    tile_end = jnp.cumsum(tiles_per)
    tile_start = tile_end - tiles_per
    pos = jnp.sum(onehot * tile_start[None, :], axis=1) * MOE_ROW_TILE + rank
    tile_id = jnp.arange(MOE_TILES, dtype=jnp.int32)
    te = jnp.sum((tile_id[:, None] >= tile_end[None, :]).astype(jnp.int32), axis=1)
    n_used = tile_end[-1]
    te = jnp.minimum(te, N_EXPERTS - 1)
    te = jnp.where(tile_id < n_used, te, te[jnp.maximum(n_used - 1, 0)])
    return pos.astype(jnp.int32), te.astype(jnp.int32), n_used.reshape(1).astype(jnp.int32)


def _ple_tile(h, p, g_ref, wg_ref, wp_ref, fn_ref):
    u = _rms(h, g_ref[...]).astype(BF16)
    emb_gate = _sigmoid(_dot(u, wg_ref[...]))
    h = h + emb_gate * _dot(p.astype(BF16), wp_ref[...])
    if fn_ref is not None:
        h = _rms(h, fn_ref[...])
    return h


def _ple_kernel(h_ref, p_ref, g_ref, wg_ref, wp_ref, *rest, final):
    fn_ref, o_ref = rest if final else (None,) + rest
    o_ref[...] = _ple_tile(h_ref[...], p_ref[...], g_ref, wg_ref, wp_ref, fn_ref)


def _ple_moe_kernel(pos_ref, posn_ref, gate_ref, y_hbm, h_ref, p_ref, g_ref, wg_ref, wp_ref, *rest, final):
    if final:
        fn_ref, o_ref, yg, sems = rest
    else:
        fn_ref, (o_ref, yg, sems) = None, rest
    j = pl.program_id(0)

    def start_row(idx_ref, slot, r):
        for k in range(TOP_K):
            s = idx_ref[0, slot, TOP_K * r + k]
            pltpu.make_async_copy(y_hbm.at[pl.ds(s, 1)], yg.at[slot, k, pl.ds(r, 1)], sems.at[slot]).start()

    def wait_rows(slot):
        for k in range(TOP_K):
            pltpu.make_async_copy(y_hbm.at[pl.ds(0, ROW_TILE)], yg.at[slot, k], sems.at[slot]).wait()

    @pl.when(j == 0)
    def _():
        for slot in range(2):
            def body(r, carry, slot=slot):
                start_row(pos_ref, slot, r)
                return carry
            lax.fori_loop(0, ROW_TILE, body, 0, unroll=8)

    for slot in range(2):
        rows = pl.ds(slot * ROW_TILE, ROW_TILE)
        wait_rows(slot)
        h = h_ref[rows, :]
        gate = gate_ref[rows, :]
        for k in range(TOP_K):
            h = h + gate[:, k:k + 1] * yg[slot, k]
        o_ref[rows, :] = _ple_tile(h, p_ref[rows, :], g_ref, wg_ref, wp_ref, fn_ref)
        for r in range(ROW_TILE):
            start_row(posn_ref, slot, r)

    @pl.when(j == pl.num_programs(0) - 1)
    def _():
        for slot in range(2):
            wait_rows(slot)


def _ple(h, moe_in, p, g, wg, wp, final_g):
    moe = moe_in is not None
    final = final_g is not None
    tile = 2 * ROW_TILE if moe else ROW_TILE
    n = TOKENS // tile
    row = lambda width: pl.BlockSpec((tile, width), lambda i: (i, 0))
    in_specs, args, scratch = [], [], []
    if moe:
        pos, gate, y = moe_in
        pos3 = pos.reshape(n, 2, TOP_K * ROW_TILE)
        smem = lambda imap: pl.BlockSpec((1, 2, TOP_K * ROW_TILE), imap, memory_space=pltpu.SMEM)
        in_specs += [smem(lambda i: (i, 0, 0)), smem(lambda i: (jnp.minimum(i + 1, n - 1), 0, 0)),
                     row(TOP_K), pl.BlockSpec(memory_space=pl.ANY)]
        args += [pos3, pos3, gate, y]
        scratch = [pltpu.VMEM((2, TOP_K, ROW_TILE, D_MODEL), F32), pltpu.SemaphoreType.DMA((2,))]
    in_specs += [row(D_MODEL), row(PLE_DIM), _resident((1, D_MODEL)), _resident((D_MODEL, D_MODEL)),
                 _resident((PLE_DIM, D_MODEL))]
    args += [h, p, g, wg, wp]
    if final:
        in_specs.append(_resident((1, D_MODEL)))
        args.append(final_g)
    return pl.pallas_call(
        functools.partial(_ple_moe_kernel if moe else _ple_kernel, final=final),
        grid=(n,),
        in_specs=in_specs,
        out_specs=row(D_MODEL),
        out_shape=jax.ShapeDtypeStruct((TOKENS, D_MODEL), F32),
        scratch_shapes=scratch,
        compiler_params=_cparams(("arbitrary",)),
        name="ple_moe" if moe else "ple",
    )(*args)


def _t5_bucket(dist):
    max_exact = REL_BUCKETS // 2
    d = jnp.maximum(dist, 0)
    large = max_exact + (jnp.log(jnp.maximum(d, 1).astype(jnp.float32) / max_exact)
                         / math.log(REL_MAX_DIST / max_exact)
                         * (REL_BUCKETS - max_exact)).astype(jnp.int32)
    large = jnp.minimum(large, REL_BUCKETS - 1)
    return jnp.where(d < max_exact, d, large)


def kernel(x, p, w_in, sinks, rel_bias, lb_logits, hgrn_norm, w_branch_a, w_branch_b, w_out,
           norm_mix, norm_ffn, norm_ple, dense_w1, dense_w3, dense_w2, router_w, moe_w1,
           moe_w3, moe_w2, ple_proj, ple_gate, final_norm):
    qi = jnp.arange(ATT_BLOCK)[:, None]
    kj = jnp.arange(2 * ATT_BLOCK)[None, :]
    bucket = _t5_bucket(qi + ATT_BLOCK - kj)[None]
    band_bias = jnp.zeros((ATT_Q_HEADS, ATT_BLOCK, 2 * ATT_BLOCK), F32)
    for b in range(REL_BUCKETS):
        band_bias = jnp.where(bucket == b, rel_bias[b].astype(F32)[:, None, None], band_bias)

    bf = lambda w: w.astype(BF16)
    h = x.reshape(TOKENS, D_MODEL)
    pt = p.reshape(DEPTH, TOKENS, PLE_DIM)
    vec = lambda g: g.reshape(1, -1)
    n_moe = moe_w1.shape[0]
    w1_all = moe_w1.reshape(n_moe * N_EXPERTS * D_MODEL, FFN_EXPERT)
    w3_all = moe_w3.reshape(n_moe * N_EXPERTS * D_MODEL, FFN_EXPERT)
    w2_all = moe_w2.reshape(n_moe * N_EXPERTS * FFN_EXPERT, D_MODEL)

    for l in range(DEPTH):
        moe = l % 2 == 1
        proj = _in_proj(h, vec(norm_mix[l]), bf(w_in[l]))
        wa, wb, wo = bf(w_branch_a[l]), bf(w_branch_b[l]), bf(w_out[l])
        final_g = vec(final_norm) if l == DEPTH - 1 else None
        if not moe:
            (h,) = _mixer(proj, h, sinks[l], band_bias, lb_logits, vec(hgrn_norm[l]), wa, wb, wo, l)
            h = _dense_ffn(h, vec(norm_ffn[l]), bf(dense_w1[l // 2]), bf(dense_w3[l // 2]),
                           bf(dense_w2[l // 2]))
            y = None
        else:
            moe_in = (vec(norm_ffn[l]), router_w[l // 2].T, w1_all, w3_all, (l // 2) * MIXER_STEPS)
            h, u, idx, gate, w1b, w3b = _mixer(proj, h, sinks[l], band_bias, lb_logits, vec(hgrn_norm[l]),
                                               wa, wb, wo, l, moe_in)
            pos, tile_expert, n_used = _moe_plan(idx)
            xs, w2b = _dispatch(u, pos, w2_all, l // 2)
            y = _moe_experts(xs, tile_expert, n_used,
                             w1b.reshape(N_EXPERTS, D_MODEL, FFN_EXPERT),
                             w3b.reshape(N_EXPERTS, D_MODEL, FFN_EXPERT),
                             w2b.reshape(N_EXPERTS, FFN_EXPERT, D_MODEL))
            y = (pos, gate, y)
        h = _ple(h, y, pt[l], vec(norm_ple[l]), bf(ple_gate[l]), bf(ple_proj[l]), final_g)
    return h.reshape(BATCH, SEQ, D_MODEL)
```

```python
import functools
import math

import jax
import jax.numpy as jnp
from jax import lax
from jax.experimental import pallas as pl
from jax.experimental.pallas import tpu as pltpu

F32 = jnp.float32
BF16 = jnp.bfloat16

D_MODEL = 1024
BATCH = 4
SEQ = 4096
TOKENS = BATCH * SEQ
DEPTH = 4
ATT_Q_HEADS = 8
ATT_KV_HEADS = 2
ATT_HEAD_DIM = 64
ATT_GROUP = ATT_Q_HEADS // ATT_KV_HEADS
WINDOW = 128
ATT_BLOCK = 128
REL_BUCKETS = 32
REL_MAX_DIST = 128
HG_HEADS = 4
HG_DK = 128
HG_DV = 128
ATT_Q_W = ATT_Q_HEADS * ATT_HEAD_DIM
ATT_KV_W = ATT_KV_HEADS * ATT_HEAD_DIM
HG_K_W = HG_HEADS * HG_DK
HG_V_W = HG_HEADS * HG_DV
IN_SPLITS = (ATT_Q_W, ATT_KV_W, ATT_KV_W, HG_K_W, HG_K_W, HG_V_W, HG_V_W, D_MODEL, D_MODEL)
IN_WIDTH = sum(IN_SPLITS)
FFN_DENSE = 2816
N_EXPERTS = 8
TOP_K = 2
FFN_EXPERT = 3584
PLE_DIM = 256
EPS = 1e-6

V7X_VMEM_LIMIT_BYTES = 52 * 1024 * 1024
ROW_TILE = 512
HG_TILE = 128
DENSE_NF = 2
DENSE_F_TILE = FFN_DENSE // DENSE_NF
MOE_ROW_TILE = 512
DISPATCH_TILE = 512
DISPATCH_STEPS = TOKENS // DISPATCH_TILE
DISPATCH_PARTS = 8
assert HG_TILE == ATT_BLOCK
MIXER_STEPS = TOKENS // ATT_BLOCK
W13_ROWS = N_EXPERTS * D_MODEL // MIXER_STEPS
W2_ROWS = N_EXPERTS * FFN_EXPERT // DISPATCH_STEPS
MOE_NF = 2
MOE_F_TILE = FFN_EXPERT // MOE_NF
MOE_TILES = (TOKENS * TOP_K) // MOE_ROW_TILE + N_EXPERTS - 1
MOE_ROWS = MOE_TILES * MOE_ROW_TILE
COL_CHUNK = 512


def _cparams(sem):
    return pltpu.CompilerParams(dimension_semantics=sem, vmem_limit_bytes=V7X_VMEM_LIMIT_BYTES)


def _rms(x, g):
    return x * lax.rsqrt(jnp.mean(x * x, axis=-1, keepdims=True) + EPS) * g


def _sigmoid(x):
    return 1.0 / (1.0 + jnp.exp(-x))


def _dot(a, b):
    return jnp.dot(a, b, preferred_element_type=F32)


def _dot_nt(a, b):
    return lax.dot_general(a, b, (((1,), (1,)), ((), ())), preferred_element_type=F32)


def _resident(shape):
    nd = len(shape)
    return pl.BlockSpec(shape, lambda *_: (0,) * nd)


def _cast_rider(w2d, first_block, rows, n_steps, step_of):
    cols = w2d.shape[1]
    in_spec = pl.BlockSpec((rows, cols), lambda *ids: (first_block + step_of(*ids), 0))
    out_spec = pl.BlockSpec((rows, cols), lambda *ids: (step_of(*ids), 0))
    return in_spec, out_spec, jax.ShapeDtypeStruct((rows * n_steps, cols), BF16)


IN_DTYPES = (BF16, BF16, BF16, BF16, F32, BF16, BF16, BF16, BF16)
ATT_SCALE = ATT_HEAD_DIM ** -0.5


def _in_proj_kernel(*refs, n_riders):
    h_ref, g_ref, w_ref = refs[:3]
    rider_in = refs[3:3 + n_riders]
    out_refs = refs[3 + n_riders:len(refs) - n_riders]
    rider_out = refs[len(refs) - n_riders:]
    for wi_ref, wo_ref in zip(rider_in, rider_out):
        wo_ref[...] = wi_ref[...].astype(BF16)
    xn = _rms(h_ref[...], g_ref[...]).astype(BF16)
    main_refs, (ak_sw_ref, av_sw_ref) = out_refs[:len(IN_SPLITS)], out_refs[len(IN_SPLITS):]
    off = 0
    for idx, (o_ref, width) in enumerate(zip(main_refs, IN_SPLITS)):
        for c in range(0, width, COL_CHUNK):
            cw = min(COL_CHUNK, width - c)
            z = _dot(xn, w_ref[:, off + c:off + c + cw])
            if idx == 0:
                z = z * ATT_SCALE
            o_ref[:, c:c + cw] = z.astype(o_ref.dtype)
            if idx in (1, 2):
                sw_ref = ak_sw_ref if idx == 1 else av_sw_ref
                sw_ref[...] = jnp.concatenate([z[:, ATT_HEAD_DIM:], z[:, :ATT_HEAD_DIM]], axis=1).astype(sw_ref.dtype)
        off += width


def _in_proj(h, g, w, riders):
    n_steps = TOKENS // ROW_TILE
    row = lambda width: pl.BlockSpec((ROW_TILE, width), lambda i: (i, 0))
    widths = IN_SPLITS + (ATT_KV_W, ATT_KV_W)
    dtypes = IN_DTYPES + (BF16, BF16)
    in_specs = [row(D_MODEL), _resident((1, D_MODEL)), _resident((D_MODEL, IN_WIDTH))]
    args = [h, g, w]
    out_specs = [row(wd) for wd in widths]
    out_shape = [jax.ShapeDtypeStruct((TOKENS, wd), dt) for wd, dt in zip(widths, dtypes)]
    for w2d, first_block, rows, n_blocks in riders:
        assert n_blocks <= n_steps
        r_in, r_out, r_shape = _cast_rider(w2d, first_block, rows, n_blocks,
                                           lambda i, nb=n_blocks: jnp.minimum(i, nb - 1))
        in_specs.append(r_in)
        args.append(w2d)
        out_specs.append(r_out)
        out_shape.append(r_shape)
    outs = pl.pallas_call(
        functools.partial(_in_proj_kernel, n_riders=len(riders)),
        grid=(n_steps,),
        in_specs=in_specs,
        out_specs=out_specs,
        out_shape=out_shape,
        compiler_params=_cparams(("arbitrary",)),
        name="in_proj",
    )(*args)
    return outs[:len(widths)], outs[len(widths):]


def _block_diag(a, a_sw, g):
    lane = lax.broadcasted_iota(jnp.int32, a.shape, 1)
    low = lane < ATT_HEAD_DIM
    own, other = (a, a_sw) if g == 0 else (a_sw, a)
    zero = jnp.zeros_like(a)
    first = jnp.where(low, own, zero)
    second = jnp.where(low, zero, other)
    cat = jnp.concatenate
    return cat([cat([first, zero], 1), cat([second, zero], 1),
                cat([zero, first], 1), cat([zero, second], 1)], 0)


def _attn_block(sink_ref, q_ref, kc_ref, kp_ref, kcs_ref, kps_ref, vc_ref, vp_ref, vcs_ref, vps_ref,
                bias_ref):
    n = pl.program_id(1)
    band = 2 * ATT_BLOCK
    r = lax.broadcasted_iota(jnp.int32, (ATT_BLOCK, band), 0)
    j = lax.broadcasted_iota(jnp.int32, (ATT_BLOCK, band), 1)
    dist = r + ATT_BLOCK - j
    valid = (dist >= 0) & (dist < WINDOW) & ((j >= ATT_BLOCK) | (n > 0))
    cat = jnp.concatenate
    k, ks = cat([kp_ref[...], kc_ref[...]], 0), cat([kps_ref[...], kcs_ref[...]], 0)
    v, vs = cat([vp_ref[...], vc_ref[...]], 0), cat([vps_ref[...], vcs_ref[...]], 0)
    gw = ATT_GROUP * ATT_HEAD_DIM
    lane = lax.broadcasted_iota(jnp.int32, (ATT_BLOCK, gw), 1)
    outs = []
    for g in range(ATT_KV_HEADS):
        s_all = _dot_nt(q_ref[:, g * gw:(g + 1) * gw], _block_diag(k, ks, g))
        ps, dens = [], []
        for hh in range(ATT_GROUP):
            h = g * ATT_GROUP + hh
            s = s_all[:, hh * band:(hh + 1) * band] + bias_ref[h]
            s = jnp.where(valid, s, -1e30)
            sink = sink_ref[h]
            m = jnp.maximum(jnp.max(s, axis=-1, keepdims=True), sink)
            p = jnp.exp(s - m)
            dens.append(jnp.sum(p, axis=-1, keepdims=True) + jnp.exp(sink - m))
            ps.append(p.astype(BF16))
        o = _dot(cat(ps, 1), _block_diag(v, vs, g))
        den = dens[ATT_GROUP - 1]
        for hh in reversed(range(ATT_GROUP - 1)):
            den = jnp.where(lane < (hh + 1) * ATT_HEAD_DIM, dens[hh], den)
        outs.append(o / den)
    return cat(outs, 1)


def _cumsum_rows(x, t):
    shift = 1
    while shift < x.shape[0]:
        x = x + jnp.where(t >= shift, pltpu.roll(x, shift, 0), 0.0)
        shift *= 2
    return x


def _hgrn_tile(q_ref, f_ref, v_ref, g_ref, lbl_ref, ng_ref, st_ref, layer):
    rows = [lbl_ref[i:i + 1, :] for i in range(DEPTH)]
    mx = functools.reduce(jnp.maximum, rows)
    ex = [jnp.exp(rw - mx) for rw in rows]
    tot = functools.reduce(lambda a, b: a + b, ex)
    lower = jnp.zeros_like(mx)
    for i in range(1, layer + 1):
        lower = lower + ex[i] / tot

    c = HG_TILE
    t = lax.broadcasted_iota(jnp.int32, (c, HG_DK), 0)
    ts = lax.broadcasted_iota(jnp.int32, (c, c), 0)
    ss = lax.broadcasted_iota(jnp.int32, (c, c), 1)
    same64 = (ts >> 6) == (ss >> 6)
    diag32 = ((ts >> 5) == (ss >> 5)) & (ss <= ts)
    ng = ng_ref[...]

    outs = []
    for h in range(HG_HEADS):
        sl = slice(h * HG_DK, (h + 1) * HG_DK)
        lb = lower[:, sl]
        f = lb + (1.0 - lb) * _sigmoid(f_ref[:, sl])
        k = 1.0 - f
        a = _cumsum_rows(jnp.log(f), t)
        a_last = a[c - 1:c, :]
        q = q_ref[:, sl].astype(F32)
        v = v_ref[:, sl]
        st = st_ref[h]

        o = _dot_nt((q * jnp.exp(a)).astype(BF16), st.astype(BF16))
        ke = (k * jnp.exp(a_last - a)).astype(BF16)
        vt = v.astype(F32).T.astype(BF16)
        st_ref[h] = st * jnp.exp(a_last) + _dot(vt, ke)

        m1 = a[63:64, :]
        q1 = jnp.where(t >= 64, q * jnp.exp(a - m1), 0.0)
        k1 = jnp.where(t < 64, k * jnp.exp(m1 - a), 0.0)
        p = _dot_nt(q1.astype(BF16), k1.astype(BF16))

        m2 = jnp.where(t < 64, a[31:32, :], a[95:96, :])
        second = (t & 63) >= 32
        q2 = jnp.where(second, q * jnp.exp(a - m2), 0.0)
        k2 = jnp.where(second, 0.0, k * jnp.exp(m2 - a))
        p = p + jnp.where(same64, _dot_nt(q2.astype(BF16), k2.astype(BF16)), 0.0)

        m3 = jnp.where(t < 64,
                       jnp.where(t < 32, a[15:16, :], a[47:48, :]),
                       jnp.where(t < 96, a[79:80, :], a[111:112, :]))
        q3 = q * jnp.exp(a - m3)
        k3 = k * jnp.exp(m3 - a)
        p = p + jnp.where(diag32, _dot_nt(q3.astype(BF16), k3.astype(BF16)), 0.0)

        o = o + _dot(p.astype(BF16), v)
        o = o * lax.rsqrt(jnp.mean(o * o, axis=-1, keepdims=True) + EPS) * ng
        gate = g_ref[:, sl].astype(F32)
        outs.append(o * (gate * _sigmoid(gate)))
    return jnp.concatenate(outs, 1)


def _route(u, rw_ref):
    ls = [jnp.sum(u * rw_ref[e:e + 1, :], axis=-1, keepdims=True) for e in range(N_EXPERTS)]
    m1 = functools.reduce(jnp.maximum, ls)
    i1 = jnp.full(m1.shape, N_EXPERTS, jnp.int32)
    for e in reversed(range(N_EXPERTS)):
        i1 = jnp.where(ls[e] == m1, e, i1)
    rest_ls = [jnp.where(i1 == e, -jnp.inf, ls[e]) for e in range(N_EXPERTS)]
    m2 = functools.reduce(jnp.maximum, rest_ls)
    i2 = jnp.full(m1.shape, N_EXPERTS, jnp.int32)
    for e in reversed(range(N_EXPERTS)):
        i2 = jnp.where((rest_ls[e] == m2) & (i1 != e), e, i2)
    e2 = jnp.exp(m2 - m1)
    den = 1.0 + e2
    return jnp.concatenate([i1, i2], axis=1), jnp.concatenate([1.0 / den, e2 / den], axis=1)


N_ATT_IN = 11
N_HGRN_IN = 6
N_MERGE_IN = 6


def _mixer_kernel(*refs, layer, moe):
    refs = list(refs)
    att_in = refs[:N_ATT_IN]
    hg_in = refs[N_ATT_IN:N_ATT_IN + N_HGRN_IN]
    ga_ref, gb_ref, h_ref, wa_ref, wb_ref, wo_ref = refs[N_ATT_IN + N_HGRN_IN:N_ATT_IN + N_HGRN_IN + N_MERGE_IN]
    rest = refs[N_ATT_IN + N_HGRN_IN + N_MERGE_IN:]
    if moe:
        (gn_ref, rw_ref, w1_ref, w3_ref, ho_ref, u_ref, idx_ref, gate_ref, w1o_ref, w3o_ref, st_ref) = rest
        w1o_ref[...] = w1_ref[...].astype(BF16)
        w3o_ref[...] = w3_ref[...].astype(BF16)
    else:
        ho_ref, st_ref = rest

    @pl.when(pl.program_id(1) == 0)
    def _():
        st_ref[...] = jnp.zeros_like(st_ref)

    att = _attn_block(*att_in)
    hgo = _hgrn_tile(*hg_in, st_ref, layer)
    ya = _dot(att.astype(BF16), wa_ref[...])
    yb = _dot(hgo.astype(BF16), wb_ref[...])
    merged = _sigmoid(ga_ref[...].astype(F32)) * ya + _sigmoid(gb_ref[...].astype(F32)) * yb
    hn = h_ref[...] + _dot(merged.astype(BF16), wo_ref[...])
    ho_ref[...] = hn
    if moe:
        u = _rms(hn, gn_ref[...])
        bits = pltpu.bitcast(u.astype(BF16).astype(F32), jnp.uint32)
        u_ref[...] = (bits[:, :D_MODEL // 2] >> 16) | (bits[:, D_MODEL // 2:] & jnp.uint32(0xFFFF0000))
        idx_ref[...], gate_ref[...] = _route(u, rw_ref)


def _mixer(proj, h, sinks, band_bias, lb_logits, norm_g, wa, wb, wo, layer, moe_in=None):
    aq, ak, av, hq, hf, hi, hg, ga, gb, ak_sw, av_sw = proj
    moe = moe_in is not None
    nb = SEQ // ATT_BLOCK
    step = lambda b, n: b * nb + n
    cur = lambda b, n: (step(b, n), 0)
    prev = lambda b, n: (jnp.maximum(step(b, n) - 1, 0), 0)
    blk = lambda width: pl.BlockSpec((ATT_BLOCK, width), cur)
    kv_cur, kv_prev = blk(ATT_KV_W), pl.BlockSpec((ATT_BLOCK, ATT_KV_W), prev)
    in_specs = [
        pl.BlockSpec(memory_space=pltpu.SMEM), blk(ATT_Q_W),
        kv_cur, kv_prev, kv_cur, kv_prev, kv_cur, kv_prev, kv_cur, kv_prev,
        _resident((ATT_Q_HEADS, ATT_BLOCK, 2 * ATT_BLOCK)),
        blk(HG_K_W), blk(HG_K_W), blk(HG_V_W), blk(HG_V_W), _resident((DEPTH, HG_K_W)), _resident((1, HG_DV)),
        blk(D_MODEL), blk(D_MODEL), blk(D_MODEL),
        _resident((ATT_Q_W, D_MODEL)), _resident((HG_V_W, D_MODEL)), _resident((D_MODEL, D_MODEL)),
    ]
    args = [sinks, aq, ak, ak, ak_sw, ak_sw, av, av, av_sw, av_sw, band_bias,
            hq, hf, hi, hg, lb_logits, norm_g, ga, gb, h, wa, wb, wo]
    assert len(in_specs) == N_ATT_IN + N_HGRN_IN + N_MERGE_IN
    out_specs = [blk(D_MODEL)]
    out_shape = [jax.ShapeDtypeStruct((TOKENS, D_MODEL), F32)]
    if moe:
        gn, rw, w1_all, w3_all, first_block = moe_in
        in_specs += [_resident((1, D_MODEL)), _resident((N_EXPERTS, D_MODEL))]
        args += [gn, rw]
        out_specs += [blk(D_MODEL // 2), blk(TOP_K), blk(TOP_K)]
        out_shape += [jax.ShapeDtypeStruct((TOKENS, D_MODEL // 2), jnp.uint32),
                      jax.ShapeDtypeStruct((TOKENS, TOP_K), jnp.int32),
                      jax.ShapeDtypeStruct((TOKENS, TOP_K), F32)]
        for w_all in (w1_all, w3_all):
            r_in, r_out, r_shape = _cast_rider(w_all, first_block, W13_ROWS, MIXER_STEPS, step)
            in_specs.append(r_in)
            args.append(w_all)
            out_specs.append(r_out)
            out_shape.append(r_shape)
    return pl.pallas_call(
        functools.partial(_mixer_kernel, layer=layer, moe=moe),
        grid=(BATCH, nb),
        in_specs=in_specs,
        out_specs=out_specs,
        out_shape=out_shape,
        scratch_shapes=[pltpu.VMEM((HG_HEADS, HG_DV, HG_DK), F32)],
        compiler_params=_cparams(("arbitrary", "arbitrary")),
        name="mixer_moe" if moe else "mixer",
    )(*args)


def _swiglu_part(x, w1, w3, w2):
    a = _dot(x, w1)
    b = _dot(x, w3)
    return _dot((a * _sigmoid(a) * b).astype(BF16), w2)


def _dense_ffn_kernel(h_ref, g_ref, w1_ref, w3_ref, w2_ref, o_ref, x_scr, acc_scr):
    f = pl.program_id(1)

    @pl.when(f == 0)
    def _():
        x = _rms(h_ref[...], g_ref[...]).astype(BF16)
        x_scr[...] = x
        acc_scr[...] = _swiglu_part(x, w1_ref[...], w3_ref[...], w2_ref[...])

    @pl.when(f == 1)
    def _():
        o_ref[...] = h_ref[...] + (acc_scr[...] + _swiglu_part(x_scr[...], w1_ref[...], w3_ref[...], w2_ref[...]))


def _dense_ffn(h, g, w1, w3, w2):
    return pl.pallas_call(
        _dense_ffn_kernel,
        grid=(TOKENS // ROW_TILE, DENSE_NF),
        in_specs=[
            pl.BlockSpec((ROW_TILE, D_MODEL), lambda i, f: (i, 0)),
            _resident((1, D_MODEL)),
            pl.BlockSpec((D_MODEL, DENSE_F_TILE), lambda i, f: (0, f)),
            pl.BlockSpec((D_MODEL, DENSE_F_TILE), lambda i, f: (0, f)),
            pl.BlockSpec((DENSE_F_TILE, D_MODEL), lambda i, f: (f, 0)),
        ],
        out_specs=pl.BlockSpec((ROW_TILE, D_MODEL), lambda i, f: (i, 0)),
        out_shape=jax.ShapeDtypeStruct((TOKENS, D_MODEL), F32),
        scratch_shapes=[pltpu.VMEM((ROW_TILE, D_MODEL), BF16), pltpu.VMEM((ROW_TILE, D_MODEL), F32)],
        compiler_params=_cparams(("arbitrary", "arbitrary")),
        name="dense_ffn",
    )(h, g, w1, w3, w2)


def _dispatch_kernel(pos_ref, u_ref, w2_ref, xs_in_ref, xs_ref, w2o_ref, sem):
    del xs_in_ref
    tok = DISPATCH_TILE // DISPATCH_PARTS
    r2 = W2_ROWS // DISPATCH_PARTS
    for part in range(DISPATCH_PARTS):
        s2 = slice(part * r2, (part + 1) * r2)
        w2o_ref[s2, :] = w2_ref[s2, :].astype(BF16)
        for r in range(part * tok, (part + 1) * tok):
            for k in range(TOP_K):
                d = pos_ref[0, 0, TOP_K * r + k]
                pltpu.make_async_copy(u_ref.at[pl.ds(r, 1)], xs_ref.at[pl.ds(d, 1)], sem).start()

    for k in range(TOP_K):
        pltpu.make_async_copy(u_ref, xs_ref.at[pl.ds(0, DISPATCH_TILE)], sem).wait()


def _dispatch(u_packed, pos, w2_all, moe_layer):
    n = DISPATCH_STEPS
    xs0 = jnp.zeros((MOE_ROWS, D_MODEL // 2), jnp.uint32)
    w2_in, w2_out, w2_shape = _cast_rider(w2_all, moe_layer * n, W2_ROWS, n, lambda i: i)
    return pl.pallas_call(
        _dispatch_kernel,
        grid=(n,),
        in_specs=[
            pl.BlockSpec((1, 1, TOP_K * DISPATCH_TILE), lambda i: (i, 0, 0), memory_space=pltpu.SMEM),
            pl.BlockSpec((DISPATCH_TILE, D_MODEL // 2), lambda i: (i, 0)),
            w2_in,
            pl.BlockSpec(memory_space=pl.ANY),
        ],
        out_specs=[pl.BlockSpec(memory_space=pl.ANY), w2_out],
        out_shape=[jax.ShapeDtypeStruct((MOE_ROWS, D_MODEL // 2), jnp.uint32), w2_shape],
        scratch_shapes=[pltpu.SemaphoreType.DMA(())],
        input_output_aliases={3: 0},
        compiler_params=_cparams(("arbitrary",)),
        name="moe_dispatch",
    )(pos.reshape(n, 1, TOP_K * DISPATCH_TILE), u_packed, w2_all, xs0)


def _moe_kernel(te_ref, nu_ref, x_ref, w1_ref, w3_ref, w2_ref, y_ref, xb, acc):
    i = pl.program_id(0)
    f = pl.program_id(1)
    valid = i < nu_ref[0]

    @pl.when((f == 0) & valid)
    def _():
        w = x_ref[...]
        x = jnp.concatenate([pltpu.bitcast(w << 16, F32).astype(BF16),
                             pltpu.bitcast(w & jnp.uint32(0xFFFF0000), F32).astype(BF16)], axis=1)
        xb[...] = x
        acc[...] = _swiglu_part(x, w1_ref[0], w3_ref[0], w2_ref[0])

    @pl.when((f == 1) & valid)
    def _():
        y_ref[...] = acc[...] + _swiglu_part(xb[...], w1_ref[0], w3_ref[0], w2_ref[0])

    @pl.when((f == 1) & jnp.logical_not(valid))
    def _():
        y_ref[...] = jnp.zeros_like(y_ref)


def _moe_experts(xs, tile_expert, n_used, w1, w3, w2):
    row_blk = lambda i, f, te, nu: (jnp.minimum(i, nu[0] - 1), 0)
    f_blk = lambda i, f, nu: jnp.where(i < nu[0], f, MOE_NF - 1)
    grid_spec = pltpu.PrefetchScalarGridSpec(
        num_scalar_prefetch=2,
        grid=(MOE_TILES, MOE_NF),
        in_specs=[
            pl.BlockSpec((MOE_ROW_TILE, D_MODEL // 2), row_blk),
            pl.BlockSpec((1, D_MODEL, MOE_F_TILE), lambda i, f, te, nu: (te[i], 0, f_blk(i, f, nu))),
            pl.BlockSpec((1, D_MODEL, MOE_F_TILE), lambda i, f, te, nu: (te[i], 0, f_blk(i, f, nu))),
            pl.BlockSpec((1, MOE_F_TILE, D_MODEL), lambda i, f, te, nu: (te[i], f_blk(i, f, nu), 0)),
        ],
        out_specs=pl.BlockSpec((MOE_ROW_TILE, D_MODEL), lambda i, f, te, nu: (i, 0)),
        scratch_shapes=[
            pltpu.VMEM((MOE_ROW_TILE, D_MODEL), BF16),
            pltpu.VMEM((MOE_ROW_TILE, D_MODEL), F32),
        ],
    )
    return pl.pallas_call(
        _moe_kernel,
        grid_spec=grid_spec,
        out_shape=jax.ShapeDtypeStruct((MOE_ROWS, D_MODEL), F32),
        compiler_params=_cparams(("arbitrary", "arbitrary")),
        name="moe_experts",
    )(tile_expert, n_used, xs, w1, w3, w2)


def _moe_plan(idx):
    e_flat = idx.reshape(-1)
    onehot = (e_flat[:, None] == jnp.arange(N_EXPERTS, dtype=jnp.int32)[None, :]).astype(jnp.int32)
    csum = jnp.cumsum(onehot, axis=0)
    counts = csum[-1]
    rank = jnp.sum((csum - onehot) * onehot, axis=1)
    tiles_per = (counts + MOE_ROW_TILE - 1) // MOE_ROW_TILE
    tile_end = jnp.cumsum(tiles_per)
    tile_start = tile_end - tiles_per
    pos = jnp.sum(onehot * tile_start[None, :], axis=1) * MOE_ROW_TILE + rank
    tile_id = jnp.arange(MOE_TILES, dtype=jnp.int32)
    te = jnp.sum((tile_id[:, None] >= tile_end[None, :]).astype(jnp.int32), axis=1)
    n_used = tile_end[-1]
    te = jnp.minimum(te, N_EXPERTS - 1)
    te = jnp.where(tile_id < n_used, te, te[jnp.maximum(n_used - 1, 0)])
    return pos.astype(jnp.int32), te.astype(jnp.int32), n_used.reshape(1).astype(jnp.int32)


def _ple_tile(h, p, g_ref, wg_ref, wp_ref, fn_ref):
    u = _rms(h, g_ref[...]).astype(BF16)
    emb_gate = _sigmoid(_dot(u, wg_ref[...]))
    h = h + emb_gate * _dot(p.astype(BF16), wp_ref[...])
    if fn_ref is not None:
        h = _rms(h, fn_ref[...])
    return h


def _ple_kernel(h_ref, p_ref, g_ref, wg_ref, wp_ref, *rest, final):
    fn_ref, o_ref = rest if final else (None,) + rest
    o_ref[...] = _ple_tile(h_ref[...], p_ref[...], g_ref, wg_ref, wp_ref, fn_ref)


def _ple_moe_kernel(pos_ref, posn_ref, gate_ref, y_hbm, h_ref, p_ref, g_ref, wg_ref, wp_ref, *rest, final):
    if final:
        fn_ref, o_ref, yg, sems = rest
    else:
        fn_ref, (o_ref, yg, sems) = None, rest
    j = pl.program_id(0)

    def start_row(idx_ref, slot, r):
        for k in range(TOP_K):
            s = idx_ref[0, slot, TOP_K * r + k]
            pltpu.make_async_copy(y_hbm.at[pl.ds(s, 1)], yg.at[slot, k, pl.ds(r, 1)], sems.at[slot]).start()

    def wait_rows(slot):
        for k in range(TOP_K):
            pltpu.make_async_copy(y_hbm.at[pl.ds(0, ROW_TILE)], yg.at[slot, k], sems.at[slot]).wait()

    @pl.when(j == 0)
    def _():
        for slot in range(2):
            def body(r, carry, slot=slot):
                start_row(pos_ref, slot, r)
                return carry
            lax.fori_loop(0, ROW_TILE, body, 0, unroll=8)

    for slot in range(2):
        rows = pl.ds(slot * ROW_TILE, ROW_TILE)
        wait_rows(slot)
        h = h_ref[rows, :]
        gate = gate_ref[rows, :]
        for k in range(TOP_K):
            h = h + gate[:, k:k + 1] * yg[slot, k]
        o_ref[rows, :] = _ple_tile(h, p_ref[rows, :], g_ref, wg_ref, wp_ref, fn_ref)
        for r in range(ROW_TILE):
            start_row(posn_ref, slot, r)

    @pl.when(j == pl.num_programs(0) - 1)
    def _():
        for slot in range(2):
            wait_rows(slot)


def _ple(h, moe_in, p, g, wg, wp, final_g):
    moe = moe_in is not None
    final = final_g is not None
    tile = 2 * ROW_TILE if moe else ROW_TILE
    n = TOKENS // tile
    row = lambda width: pl.BlockSpec((tile, width), lambda i: (i, 0))
    in_specs, args, scratch = [], [], []
    if moe:
        pos, gate, y = moe_in
        pos3 = pos.reshape(n, 2, TOP_K * ROW_TILE)
        smem = lambda imap: pl.BlockSpec((1, 2, TOP_K * ROW_TILE), imap, memory_space=pltpu.SMEM)
        in_specs += [smem(lambda i: (i, 0, 0)), smem(lambda i: (jnp.minimum(i + 1, n - 1), 0, 0)),
                     row(TOP_K), pl.BlockSpec(memory_space=pl.ANY)]
        args += [pos3, pos3, gate, y]
        scratch = [pltpu.VMEM((2, TOP_K, ROW_TILE, D_MODEL), F32), pltpu.SemaphoreType.DMA((2,))]
    in_specs += [row(D_MODEL), row(PLE_DIM), _resident((1, D_MODEL)), _resident((D_MODEL, D_MODEL)),
                 _resident((PLE_DIM, D_MODEL))]
    args += [h, p, g, wg, wp]
    if final:
        in_specs.append(_resident((1, D_MODEL)))
        args.append(final_g)
    return pl.pallas_call(
        functools.partial(_ple_moe_kernel if moe else _ple_kernel, final=final),
        grid=(n,),
        in_specs=in_specs,
        out_specs=row(D_MODEL),
        out_shape=jax.ShapeDtypeStruct((TOKENS, D_MODEL), F32),
        scratch_shapes=scratch,
        compiler_params=_cparams(("arbitrary",)),
        name="ple_moe" if moe else "ple",
    )(*args)


def _t5_bucket(dist):
    max_exact = REL_BUCKETS // 2
    d = jnp.maximum(dist, 0)
    large = max_exact + (jnp.log(jnp.maximum(d, 1).astype(jnp.float32) / max_exact)
                         / math.log(REL_MAX_DIST / max_exact)
                         * (REL_BUCKETS - max_exact)).astype(jnp.int32)
    large = jnp.minimum(large, REL_BUCKETS - 1)
    return jnp.where(d < max_exact, d, large)


def kernel(x, p, w_in, sinks, rel_bias, lb_logits, hgrn_norm, w_branch_a, w_branch_b, w_out,
           norm_mix, norm_ffn, norm_ple, dense_w1, dense_w3, dense_w2, router_w, moe_w1,
           moe_w3, moe_w2, ple_proj, ple_gate, final_norm):
    qi = jnp.arange(ATT_BLOCK)[:, None]
    kj = jnp.arange(2 * ATT_BLOCK)[None, :]
    bucket = _t5_bucket(qi + ATT_BLOCK - kj)[None]
    band_bias = jnp.zeros((ATT_Q_HEADS, ATT_BLOCK, 2 * ATT_BLOCK), F32)
    for b in range(REL_BUCKETS):
        band_bias = jnp.where(bucket == b, rel_bias[b].astype(F32)[:, None, None], band_bias)

    bf = lambda w: w.astype(BF16)
    h = x.reshape(TOKENS, D_MODEL)
    pt = p.reshape(DEPTH, TOKENS, PLE_DIM)
    vec = lambda g: g.reshape(1, -1)
    n_moe = moe_w1.shape[0]
    w1_all = moe_w1.reshape(n_moe * N_EXPERTS * D_MODEL, FFN_EXPERT)
    w3_all = moe_w3.reshape(n_moe * N_EXPERTS * D_MODEL, FFN_EXPERT)
    w2_all = moe_w2.reshape(n_moe * N_EXPERTS * FFN_EXPERT, D_MODEL)

    def stacked(w):
        return w.reshape(w.shape[0] * w.shape[1], w.shape[2])

    def layer_rider(w, index, rows):
        blocks = w.shape[1] // rows
        return stacked(w), index * blocks, rows, blocks

    w_in_next = bf(w_in[0])
    for l in range(DEPTH):
        moe = l % 2 == 1
        riders = [layer_rider(w_branch_a, l, 16), layer_rider(w_branch_b, l, 16),
                  layer_rider(w_out, l, 32), layer_rider(ple_gate, l, 32)]
        if not moe:
            riders += [layer_rider(dense_w1, l // 2, 32), layer_rider(dense_w3, l // 2, 32),
                       layer_rider(dense_w2, l // 2, 176)]
        if l + 1 < DEPTH:
            riders.append(layer_rider(w_in, l + 1, 32))
        proj, cast = _in_proj(h, vec(norm_mix[l]), w_in_next, riders)
        wa, wb, wo, wpg = cast[:4]
        if l + 1 < DEPTH:
            w_in_next = cast[-1]
        final_g = vec(final_norm) if l == DEPTH - 1 else None
        if not moe:
            (h,) = _mixer(proj, h, sinks[l], band_bias, lb_logits, vec(hgrn_norm[l]), wa, wb, wo, l)
            h = _dense_ffn(h, vec(norm_ffn[l]), cast[4], cast[5], cast[6])
            y = None
        else:
            moe_in = (vec(norm_ffn[l]), router_w[l // 2].T, w1_all, w3_all, (l // 2) * MIXER_STEPS)
            h, u, idx, gate, w1b, w3b = _mixer(proj, h, sinks[l], band_bias, lb_logits, vec(hgrn_norm[l]),
                                               wa, wb, wo, l, moe_in)
            pos, tile_expert, n_used = _moe_plan(idx)
            xs, w2b = _dispatch(u, pos, w2_all, l // 2)
            y = _moe_experts(xs, tile_expert, n_used,
                             w1b.reshape(N_EXPERTS, D_MODEL, FFN_EXPERT),
                             w3b.reshape(N_EXPERTS, D_MODEL, FFN_EXPERT),
                             w2b.reshape(N_EXPERTS, FFN_EXPERT, D_MODEL))
            y = (pos, gate, y)
        h = _ple(h, y, pt[l], vec(norm_ple[l]), wpg, bf(ple_proj[l]), final_g)
    return h.reshape(BATCH, SEQ, D_MODEL)
```

```python
import functools
import math

import jax
import jax.numpy as jnp
from jax import lax
from jax.experimental import pallas as pl
from jax.experimental.pallas import tpu as pltpu

F32 = jnp.float32
BF16 = jnp.bfloat16

D_MODEL = 1024
BATCH = 4
SEQ = 4096
TOKENS = BATCH * SEQ
DEPTH = 4
ATT_Q_HEADS = 8
ATT_KV_HEADS = 2
ATT_HEAD_DIM = 64
ATT_GROUP = ATT_Q_HEADS // ATT_KV_HEADS
WINDOW = 128
ATT_BLOCK = 128
REL_BUCKETS = 32
REL_MAX_DIST = 128
HG_HEADS = 4
HG_DK = 128
HG_DV = 128
ATT_Q_W = ATT_Q_HEADS * ATT_HEAD_DIM
ATT_KV_W = ATT_KV_HEADS * ATT_HEAD_DIM
HG_K_W = HG_HEADS * HG_DK
HG_V_W = HG_HEADS * HG_DV
IN_SPLITS = (ATT_Q_W, ATT_KV_W, ATT_KV_W, HG_K_W, HG_K_W, HG_V_W, HG_V_W, D_MODEL, D_MODEL)
IN_WIDTH = sum(IN_SPLITS)
FFN_DENSE = 2816
N_EXPERTS = 8
TOP_K = 2
FFN_EXPERT = 3584
PLE_DIM = 256
EPS = 1e-6

V7X_VMEM_LIMIT_BYTES = 52 * 1024 * 1024
ROW_TILE = 512
HG_TILE = 128
DENSE_NF = 2
DENSE_F_TILE = FFN_DENSE // DENSE_NF
MOE_ROW_TILE = 512
DISPATCH_TILE = 512
DISPATCH_STEPS = TOKENS // DISPATCH_TILE
DISPATCH_PARTS = 8
assert HG_TILE == ATT_BLOCK
MIXER_BLOCKS = 2
MIXER_TILE = MIXER_BLOCKS * ATT_BLOCK
MIXER_STEPS = TOKENS // MIXER_TILE
W13_ROWS = N_EXPERTS * D_MODEL // MIXER_STEPS
W2_ROWS = N_EXPERTS * FFN_EXPERT // DISPATCH_STEPS
MOE_NF = 2
MOE_F_TILE = FFN_EXPERT // MOE_NF
MOE_TILES = (TOKENS * TOP_K) // MOE_ROW_TILE + N_EXPERTS - 1
MOE_ROWS = MOE_TILES * MOE_ROW_TILE
COL_CHUNK = 512


def _cparams(sem):
    return pltpu.CompilerParams(dimension_semantics=sem, vmem_limit_bytes=V7X_VMEM_LIMIT_BYTES)


def _rms(x, g):
    return x * lax.rsqrt(jnp.mean(x * x, axis=-1, keepdims=True) + EPS) * g


def _sigmoid(x):
    return 1.0 / (1.0 + jnp.exp(-x))


def _dot(a, b):
    return jnp.dot(a, b, preferred_element_type=F32)


def _dot_nt(a, b):
    return lax.dot_general(a, b, (((1,), (1,)), ((), ())), preferred_element_type=F32)


def _resident(shape):
    nd = len(shape)
    return pl.BlockSpec(shape, lambda *_: (0,) * nd)


def _cast_rider(w2d, first_block, rows, n_steps, step_of):
    cols = w2d.shape[1]
    in_spec = pl.BlockSpec((rows, cols), lambda *ids: (first_block + step_of(*ids), 0))
    out_spec = pl.BlockSpec((rows, cols), lambda *ids: (step_of(*ids), 0))
    return in_spec, out_spec, jax.ShapeDtypeStruct((rows * n_steps, cols), BF16)


IN_DTYPES = (BF16, BF16, BF16, BF16, F32, BF16, BF16, BF16, BF16)
ATT_SCALE = ATT_HEAD_DIM ** -0.5


def _in_proj_kernel(*refs, n_riders):
    h_ref, g_ref, w_ref = refs[:3]
    rider_in = refs[3:3 + n_riders]
    out_refs = refs[3 + n_riders:len(refs) - n_riders]
    rider_out = refs[len(refs) - n_riders:]
    for wi_ref, wo_ref in zip(rider_in, rider_out):
        wo_ref[...] = wi_ref[...].astype(BF16)
    xn = _rms(h_ref[...], g_ref[...]).astype(BF16)
    main_refs, (ak_sw_ref, av_sw_ref) = out_refs[:len(IN_SPLITS)], out_refs[len(IN_SPLITS):]
    off = 0
    for idx, (o_ref, width) in enumerate(zip(main_refs, IN_SPLITS)):
        for c in range(0, width, COL_CHUNK):
            cw = min(COL_CHUNK, width - c)
            z = _dot(xn, w_ref[:, off + c:off + c + cw])
            if idx == 0:
                z = z * ATT_SCALE
            o_ref[:, c:c + cw] = z.astype(o_ref.dtype)
            if idx in (1, 2):
                sw_ref = ak_sw_ref if idx == 1 else av_sw_ref
                sw_ref[...] = jnp.concatenate([z[:, ATT_HEAD_DIM:], z[:, :ATT_HEAD_DIM]], axis=1).astype(sw_ref.dtype)
        off += width


def _in_proj(h, g, w, riders):
    n_steps = TOKENS // ROW_TILE
    row = lambda width: pl.BlockSpec((ROW_TILE, width), lambda i: (i, 0))
    widths = IN_SPLITS + (ATT_KV_W, ATT_KV_W)
    dtypes = IN_DTYPES + (BF16, BF16)
    in_specs = [row(D_MODEL), _resident((1, D_MODEL)), _resident((D_MODEL, IN_WIDTH))]
    args = [h, g, w]
    out_specs = [row(wd) for wd in widths]
    out_shape = [jax.ShapeDtypeStruct((TOKENS, wd), dt) for wd, dt in zip(widths, dtypes)]
    for w2d, first_block, rows, n_blocks in riders:
        assert n_blocks <= n_steps
        r_in, r_out, r_shape = _cast_rider(w2d, first_block, rows, n_blocks,
                                           lambda i, nb=n_blocks: jnp.minimum(i, nb - 1))
        in_specs.append(r_in)
        args.append(w2d)
        out_specs.append(r_out)
        out_shape.append(r_shape)
    outs = pl.pallas_call(
        functools.partial(_in_proj_kernel, n_riders=len(riders)),
        grid=(n_steps,),
        in_specs=in_specs,
        out_specs=out_specs,
        out_shape=out_shape,
        compiler_params=_cparams(("arbitrary",)),
        name="in_proj",
    )(*args)
    return outs[:len(widths)], outs[len(widths):]


def _block_diag(a, a_sw, g):
    lane = lax.broadcasted_iota(jnp.int32, a.shape, 1)
    low = lane < ATT_HEAD_DIM
    own, other = (a, a_sw) if g == 0 else (a_sw, a)
    zero = jnp.zeros_like(a)
    first = jnp.where(low, own, zero)
    second = jnp.where(low, zero, other)
    cat = jnp.concatenate
    return cat([cat([first, zero], 1), cat([second, zero], 1),
                cat([zero, first], 1), cat([zero, second], 1)], 0)


def _attn_block(sink_ref, q_ref, kc_ref, kp_ref, kcs_ref, kps_ref, vc_ref, vp_ref, vcs_ref, vps_ref,
                bias_ref, first):
    band = 2 * ATT_BLOCK
    r = lax.broadcasted_iota(jnp.int32, (ATT_BLOCK, band), 0)
    j = lax.broadcasted_iota(jnp.int32, (ATT_BLOCK, band), 1)
    dist = r + ATT_BLOCK - j
    valid = (dist >= 0) & (dist < WINDOW)
    if first is not False:
        valid = valid & ((j >= ATT_BLOCK) | jnp.logical_not(first))
    cat = jnp.concatenate
    k, ks = cat([kp_ref[...], kc_ref[...]], 0), cat([kps_ref[...], kcs_ref[...]], 0)
    v, vs = cat([vp_ref[...], vc_ref[...]], 0), cat([vps_ref[...], vcs_ref[...]], 0)
    gw = ATT_GROUP * ATT_HEAD_DIM
    lane = lax.broadcasted_iota(jnp.int32, (ATT_BLOCK, gw), 1)
    outs = []
    for g in range(ATT_KV_HEADS):
        s_all = _dot_nt(q_ref[:, g * gw:(g + 1) * gw], _block_diag(k, ks, g))
        ps, dens = [], []
        for hh in range(ATT_GROUP):
            h = g * ATT_GROUP + hh
            s = s_all[:, hh * band:(hh + 1) * band] + bias_ref[h]
            s = jnp.where(valid, s, -1e30)
            sink = sink_ref[h]
            m = jnp.maximum(jnp.max(s, axis=-1, keepdims=True), sink)
            p = jnp.exp(s - m)
            dens.append(jnp.sum(p, axis=-1, keepdims=True) + jnp.exp(sink - m))
            ps.append(p.astype(BF16))
        o = _dot(cat(ps, 1), _block_diag(v, vs, g))
        den = dens[ATT_GROUP - 1]
        for hh in reversed(range(ATT_GROUP - 1)):
            den = jnp.where(lane < (hh + 1) * ATT_HEAD_DIM, dens[hh], den)
        outs.append(o / den)
    return cat(outs, 1)


SUBLANES = 8


def _cumsum_rows(x, t):
    within = t & (SUBLANES - 1)
    shift = 1
    while shift < SUBLANES:
        x = x + jnp.where(within >= shift, pltpu.roll(x, shift, 0), 0.0)
        shift *= 2
    tiles = []
    carry = jnp.zeros_like(x[0:1, :])
    for r in range(0, x.shape[0], SUBLANES):
        tile = x[r:r + SUBLANES, :] + carry
        tiles.append(tile)
        carry = tile[SUBLANES - 1:SUBLANES, :]
    return jnp.concatenate(tiles, 0)


def _hgrn_tile(q_ref, f_ref, v_ref, g_ref, lbl_ref, ng_ref, st_ref, layer):
    rows = [lbl_ref[i:i + 1, :] for i in range(DEPTH)]
    mx = functools.reduce(jnp.maximum, rows)
    ex = [jnp.exp(rw - mx) for rw in rows]
    tot = functools.reduce(lambda a, b: a + b, ex)
    lower = jnp.zeros_like(mx)
    for i in range(1, layer + 1):
        lower = lower + ex[i] / tot

    c = HG_TILE
    t = lax.broadcasted_iota(jnp.int32, (c, HG_DK), 0)
    ts = lax.broadcasted_iota(jnp.int32, (c, c), 0)
    ss = lax.broadcasted_iota(jnp.int32, (c, c), 1)
    same64 = (ts >> 6) == (ss >> 6)
    diag32 = ((ts >> 5) == (ss >> 5)) & (ss <= ts)
    ng = ng_ref[...]

    outs = []
    for h in range(HG_HEADS):
        sl = slice(h * HG_DK, (h + 1) * HG_DK)
        lb = lower[:, sl]
        f = lb + (1.0 - lb) * _sigmoid(f_ref[:, sl])
        k = 1.0 - f
        a = _cumsum_rows(jnp.log(f), t)
        a_last = a[c - 1:c, :]
        q = q_ref[:, sl].astype(F32)
        v = v_ref[:, sl]
        st = st_ref[h]

        o = _dot_nt((q * jnp.exp(a)).astype(BF16), st.astype(BF16))
        ke = (k * jnp.exp(a_last - a)).astype(BF16)
        vt = v.astype(F32).T.astype(BF16)
        st_ref[h] = st * jnp.exp(a_last) + _dot(vt, ke)

        m1 = a[63:64, :]
        q1 = jnp.where(t >= 64, q * jnp.exp(a - m1), 0.0)
        k1 = jnp.where(t < 64, k * jnp.exp(m1 - a), 0.0)
        p = _dot_nt(q1.astype(BF16), k1.astype(BF16))

        m2 = jnp.where(t < 64, a[31:32, :], a[95:96, :])
        second = (t & 63) >= 32
        q2 = jnp.where(second, q * jnp.exp(a - m2), 0.0)
        k2 = jnp.where(second, 0.0, k * jnp.exp(m2 - a))
        p = p + jnp.where(same64, _dot_nt(q2.astype(BF16), k2.astype(BF16)), 0.0)

        m3 = jnp.where(t < 64,
                       jnp.where(t < 32, a[15:16, :], a[47:48, :]),
                       jnp.where(t < 96, a[79:80, :], a[111:112, :]))
        q3 = q * jnp.exp(a - m3)
        k3 = k * jnp.exp(m3 - a)
        p = p + jnp.where(diag32, _dot_nt(q3.astype(BF16), k3.astype(BF16)), 0.0)

        o = o + _dot(p.astype(BF16), v)
        o = o * lax.rsqrt(jnp.mean(o * o, axis=-1, keepdims=True) + EPS) * ng
        gate = g_ref[:, sl].astype(F32)
        outs.append(o * (gate * _sigmoid(gate)))
    return jnp.concatenate(outs, 1)


def _route(u, rw_ref):
    ls = [jnp.sum(u * rw_ref[e:e + 1, :], axis=-1, keepdims=True) for e in range(N_EXPERTS)]
    m1 = functools.reduce(jnp.maximum, ls)
    i1 = jnp.full(m1.shape, N_EXPERTS, jnp.int32)
    for e in reversed(range(N_EXPERTS)):
        i1 = jnp.where(ls[e] == m1, e, i1)
    rest_ls = [jnp.where(i1 == e, -jnp.inf, ls[e]) for e in range(N_EXPERTS)]
    m2 = functools.reduce(jnp.maximum, rest_ls)
    i2 = jnp.full(m1.shape, N_EXPERTS, jnp.int32)
    for e in reversed(range(N_EXPERTS)):
        i2 = jnp.where((rest_ls[e] == m2) & (i1 != e), e, i2)
    e2 = jnp.exp(m2 - m1)
    den = 1.0 + e2
    return jnp.concatenate([i1, i2], axis=1), jnp.concatenate([1.0 / den, e2 / den], axis=1)


N_ATT_IN = 11
N_HGRN_IN = 6
N_MERGE_IN = 6


def _mixer_kernel(*refs, layer, moe):
    refs = list(refs)
    att_in = refs[:N_ATT_IN]
    hg_in = refs[N_ATT_IN:N_ATT_IN + N_HGRN_IN]
    ga_ref, gb_ref, h_ref, wa_ref, wb_ref, wo_ref = refs[N_ATT_IN + N_HGRN_IN:N_ATT_IN + N_HGRN_IN + N_MERGE_IN]
    rest = refs[N_ATT_IN + N_HGRN_IN + N_MERGE_IN:]
    if moe:
        (gn_ref, rw_ref, w1_ref, w3_ref, ho_ref, u_ref, idx_ref, gate_ref, w1o_ref, w3o_ref, st_ref) = rest
        w1o_ref[...] = w1_ref[...].astype(BF16)
        w3o_ref[...] = w3_ref[...].astype(BF16)
    else:
        ho_ref, st_ref = rest

    @pl.when(pl.program_id(1) == 0)
    def _():
        st_ref[...] = jnp.zeros_like(st_ref)

    sink_ref, q_ref, kc_ref, kp_ref, kcs_ref, kps_ref, vc_ref, vp_ref, vcs_ref, vps_ref, bias_ref = att_in
    hq_ref, hf_ref, hi_ref, hgt_ref, lbl_ref, ng_ref = hg_in
    for sub in range(MIXER_BLOCKS):
        rows = pl.ds(sub * ATT_BLOCK, ATT_BLOCK)
        if sub == 0:
            prev = (kp_ref, kps_ref, vp_ref, vps_ref)
            first = pl.program_id(1) == 0
        else:
            before = pl.ds((sub - 1) * ATT_BLOCK, ATT_BLOCK)
            prev = tuple(r.at[before] for r in (kc_ref, kcs_ref, vc_ref, vcs_ref))
            first = False
        att = _attn_block(sink_ref, q_ref.at[rows], kc_ref.at[rows], prev[0], kcs_ref.at[rows], prev[1],
                          vc_ref.at[rows], prev[2], vcs_ref.at[rows], prev[3], bias_ref, first)
        hgo = _hgrn_tile(hq_ref.at[rows], hf_ref.at[rows], hi_ref.at[rows], hgt_ref.at[rows], lbl_ref, ng_ref,
                         st_ref, layer)
        ya = _dot(att.astype(BF16), wa_ref[...])
        yb = _dot(hgo.astype(BF16), wb_ref[...])
        merged = _sigmoid(ga_ref[rows, :]).astype(F32) * ya + _sigmoid(gb_ref[rows, :]).astype(F32) * yb
        hn = h_ref[rows, :] + _dot(merged.astype(BF16), wo_ref[...])
        ho_ref[rows, :] = hn
        if moe:
            u = _rms(hn, gn_ref[...])
            bits = pltpu.bitcast(u.astype(BF16).astype(F32), jnp.uint32)
            u_ref[rows, :] = (bits[:, :D_MODEL // 2] >> 16) | (bits[:, D_MODEL // 2:] & jnp.uint32(0xFFFF0000))
            idx_ref[rows, :], gate_ref[rows, :] = _route(u, rw_ref)


def _mixer(proj, h, sinks, band_bias, lb_logits, norm_g, wa, wb, wo, layer, moe_in=None):
    aq, ak, av, hq, hf, hi, hg, ga, gb, ak_sw, av_sw = proj
    moe = moe_in is not None
    nb = SEQ // MIXER_TILE
    step = lambda b, n: b * nb + n
    cur = lambda b, n: (step(b, n), 0)
    prev = lambda b, n: (jnp.maximum(step(b, n) * MIXER_BLOCKS - 1, 0), 0)
    blk = lambda width: pl.BlockSpec((MIXER_TILE, width), cur)
    kv_cur, kv_prev = blk(ATT_KV_W), pl.BlockSpec((ATT_BLOCK, ATT_KV_W), prev)
    in_specs = [
        pl.BlockSpec(memory_space=pltpu.SMEM), blk(ATT_Q_W),
        kv_cur, kv_prev, kv_cur, kv_prev, kv_cur, kv_prev, kv_cur, kv_prev,
        _resident((ATT_Q_HEADS, ATT_BLOCK, 2 * ATT_BLOCK)),
        blk(HG_K_W), blk(HG_K_W), blk(HG_V_W), blk(HG_V_W), _resident((DEPTH, HG_K_W)), _resident((1, HG_DV)),
        blk(D_MODEL), blk(D_MODEL), blk(D_MODEL),
        _resident((ATT_Q_W, D_MODEL)), _resident((HG_V_W, D_MODEL)), _resident((D_MODEL, D_MODEL)),
    ]
    args = [sinks, aq, ak, ak, ak_sw, ak_sw, av, av, av_sw, av_sw, band_bias,
            hq, hf, hi, hg, lb_logits, norm_g, ga, gb, h, wa, wb, wo]
    assert len(in_specs) == N_ATT_IN + N_HGRN_IN + N_MERGE_IN
    out_specs = [blk(D_MODEL)]
    out_shape = [jax.ShapeDtypeStruct((TOKENS, D_MODEL), F32)]
    if moe:
        gn, rw, w1_all, w3_all, first_block = moe_in
        in_specs += [_resident((1, D_MODEL)), _resident((N_EXPERTS, D_MODEL))]
        args += [gn, rw]
        out_specs += [blk(D_MODEL // 2), blk(TOP_K), blk(TOP_K)]
        out_shape += [jax.ShapeDtypeStruct((TOKENS, D_MODEL // 2), jnp.uint32),
                      jax.ShapeDtypeStruct((TOKENS, TOP_K), jnp.int32),
                      jax.ShapeDtypeStruct((TOKENS, TOP_K), F32)]
        for w_all in (w1_all, w3_all):
            r_in, r_out, r_shape = _cast_rider(w_all, first_block, W13_ROWS, MIXER_STEPS, step)
            in_specs.append(r_in)
            args.append(w_all)
            out_specs.append(r_out)
            out_shape.append(r_shape)
    return pl.pallas_call(
        functools.partial(_mixer_kernel, layer=layer, moe=moe),
        grid=(BATCH, nb),
        in_specs=in_specs,
        out_specs=out_specs,
        out_shape=out_shape,
        scratch_shapes=[pltpu.VMEM((HG_HEADS, HG_DV, HG_DK), F32)],
        compiler_params=_cparams(("arbitrary", "arbitrary")),
        name="mixer_moe" if moe else "mixer",
    )(*args)


def _swiglu_part(x, w1, w3, w2):
    a = _dot(x, w1)
    b = _dot(x, w3)
    return _dot((a * _sigmoid(a) * b).astype(BF16), w2)


def _dense_ffn_kernel(h_ref, g_ref, w1_ref, w3_ref, w2_ref, o_ref, x_scr, acc_scr):
    f = pl.program_id(1)

    @pl.when(f == 0)
    def _():
        x = _rms(h_ref[...], g_ref[...]).astype(BF16)
        x_scr[...] = x
        acc_scr[...] = _swiglu_part(x, w1_ref[...], w3_ref[...], w2_ref[...])

    @pl.when(f == 1)
    def _():
        o_ref[...] = h_ref[...] + (acc_scr[...] + _swiglu_part(x_scr[...], w1_ref[...], w3_ref[...], w2_ref[...]))


def _dense_ffn(h, g, w1, w3, w2):
    return pl.pallas_call(
        _dense_ffn_kernel,
        grid=(TOKENS // ROW_TILE, DENSE_NF),
        in_specs=[
            pl.BlockSpec((ROW_TILE, D_MODEL), lambda i, f: (i, 0)),
            _resident((1, D_MODEL)),
            pl.BlockSpec((D_MODEL, DENSE_F_TILE), lambda i, f: (0, f)),
            pl.BlockSpec((D_MODEL, DENSE_F_TILE), lambda i, f: (0, f)),
            pl.BlockSpec((DENSE_F_TILE, D_MODEL), lambda i, f: (f, 0)),
        ],
        out_specs=pl.BlockSpec((ROW_TILE, D_MODEL), lambda i, f: (i, 0)),
        out_shape=jax.ShapeDtypeStruct((TOKENS, D_MODEL), F32),
        scratch_shapes=[pltpu.VMEM((ROW_TILE, D_MODEL), BF16), pltpu.VMEM((ROW_TILE, D_MODEL), F32)],
        compiler_params=_cparams(("arbitrary", "arbitrary")),
        name="dense_ffn",
    )(h, g, w1, w3, w2)


def _dispatch_kernel(pos_ref, u_ref, w2_ref, xs_in_ref, xs_ref, w2o_ref, sem):
    del xs_in_ref
    tok = DISPATCH_TILE // DISPATCH_PARTS
    r2 = W2_ROWS // DISPATCH_PARTS
    for part in range(DISPATCH_PARTS):
        s2 = slice(part * r2, (part + 1) * r2)
        w2o_ref[s2, :] = w2_ref[s2, :].astype(BF16)
        for r in range(part * tok, (part + 1) * tok):
            for k in range(TOP_K):
                d = pos_ref[0, 0, TOP_K * r + k]
                pltpu.make_async_copy(u_ref.at[pl.ds(r, 1)], xs_ref.at[pl.ds(d, 1)], sem).start()

    for k in range(TOP_K):
        pltpu.make_async_copy(u_ref, xs_ref.at[pl.ds(0, DISPATCH_TILE)], sem).wait()


def _dispatch(u_packed, pos, w2_all, moe_layer):
    n = DISPATCH_STEPS
    xs0 = jnp.zeros((MOE_ROWS, D_MODEL // 2), jnp.uint32)
    w2_in, w2_out, w2_shape = _cast_rider(w2_all, moe_layer * n, W2_ROWS, n, lambda i: i)
    return pl.pallas_call(
        _dispatch_kernel,
        grid=(n,),
        in_specs=[
            pl.BlockSpec((1, 1, TOP_K * DISPATCH_TILE), lambda i: (i, 0, 0), memory_space=pltpu.SMEM),
            pl.BlockSpec((DISPATCH_TILE, D_MODEL // 2), lambda i: (i, 0)),
            w2_in,
            pl.BlockSpec(memory_space=pl.ANY),
        ],
        out_specs=[pl.BlockSpec(memory_space=pl.ANY), w2_out],
        out_shape=[jax.ShapeDtypeStruct((MOE_ROWS, D_MODEL // 2), jnp.uint32), w2_shape],
        scratch_shapes=[pltpu.SemaphoreType.DMA(())],
        input_output_aliases={3: 0},
        compiler_params=_cparams(("arbitrary",)),
        name="moe_dispatch",
    )(pos.reshape(n, 1, TOP_K * DISPATCH_TILE), u_packed, w2_all, xs0)


def _moe_kernel(te_ref, nu_ref, x_ref, w1_ref, w3_ref, w2_ref, y_ref, xb, acc):
    i = pl.program_id(0)
    f = pl.program_id(1)
    valid = i < nu_ref[0]

    @pl.when((f == 0) & valid)
    def _():
        w = x_ref[...]
        x = jnp.concatenate([pltpu.bitcast(w << 16, F32).astype(BF16),
                             pltpu.bitcast(w & jnp.uint32(0xFFFF0000), F32).astype(BF16)], axis=1)
        xb[...] = x
        acc[...] = _swiglu_part(x, w1_ref[0], w3_ref[0], w2_ref[0])

    @pl.when((f == 1) & valid)
    def _():
        y_ref[...] = acc[...] + _swiglu_part(xb[...], w1_ref[0], w3_ref[0], w2_ref[0])

    @pl.when((f == 1) & jnp.logical_not(valid))
    def _():
        y_ref[...] = jnp.zeros_like(y_ref)


def _moe_experts(xs, tile_expert, n_used, w1, w3, w2):
    row_blk = lambda i, f, te, nu: (jnp.minimum(i, nu[0] - 1), 0)
    f_blk = lambda i, f, nu: jnp.where(i < nu[0], f, MOE_NF - 1)
    grid_spec = pltpu.PrefetchScalarGridSpec(
        num_scalar_prefetch=2,
        grid=(MOE_TILES, MOE_NF),
        in_specs=[
            pl.BlockSpec((MOE_ROW_TILE, D_MODEL // 2), row_blk),
            pl.BlockSpec((1, D_MODEL, MOE_F_TILE), lambda i, f, te, nu: (te[i], 0, f_blk(i, f, nu))),
            pl.BlockSpec((1, D_MODEL, MOE_F_TILE), lambda i, f, te, nu: (te[i], 0, f_blk(i, f, nu))),
            pl.BlockSpec((1, MOE_F_TILE, D_MODEL), lambda i, f, te, nu: (te[i], f_blk(i, f, nu), 0)),
        ],
        out_specs=pl.BlockSpec((MOE_ROW_TILE, D_MODEL), lambda i, f, te, nu: (i, 0)),
        scratch_shapes=[
            pltpu.VMEM((MOE_ROW_TILE, D_MODEL), BF16),
            pltpu.VMEM((MOE_ROW_TILE, D_MODEL), F32),
        ],
    )
    return pl.pallas_call(
        _moe_kernel,
        grid_spec=grid_spec,
        out_shape=jax.ShapeDtypeStruct((MOE_ROWS, D_MODEL), F32),
        compiler_params=_cparams(("arbitrary", "arbitrary")),
        name="moe_experts",
    )(tile_expert, n_used, xs, w1, w3, w2)


def _moe_plan(idx):
    e_flat = idx.reshape(-1)
    onehot = (e_flat[:, None] == jnp.arange(N_EXPERTS, dtype=jnp.int32)[None, :]).astype(jnp.int32)
    csum = jnp.cumsum(onehot, axis=0)
    counts = csum[-1]
    rank = jnp.sum((csum - onehot) * onehot, axis=1)
    tiles_per = (counts + MOE_ROW_TILE - 1) // MOE_ROW_TILE
    tile_end = jnp.cumsum(tiles_per)
    tile_start = tile_end - tiles_per
    pos = jnp.sum(onehot * tile_start[None, :], axis=1) * MOE_ROW_TILE + rank
    tile_id = jnp.arange(MOE_TILES, dtype=jnp.int32)
    te = jnp.sum((tile_id[:, None] >= tile_end[None, :]).astype(jnp.int32), axis=1)
    n_used = tile_end[-1]
    te = jnp.minimum(te, N_EXPERTS - 1)
    te = jnp.where(tile_id < n_used, te, te[jnp.maximum(n_used - 1, 0)])
    return pos.astype(jnp.int32), te.astype(jnp.int32), n_used.reshape(1).astype(jnp.int32)


def _ple_tile(h, p, g_ref, wg_ref, wp_ref, fn_ref):
    u = _rms(h, g_ref[...]).astype(BF16)
    emb_gate = _sigmoid(_dot(u, wg_ref[...]))
    h = h + emb_gate * _dot(p.astype(BF16), wp_ref[...])
    if fn_ref is not None:
        h = _rms(h, fn_ref[...])
    return h


def _ple_kernel(h_ref, p_ref, g_ref, wg_ref, wp_ref, *rest, final):
    fn_ref, o_ref = rest if final else (None,) + rest
    o_ref[...] = _ple_tile(h_ref[...], p_ref[...], g_ref, wg_ref, wp_ref, fn_ref)


def _ple_moe_kernel(pos_ref, posn_ref, gate_ref, y_hbm, h_ref, p_ref, g_ref, wg_ref, wp_ref, *rest, final):
    if final:
        fn_ref, o_ref, yg, sems = rest
    else:
        fn_ref, (o_ref, yg, sems) = None, rest
    j = pl.program_id(0)

    def start_row(idx_ref, slot, r):
        for k in range(TOP_K):
            s = idx_ref[0, slot, TOP_K * r + k]
            pltpu.make_async_copy(y_hbm.at[pl.ds(s, 1)], yg.at[slot, k, pl.ds(r, 1)], sems.at[slot]).start()

    def wait_rows(slot):
        for k in range(TOP_K):
            pltpu.make_async_copy(y_hbm.at[pl.ds(0, ROW_TILE)], yg.at[slot, k], sems.at[slot]).wait()

    @pl.when(j == 0)
    def _():
        for slot in range(2):
            def body(r, carry, slot=slot):
                start_row(pos_ref, slot, r)
                return carry
            lax.fori_loop(0, ROW_TILE, body, 0, unroll=8)

    for slot in range(2):
        rows = pl.ds(slot * ROW_TILE, ROW_TILE)
        wait_rows(slot)
        h = h_ref[rows, :]
        gate = gate_ref[rows, :]
        for k in range(TOP_K):
            h = h + gate[:, k:k + 1] * yg[slot, k]
        o_ref[rows, :] = _ple_tile(h, p_ref[rows, :], g_ref, wg_ref, wp_ref, fn_ref)
        for r in range(ROW_TILE):
            start_row(posn_ref, slot, r)

    @pl.when(j == pl.num_programs(0) - 1)
    def _():
        for slot in range(2):
            wait_rows(slot)


def _ple(h, moe_in, p, g, wg, wp, final_g):
    moe = moe_in is not None
    final = final_g is not None
    tile = 2 * ROW_TILE if moe else ROW_TILE
    n = TOKENS // tile
    row = lambda width: pl.BlockSpec((tile, width), lambda i: (i, 0))
    in_specs, args, scratch = [], [], []
    if moe:
        pos, gate, y = moe_in
        pos3 = pos.reshape(n, 2, TOP_K * ROW_TILE)
        smem = lambda imap: pl.BlockSpec((1, 2, TOP_K * ROW_TILE), imap, memory_space=pltpu.SMEM)
        in_specs += [smem(lambda i: (i, 0, 0)), smem(lambda i: (jnp.minimum(i + 1, n - 1), 0, 0)),
                     row(TOP_K), pl.BlockSpec(memory_space=pl.ANY)]
        args += [pos3, pos3, gate, y]
        scratch = [pltpu.VMEM((2, TOP_K, ROW_TILE, D_MODEL), F32), pltpu.SemaphoreType.DMA((2,))]
    in_specs += [row(D_MODEL), row(PLE_DIM), _resident((1, D_MODEL)), _resident((D_MODEL, D_MODEL)),
                 _resident((PLE_DIM, D_MODEL))]
    args += [h, p, g, wg, wp]
    if final:
        in_specs.append(_resident((1, D_MODEL)))
        args.append(final_g)
    return pl.pallas_call(
        functools.partial(_ple_moe_kernel if moe else _ple_kernel, final=final),
        grid=(n,),
        in_specs=in_specs,
        out_specs=row(D_MODEL),
        out_shape=jax.ShapeDtypeStruct((TOKENS, D_MODEL), F32),
        scratch_shapes=scratch,
        compiler_params=_cparams(("arbitrary",)),
        name="ple_moe" if moe else "ple",
    )(*args)


def _t5_bucket(dist):
    max_exact = REL_BUCKETS // 2
    d = jnp.maximum(dist, 0)
    large = max_exact + (jnp.log(jnp.maximum(d, 1).astype(jnp.float32) / max_exact)
                         / math.log(REL_MAX_DIST / max_exact)
                         * (REL_BUCKETS - max_exact)).astype(jnp.int32)
    large = jnp.minimum(large, REL_BUCKETS - 1)
    return jnp.where(d < max_exact, d, large)


def kernel(x, p, w_in, sinks, rel_bias, lb_logits, hgrn_norm, w_branch_a, w_branch_b, w_out,
           norm_mix, norm_ffn, norm_ple, dense_w1, dense_w3, dense_w2, router_w, moe_w1,
           moe_w3, moe_w2, ple_proj, ple_gate, final_norm):
    qi = jnp.arange(ATT_BLOCK)[:, None]
    kj = jnp.arange(2 * ATT_BLOCK)[None, :]
    bucket = _t5_bucket(qi + ATT_BLOCK - kj)[None]
    band_bias = jnp.zeros((ATT_Q_HEADS, ATT_BLOCK, 2 * ATT_BLOCK), F32)
    for b in range(REL_BUCKETS):
        band_bias = jnp.where(bucket == b, rel_bias[b].astype(F32)[:, None, None], band_bias)

    bf = lambda w: w.astype(BF16)
    h = x.reshape(TOKENS, D_MODEL)
    pt = p.reshape(DEPTH, TOKENS, PLE_DIM)
    vec = lambda g: g.reshape(1, -1)
    n_moe = moe_w1.shape[0]
    w1_all = moe_w1.reshape(n_moe * N_EXPERTS * D_MODEL, FFN_EXPERT)
    w3_all = moe_w3.reshape(n_moe * N_EXPERTS * D_MODEL, FFN_EXPERT)
    w2_all = moe_w2.reshape(n_moe * N_EXPERTS * FFN_EXPERT, D_MODEL)

    def stacked(w):
        return w.reshape(w.shape[0] * w.shape[1], w.shape[2])

    def layer_rider(w, index, rows):
        blocks = w.shape[1] // rows
        return stacked(w), index * blocks, rows, blocks

    w_in_next = bf(w_in[0])
    for l in range(DEPTH):
        moe = l % 2 == 1
        riders = [layer_rider(w_branch_a, l, 16), layer_rider(w_branch_b, l, 16),
                  layer_rider(w_out, l, 32), layer_rider(ple_gate, l, 32)]
        if not moe:
            riders += [layer_rider(dense_w1, l // 2, 32), layer_rider(dense_w3, l // 2, 32),
                       layer_rider(dense_w2, l // 2, 176)]
        if l + 1 < DEPTH:
            riders.append(layer_rider(w_in, l + 1, 32))
        proj, cast = _in_proj(h, vec(norm_mix[l]), w_in_next, riders)
        wa, wb, wo, wpg = cast[:4]
        if l + 1 < DEPTH:
            w_in_next = cast[-1]
        final_g = vec(final_norm) if l == DEPTH - 1 else None
        if not moe:
            (h,) = _mixer(proj, h, sinks[l], band_bias, lb_logits, vec(hgrn_norm[l]), wa, wb, wo, l)
            h = _dense_ffn(h, vec(norm_ffn[l]), cast[4], cast[5], cast[6])
            y = None
        else:
            moe_in = (vec(norm_ffn[l]), router_w[l // 2].T, w1_all, w3_all, (l // 2) * MIXER_STEPS)
            h, u, idx, gate, w1b, w3b = _mixer(proj, h, sinks[l], band_bias, lb_logits, vec(hgrn_norm[l]),
                                               wa, wb, wo, l, moe_in)
            pos, tile_expert, n_used = _moe_plan(idx)
            xs, w2b = _dispatch(u, pos, w2_all, l // 2)
            y = _moe_experts(xs, tile_expert, n_used,
                             w1b.reshape(N_EXPERTS, D_MODEL, FFN_EXPERT),
                             w3b.reshape(N_EXPERTS, D_MODEL, FFN_EXPERT),
                             w2b.reshape(N_EXPERTS, FFN_EXPERT, D_MODEL))
            y = (pos, gate, y)
        h = _ple(h, y, pt[l], vec(norm_ple[l]), wpg, bf(ple_proj[l]), final_g)
    return h.reshape(BATCH, SEQ, D_MODEL)
```

```python
import functools
import math

import jax
import jax.numpy as jnp
from jax import lax
from jax.experimental import pallas as pl
from jax.experimental.pallas import tpu as pltpu

F32 = jnp.float32
BF16 = jnp.bfloat16

D_MODEL = 1024
BATCH = 4
SEQ = 4096
TOKENS = BATCH * SEQ
DEPTH = 4
ATT_Q_HEADS = 8
ATT_KV_HEADS = 2
ATT_HEAD_DIM = 64
ATT_GROUP = ATT_Q_HEADS // ATT_KV_HEADS
WINDOW = 128
ATT_BLOCK = 128
REL_BUCKETS = 32
REL_MAX_DIST = 128
HG_HEADS = 4
HG_DK = 128
HG_DV = 128
ATT_Q_W = ATT_Q_HEADS * ATT_HEAD_DIM
ATT_KV_W = ATT_KV_HEADS * ATT_HEAD_DIM
HG_K_W = HG_HEADS * HG_DK
HG_V_W = HG_HEADS * HG_DV
IN_SPLITS = (ATT_Q_W, ATT_KV_W, ATT_KV_W, HG_K_W, HG_K_W, HG_V_W, HG_V_W, D_MODEL, D_MODEL)
IN_WIDTH = sum(IN_SPLITS)
FFN_DENSE = 2816
N_EXPERTS = 8
TOP_K = 2
FFN_EXPERT = 3584
PLE_DIM = 256
EPS = 1e-6

V7X_VMEM_LIMIT_BYTES = 52 * 1024 * 1024
ROW_TILE = 512
HG_TILE = 128
DENSE_NF = 2
DENSE_F_TILE = FFN_DENSE // DENSE_NF
MOE_ROW_TILE = 512
DISPATCH_TILE = 1024
DISPATCH_STEPS = TOKENS // DISPATCH_TILE
DISPATCH_PARTS = 16
assert HG_TILE == ATT_BLOCK
MIXER_BLOCKS = 2
MIXER_TILE = MIXER_BLOCKS * ATT_BLOCK
MIXER_STEPS = TOKENS // MIXER_TILE
W13_ROWS = N_EXPERTS * D_MODEL // MIXER_STEPS
W2_ROWS = N_EXPERTS * FFN_EXPERT // DISPATCH_STEPS
MOE_NF = 2
MOE_F_TILE = FFN_EXPERT // MOE_NF
MOE_TILES = (TOKENS * TOP_K) // MOE_ROW_TILE + N_EXPERTS - 1
MOE_ROWS = MOE_TILES * MOE_ROW_TILE
COL_CHUNK = 512
ROUTER_LANES = 128


def _cparams(sem):
    return pltpu.CompilerParams(dimension_semantics=sem, vmem_limit_bytes=V7X_VMEM_LIMIT_BYTES)


def _rms(x, g):
    return x * lax.rsqrt(jnp.mean(x * x, axis=-1, keepdims=True) + EPS) * g


def _sigmoid(x):
    return 1.0 / (1.0 + jnp.exp(-x))


def _dot(a, b):
    return jnp.dot(a, b, preferred_element_type=F32)


def _dot_nt(a, b):
    return lax.dot_general(a, b, (((1,), (1,)), ((), ())), preferred_element_type=F32)


def _resident(shape):
    nd = len(shape)
    return pl.BlockSpec(shape, lambda *_: (0,) * nd)


def _cast_rider(w2d, first_block, rows, n_steps, step_of):
    cols = w2d.shape[1]
    in_spec = pl.BlockSpec((rows, cols), lambda *ids: (first_block + step_of(*ids), 0))
    out_spec = pl.BlockSpec((rows, cols), lambda *ids: (step_of(*ids), 0))
    return in_spec, out_spec, jax.ShapeDtypeStruct((rows * n_steps, cols), BF16)


IN_DTYPES = (BF16, BF16, BF16, BF16, F32, BF16, BF16, BF16, BF16)
ATT_SCALE = ATT_HEAD_DIM ** -0.5


def _in_proj_kernel(*refs, n_riders):
    h_ref, g_ref, w_ref = refs[:3]
    rider_in = refs[3:3 + n_riders]
    out_refs = refs[3 + n_riders:len(refs) - n_riders]
    rider_out = refs[len(refs) - n_riders:]
    for wi_ref, wo_ref in zip(rider_in, rider_out):
        wo_ref[...] = wi_ref[...].astype(BF16)
    xn = _rms(h_ref[...], g_ref[...]).astype(BF16)
    main_refs, (ak_sw_ref, av_sw_ref) = out_refs[:len(IN_SPLITS)], out_refs[len(IN_SPLITS):]
    off = 0
    for idx, (o_ref, width) in enumerate(zip(main_refs, IN_SPLITS)):
        for c in range(0, width, COL_CHUNK):
            cw = min(COL_CHUNK, width - c)
            z = _dot(xn, w_ref[:, off + c:off + c + cw])
            if idx == 0:
                z = z * ATT_SCALE
            o_ref[:, c:c + cw] = z.astype(o_ref.dtype)
            if idx in (1, 2):
                sw_ref = ak_sw_ref if idx == 1 else av_sw_ref
                sw_ref[...] = jnp.concatenate([z[:, ATT_HEAD_DIM:], z[:, :ATT_HEAD_DIM]], axis=1).astype(sw_ref.dtype)
        off += width


def _in_proj(h, g, w, riders):
    n_steps = TOKENS // ROW_TILE
    row = lambda width: pl.BlockSpec((ROW_TILE, width), lambda i: (i, 0))
    widths = IN_SPLITS + (ATT_KV_W, ATT_KV_W)
    dtypes = IN_DTYPES + (BF16, BF16)
    in_specs = [row(D_MODEL), _resident((1, D_MODEL)), _resident((D_MODEL, IN_WIDTH))]
    args = [h, g, w]
    out_specs = [row(wd) for wd in widths]
    out_shape = [jax.ShapeDtypeStruct((TOKENS, wd), dt) for wd, dt in zip(widths, dtypes)]
    for w2d, first_block, rows, n_blocks in riders:
        assert n_blocks <= n_steps
        r_in, r_out, r_shape = _cast_rider(w2d, first_block, rows, n_blocks,
                                           lambda i, nb=n_blocks: jnp.minimum(i, nb - 1))
        in_specs.append(r_in)
        args.append(w2d)
        out_specs.append(r_out)
        out_shape.append(r_shape)
    outs = pl.pallas_call(
        functools.partial(_in_proj_kernel, n_riders=len(riders)),
        grid=(n_steps,),
        in_specs=in_specs,
        out_specs=out_specs,
        out_shape=out_shape,
        compiler_params=_cparams(("arbitrary",)),
        name="in_proj",
    )(*args)
    return outs[:len(widths)], outs[len(widths):]


def _block_diag(a, a_sw, g):
    lane = lax.broadcasted_iota(jnp.int32, a.shape, 1)
    low = lane < ATT_HEAD_DIM
    own, other = (a, a_sw) if g == 0 else (a_sw, a)
    zero = jnp.zeros_like(a)
    first = jnp.where(low, own, zero)
    second = jnp.where(low, zero, other)
    cat = jnp.concatenate
    return cat([cat([first, zero], 1), cat([second, zero], 1),
                cat([zero, first], 1), cat([zero, second], 1)], 0)


def _attn_block(sink_ref, q_ref, kc_ref, kp_ref, kcs_ref, kps_ref, vc_ref, vp_ref, vcs_ref, vps_ref,
                bias_ref, first):
    band = 2 * ATT_BLOCK
    r = lax.broadcasted_iota(jnp.int32, (ATT_BLOCK, band), 0)
    j = lax.broadcasted_iota(jnp.int32, (ATT_BLOCK, band), 1)
    dist = r + ATT_BLOCK - j
    valid = (dist >= 0) & (dist < WINDOW)
    if first is not False:
        valid = valid & ((j >= ATT_BLOCK) | jnp.logical_not(first))
    cat = jnp.concatenate
    k, ks = cat([kp_ref[...], kc_ref[...]], 0), cat([kps_ref[...], kcs_ref[...]], 0)
    v, vs = cat([vp_ref[...], vc_ref[...]], 0), cat([vps_ref[...], vcs_ref[...]], 0)
    gw = ATT_GROUP * ATT_HEAD_DIM
    lane = lax.broadcasted_iota(jnp.int32, (ATT_BLOCK, gw), 1)
    outs = []
    for g in range(ATT_KV_HEADS):
        s_all = _dot_nt(q_ref[:, g * gw:(g + 1) * gw], _block_diag(k, ks, g))
        ps, dens = [], []
        for hh in range(ATT_GROUP):
            h = g * ATT_GROUP + hh
            s = s_all[:, hh * band:(hh + 1) * band] + bias_ref[h]
            s = jnp.where(valid, s, -1e30)
            sink = sink_ref[h]
            m = jnp.maximum(jnp.max(s, axis=-1, keepdims=True), sink)
            p = jnp.exp(s - m)
            dens.append(jnp.sum(p, axis=-1, keepdims=True) + jnp.exp(sink - m))
            ps.append(p.astype(BF16))
        o = _dot(cat(ps, 1), _block_diag(v, vs, g))
        den = dens[ATT_GROUP - 1]
        for hh in reversed(range(ATT_GROUP - 1)):
            den = jnp.where(lane < (hh + 1) * ATT_HEAD_DIM, dens[hh], den)
        outs.append(o / den)
    return cat(outs, 1)


SUBLANES = 8


def _cumsum_rows(x, t):
    within = t & (SUBLANES - 1)
    shift = 1
    while shift < SUBLANES:
        x = x + jnp.where(within >= shift, pltpu.roll(x, shift, 0), 0.0)
        shift *= 2
    tiles = []
    carry = jnp.zeros_like(x[0:1, :])
    for r in range(0, x.shape[0], SUBLANES):
        tile = x[r:r + SUBLANES, :] + carry
        tiles.append(tile)
        carry = tile[SUBLANES - 1:SUBLANES, :]
    return jnp.concatenate(tiles, 0)


def _hgrn_tile(q_ref, f_ref, v_ref, g_ref, lbl_ref, ng_ref, st_ref, layer):
    rows = [lbl_ref[i:i + 1, :] for i in range(DEPTH)]
    mx = functools.reduce(jnp.maximum, rows)
    ex = [jnp.exp(rw - mx) for rw in rows]
    tot = functools.reduce(lambda a, b: a + b, ex)
    lower = jnp.zeros_like(mx)
    for i in range(1, layer + 1):
        lower = lower + ex[i] / tot

    c = HG_TILE
    t = lax.broadcasted_iota(jnp.int32, (c, HG_DK), 0)
    ts = lax.broadcasted_iota(jnp.int32, (c, c), 0)
    ss = lax.broadcasted_iota(jnp.int32, (c, c), 1)
    same64 = (ts >> 6) == (ss >> 6)
    diag32 = ((ts >> 5) == (ss >> 5)) & (ss <= ts)
    ng = ng_ref[...]

    outs = []
    for h in range(HG_HEADS):
        sl = slice(h * HG_DK, (h + 1) * HG_DK)
        lb = lower[:, sl]
        f = lb + (1.0 - lb) * _sigmoid(f_ref[:, sl])
        k = 1.0 - f
        a = _cumsum_rows(jnp.log(f), t)
        a_last = a[c - 1:c, :]
        q = q_ref[:, sl].astype(F32)
        v = v_ref[:, sl]
        st = st_ref[h]

        o = _dot_nt((q * jnp.exp(a)).astype(BF16), st.astype(BF16))
        ke = (k * jnp.exp(a_last - a)).astype(BF16)
        vt = v.astype(F32).T.astype(BF16)
        st_ref[h] = st * jnp.exp(a_last) + _dot(vt, ke)

        m1 = a[63:64, :]
        q1 = jnp.where(t >= 64, q * jnp.exp(a - m1), 0.0)
        k1 = jnp.where(t < 64, k * jnp.exp(m1 - a), 0.0)
        p = _dot_nt(q1.astype(BF16), k1.astype(BF16))

        m2 = jnp.where(t < 64, a[31:32, :], a[95:96, :])
        second = (t & 63) >= 32
        q2 = jnp.where(second, q * jnp.exp(a - m2), 0.0)
        k2 = jnp.where(second, 0.0, k * jnp.exp(m2 - a))
        p = p + jnp.where(same64, _dot_nt(q2.astype(BF16), k2.astype(BF16)), 0.0)

        m3 = jnp.where(t < 64,
                       jnp.where(t < 32, a[15:16, :], a[47:48, :]),
                       jnp.where(t < 96, a[79:80, :], a[111:112, :]))
        q3 = q * jnp.exp(a - m3)
        k3 = k * jnp.exp(m3 - a)
        p = p + jnp.where(diag32, _dot_nt(q3.astype(BF16), k3.astype(BF16)), 0.0)

        o = o + _dot(p.astype(BF16), v)
        o = o * lax.rsqrt(jnp.mean(o * o, axis=-1, keepdims=True) + EPS) * ng
        gate = g_ref[:, sl].astype(F32)
        outs.append(o * (gate * _sigmoid(gate)))
    return jnp.concatenate(outs, 1)


def _route(u, rwh_ref, rwl_ref):
    u_hi = u.astype(BF16)
    u_lo = (u - u_hi.astype(F32)).astype(BF16)
    logits = _dot(u_hi, rwh_ref[...]) + (_dot(u_hi, rwl_ref[...]) + _dot(u_lo, rwh_ref[...]))
    lane = lax.broadcasted_iota(jnp.int32, logits.shape, 1).astype(F32)
    none = float(logits.shape[1])
    l0 = jnp.where(lane < N_EXPERTS, logits, -jnp.inf)
    m1 = jnp.max(l0, axis=-1, keepdims=True)
    i1 = jnp.min(jnp.where(l0 == m1, lane, none), axis=-1, keepdims=True)
    l1 = jnp.where(lane == i1, -jnp.inf, l0)
    m2 = jnp.max(l1, axis=-1, keepdims=True)
    i2 = jnp.min(jnp.where(l1 == m2, lane, none), axis=-1, keepdims=True)
    e2 = jnp.exp(m2 - m1)
    den = 1.0 + e2
    idx = jnp.concatenate([i1, i2], axis=1).astype(jnp.int32)
    return idx, jnp.concatenate([1.0 / den, e2 / den], axis=1)


N_ATT_IN = 11
N_HGRN_IN = 6
N_MERGE_IN = 6


def _mixer_kernel(*refs, layer, moe):
    refs = list(refs)
    att_in = refs[:N_ATT_IN]
    hg_in = refs[N_ATT_IN:N_ATT_IN + N_HGRN_IN]
    ga_ref, gb_ref, h_ref, wa_ref, wb_ref, wo_ref = refs[N_ATT_IN + N_HGRN_IN:N_ATT_IN + N_HGRN_IN + N_MERGE_IN]
    rest = refs[N_ATT_IN + N_HGRN_IN + N_MERGE_IN:]
    if moe:
        (gn_ref, rwh_ref, rwl_ref, w1_ref, w3_ref, ho_ref, u_ref, idx_ref, gate_ref, w1o_ref, w3o_ref,
         st_ref) = rest
        w1o_ref[...] = w1_ref[...].astype(BF16)
        w3o_ref[...] = w3_ref[...].astype(BF16)
    else:
        ho_ref, st_ref = rest

    @pl.when(pl.program_id(1) == 0)
    def _():
        st_ref[...] = jnp.zeros_like(st_ref)

    sink_ref, q_ref, kc_ref, kp_ref, kcs_ref, kps_ref, vc_ref, vp_ref, vcs_ref, vps_ref, bias_ref = att_in
    hq_ref, hf_ref, hi_ref, hgt_ref, lbl_ref, ng_ref = hg_in
    for sub in range(MIXER_BLOCKS):
        rows = pl.ds(sub * ATT_BLOCK, ATT_BLOCK)
        if sub == 0:
            prev = (kp_ref, kps_ref, vp_ref, vps_ref)
            first = pl.program_id(1) == 0
        else:
            before = pl.ds((sub - 1) * ATT_BLOCK, ATT_BLOCK)
            prev = tuple(r.at[before] for r in (kc_ref, kcs_ref, vc_ref, vcs_ref))
            first = False
        att = _attn_block(sink_ref, q_ref.at[rows], kc_ref.at[rows], prev[0], kcs_ref.at[rows], prev[1],
                          vc_ref.at[rows], prev[2], vcs_ref.at[rows], prev[3], bias_ref, first)
        hgo = _hgrn_tile(hq_ref.at[rows], hf_ref.at[rows], hi_ref.at[rows], hgt_ref.at[rows], lbl_ref, ng_ref,
                         st_ref, layer)
        ya = _dot(att.astype(BF16), wa_ref[...])
        yb = _dot(hgo.astype(BF16), wb_ref[...])
        merged = _sigmoid(ga_ref[rows, :]).astype(F32) * ya + _sigmoid(gb_ref[rows, :]).astype(F32) * yb
        hn = h_ref[rows, :] + _dot(merged.astype(BF16), wo_ref[...])
        ho_ref[rows, :] = hn
        if moe:
            u = _rms(hn, gn_ref[...])
            bits = pltpu.bitcast(u.astype(BF16).astype(F32), jnp.uint32)
            u_ref[rows, :] = (bits[:, :D_MODEL // 2] >> 16) | (bits[:, D_MODEL // 2:] & jnp.uint32(0xFFFF0000))
            idx_ref[rows, :], gate_ref[rows, :] = _route(u, rwh_ref, rwl_ref)


def _mixer(proj, h, sinks, band_bias, lb_logits, norm_g, wa, wb, wo, layer, moe_in=None):
    aq, ak, av, hq, hf, hi, hg, ga, gb, ak_sw, av_sw = proj
    moe = moe_in is not None
    nb = SEQ // MIXER_TILE
    step = lambda b, n: b * nb + n
    cur = lambda b, n: (step(b, n), 0)
    prev = lambda b, n: (jnp.maximum(step(b, n) * MIXER_BLOCKS - 1, 0), 0)
    blk = lambda width: pl.BlockSpec((MIXER_TILE, width), cur)
    kv_cur, kv_prev = blk(ATT_KV_W), pl.BlockSpec((ATT_BLOCK, ATT_KV_W), prev)
    in_specs = [
        pl.BlockSpec(memory_space=pltpu.SMEM), blk(ATT_Q_W),
        kv_cur, kv_prev, kv_cur, kv_prev, kv_cur, kv_prev, kv_cur, kv_prev,
        _resident((ATT_Q_HEADS, ATT_BLOCK, 2 * ATT_BLOCK)),
        blk(HG_K_W), blk(HG_K_W), blk(HG_V_W), blk(HG_V_W), _resident((DEPTH, HG_K_W)), _resident((1, HG_DV)),
        blk(D_MODEL), blk(D_MODEL), blk(D_MODEL),
        _resident((ATT_Q_W, D_MODEL)), _resident((HG_V_W, D_MODEL)), _resident((D_MODEL, D_MODEL)),
    ]
    args = [sinks, aq, ak, ak, ak_sw, ak_sw, av, av, av_sw, av_sw, band_bias,
            hq, hf, hi, hg, lb_logits, norm_g, ga, gb, h, wa, wb, wo]
    assert len(in_specs) == N_ATT_IN + N_HGRN_IN + N_MERGE_IN
    out_specs = [blk(D_MODEL)]
    out_shape = [jax.ShapeDtypeStruct((TOKENS, D_MODEL), F32)]
    if moe:
        gn, rw_hi, rw_lo, w1_all, w3_all, first_block = moe_in
        in_specs += [_resident((1, D_MODEL)), _resident((D_MODEL, ROUTER_LANES)),
                     _resident((D_MODEL, ROUTER_LANES))]
        args += [gn, rw_hi, rw_lo]
        out_specs += [blk(D_MODEL // 2), blk(TOP_K), blk(TOP_K)]
        out_shape += [jax.ShapeDtypeStruct((TOKENS, D_MODEL // 2), jnp.uint32),
                      jax.ShapeDtypeStruct((TOKENS, TOP_K), jnp.int32),
                      jax.ShapeDtypeStruct((TOKENS, TOP_K), F32)]
        for w_all in (w1_all, w3_all):
            r_in, r_out, r_shape = _cast_rider(w_all, first_block, W13_ROWS, MIXER_STEPS, step)
            in_specs.append(r_in)
            args.append(w_all)
            out_specs.append(r_out)
            out_shape.append(r_shape)
    return pl.pallas_call(
        functools.partial(_mixer_kernel, layer=layer, moe=moe),
        grid=(BATCH, nb),
        in_specs=in_specs,
        out_specs=out_specs,
        out_shape=out_shape,
        scratch_shapes=[pltpu.VMEM((HG_HEADS, HG_DV, HG_DK), F32)],
        compiler_params=_cparams(("arbitrary", "arbitrary")),
        name="mixer_moe" if moe else "mixer",
    )(*args)


def _swiglu_part(x, w1, w3, w2):
    a = _dot(x, w1)
    b = _dot(x, w3)
    return _dot((a * _sigmoid(a) * b).astype(BF16), w2)


def _dense_ffn_kernel(h_ref, g_ref, w1_ref, w3_ref, w2_ref, o_ref, x_scr, acc_scr):
    f = pl.program_id(1)

    @pl.when(f == 0)
    def _():
        x = _rms(h_ref[...], g_ref[...]).astype(BF16)
        x_scr[...] = x
        acc_scr[...] = _swiglu_part(x, w1_ref[...], w3_ref[...], w2_ref[...])

    @pl.when(f == 1)
    def _():
        o_ref[...] = h_ref[...] + (acc_scr[...] + _swiglu_part(x_scr[...], w1_ref[...], w3_ref[...], w2_ref[...]))


def _dense_ffn(h, g, w1, w3, w2):
    return pl.pallas_call(
        _dense_ffn_kernel,
        grid=(TOKENS // ROW_TILE, DENSE_NF),
        in_specs=[
            pl.BlockSpec((ROW_TILE, D_MODEL), lambda i, f: (i, 0)),
            _resident((1, D_MODEL)),
            pl.BlockSpec((D_MODEL, DENSE_F_TILE), lambda i, f: (0, f)),
            pl.BlockSpec((D_MODEL, DENSE_F_TILE), lambda i, f: (0, f)),
            pl.BlockSpec((DENSE_F_TILE, D_MODEL), lambda i, f: (f, 0)),
        ],
        out_specs=pl.BlockSpec((ROW_TILE, D_MODEL), lambda i, f: (i, 0)),
        out_shape=jax.ShapeDtypeStruct((TOKENS, D_MODEL), F32),
        scratch_shapes=[pltpu.VMEM((ROW_TILE, D_MODEL), BF16), pltpu.VMEM((ROW_TILE, D_MODEL), F32)],
        compiler_params=_cparams(("arbitrary", "arbitrary")),
        name="dense_ffn",
    )(h, g, w1, w3, w2)


def _dispatch_kernel(pos_ref, u_ref, w2_ref, xs_in_ref, xs_ref, w2o_ref, sem):
    del xs_in_ref
    tok = DISPATCH_TILE // DISPATCH_PARTS
    r2 = W2_ROWS // DISPATCH_PARTS
    for part in range(DISPATCH_PARTS):
        s2 = slice(part * r2, (part + 1) * r2)
        w2o_ref[s2, :] = w2_ref[s2, :].astype(BF16)
        for r in range(part * tok, (part + 1) * tok):
            for k in range(TOP_K):
                d = pos_ref[0, 0, TOP_K * r + k]
                pltpu.make_async_copy(u_ref.at[pl.ds(r, 1)], xs_ref.at[pl.ds(d, 1)], sem).start()

    for k in range(TOP_K):
        pltpu.make_async_copy(u_ref, xs_ref.at[pl.ds(0, DISPATCH_TILE)], sem).wait()


def _dispatch(u_packed, pos, w2_all, moe_layer):
    n = DISPATCH_STEPS
    xs0 = jnp.zeros((MOE_ROWS, D_MODEL // 2), jnp.uint32)
    w2_in, w2_out, w2_shape = _cast_rider(w2_all, moe_layer * n, W2_ROWS, n, lambda i: i)
    return pl.pallas_call(
        _dispatch_kernel,
        grid=(n,),
        in_specs=[
            pl.BlockSpec((1, 1, TOP_K * DISPATCH_TILE), lambda i: (i, 0, 0), memory_space=pltpu.SMEM),
            pl.BlockSpec((DISPATCH_TILE, D_MODEL // 2), lambda i: (i, 0)),
            w2_in,
            pl.BlockSpec(memory_space=pl.ANY),
        ],
        out_specs=[pl.BlockSpec(memory_space=pl.ANY), w2_out],
        out_shape=[jax.ShapeDtypeStruct((MOE_ROWS, D_MODEL // 2), jnp.uint32), w2_shape],
        scratch_shapes=[pltpu.SemaphoreType.DMA(())],
        input_output_aliases={3: 0},
        compiler_params=_cparams(("arbitrary",)),
        name="moe_dispatch",
    )(pos.reshape(n, 1, TOP_K * DISPATCH_TILE), u_packed, w2_all, xs0)


def _moe_kernel(te_ref, nu_ref, x_ref, w1_ref, w3_ref, w2_ref, y_ref, xb, acc):
    i = pl.program_id(0)
    f = pl.program_id(1)
    valid = i < nu_ref[0]

    @pl.when((f == 0) & valid)
    def _():
        w = x_ref[...]
        x = jnp.concatenate([pltpu.bitcast(w << 16, F32).astype(BF16),
                             pltpu.bitcast(w & jnp.uint32(0xFFFF0000), F32).astype(BF16)], axis=1)
        xb[...] = x
        acc[...] = _swiglu_part(x, w1_ref[0], w3_ref[0], w2_ref[0])

    @pl.when((f == 1) & valid)
    def _():
        y_ref[...] = acc[...] + _swiglu_part(xb[...], w1_ref[0], w3_ref[0], w2_ref[0])

    @pl.when((f == 1) & jnp.logical_not(valid))
    def _():
        y_ref[...] = jnp.zeros_like(y_ref)


def _moe_experts(xs, tile_expert, n_used, w1, w3, w2):
    row_blk = lambda i, f, te, nu: (jnp.minimum(i, nu[0] - 1), 0)
    f_blk = lambda i, f, nu: jnp.where(i < nu[0], f, MOE_NF - 1)
    grid_spec = pltpu.PrefetchScalarGridSpec(
        num_scalar_prefetch=2,
        grid=(MOE_TILES, MOE_NF),
        in_specs=[
            pl.BlockSpec((MOE_ROW_TILE, D_MODEL // 2), row_blk),
            pl.BlockSpec((1, D_MODEL, MOE_F_TILE), lambda i, f, te, nu: (te[i], 0, f_blk(i, f, nu))),
            pl.BlockSpec((1, D_MODEL, MOE_F_TILE), lambda i, f, te, nu: (te[i], 0, f_blk(i, f, nu))),
            pl.BlockSpec((1, MOE_F_TILE, D_MODEL), lambda i, f, te, nu: (te[i], f_blk(i, f, nu), 0)),
        ],
        out_specs=pl.BlockSpec((MOE_ROW_TILE, D_MODEL), lambda i, f, te, nu: (i, 0)),
        scratch_shapes=[
            pltpu.VMEM((MOE_ROW_TILE, D_MODEL), BF16),
            pltpu.VMEM((MOE_ROW_TILE, D_MODEL), F32),
        ],
    )
    return pl.pallas_call(
        _moe_kernel,
        grid_spec=grid_spec,
        out_shape=jax.ShapeDtypeStruct((MOE_ROWS, D_MODEL), F32),
        compiler_params=_cparams(("arbitrary", "arbitrary")),
        name="moe_experts",
    )(tile_expert, n_used, xs, w1, w3, w2)


def _moe_plan(idx):
    e_flat = idx.reshape(-1)
    onehot = (e_flat[:, None] == jnp.arange(N_EXPERTS, dtype=jnp.int32)[None, :]).astype(jnp.int32)
    csum = jnp.cumsum(onehot, axis=0)
    counts = csum[-1]
    rank = jnp.sum((csum - onehot) * onehot, axis=1)
    tiles_per = (counts + MOE_ROW_TILE - 1) // MOE_ROW_TILE
    tile_end = jnp.cumsum(tiles_per)
    tile_start = tile_end - tiles_per
    pos = jnp.sum(onehot * tile_start[None, :], axis=1) * MOE_ROW_TILE + rank
    tile_id = jnp.arange(MOE_TILES, dtype=jnp.int32)
    te = jnp.sum((tile_id[:, None] >= tile_end[None, :]).astype(jnp.int32), axis=1)
    n_used = tile_end[-1]
    te = jnp.minimum(te, N_EXPERTS - 1)
    te = jnp.where(tile_id < n_used, te, te[jnp.maximum(n_used - 1, 0)])
    return pos.astype(jnp.int32), te.astype(jnp.int32), n_used.reshape(1).astype(jnp.int32)


def _ple_tile(h, p, g_ref, wg_ref, wp_ref, fn_ref):
    u = _rms(h, g_ref[...]).astype(BF16)
    emb_gate = _sigmoid(_dot(u, wg_ref[...]))
    h = h + emb_gate * _dot(p.astype(BF16), wp_ref[...])
    if fn_ref is not None:
        h = _rms(h, fn_ref[...])
    return h


def _ple_kernel(h_ref, p_ref, g_ref, wg_ref, wp_ref, *rest, final):
    fn_ref, o_ref = rest if final else (None,) + rest
    o_ref[...] = _ple_tile(h_ref[...], p_ref[...], g_ref, wg_ref, wp_ref, fn_ref)


def _ple_moe_kernel(pos_ref, posn_ref, gate_ref, y_hbm, h_ref, p_ref, g_ref, wg_ref, wp_ref, *rest, final):
    if final:
        fn_ref, o_ref, yg, sems = rest
    else:
        fn_ref, (o_ref, yg, sems) = None, rest
    j = pl.program_id(0)

    def start_row(idx_ref, slot, r):
        for k in range(TOP_K):
            s = idx_ref[0, slot, TOP_K * r + k]
            pltpu.make_async_copy(y_hbm.at[pl.ds(s, 1)], yg.at[slot, k, pl.ds(r, 1)], sems.at[slot]).start()

    def wait_rows(slot):
        for k in range(TOP_K):
            pltpu.make_async_copy(y_hbm.at[pl.ds(0, ROW_TILE)], yg.at[slot, k], sems.at[slot]).wait()

    @pl.when(j == 0)
    def _():
        for slot in range(2):
            def body(r, carry, slot=slot):
                start_row(pos_ref, slot, r)
                return carry
            lax.fori_loop(0, ROW_TILE, body, 0, unroll=8)

    for slot in range(2):
        rows = pl.ds(slot * ROW_TILE, ROW_TILE)
        wait_rows(slot)
        h = h_ref[rows, :]
        gate = gate_ref[rows, :]
        for k in range(TOP_K):
            h = h + gate[:, k:k + 1] * yg[slot, k]
        o_ref[rows, :] = _ple_tile(h, p_ref[rows, :], g_ref, wg_ref, wp_ref, fn_ref)
        for r in range(ROW_TILE):
            start_row(posn_ref, slot, r)

    @pl.when(j == pl.num_programs(0) - 1)
    def _():
        for slot in range(2):
            wait_rows(slot)


def _ple(h, moe_in, p, g, wg, wp, final_g):
    moe = moe_in is not None
    final = final_g is not None
    tile = 2 * ROW_TILE if moe else ROW_TILE
    n = TOKENS // tile
    row = lambda width: pl.BlockSpec((tile, width), lambda i: (i, 0))
    in_specs, args, scratch = [], [], []
    if moe:
        pos, gate, y = moe_in
        pos3 = pos.reshape(n, 2, TOP_K * ROW_TILE)
        smem = lambda imap: pl.BlockSpec((1, 2, TOP_K * ROW_TILE), imap, memory_space=pltpu.SMEM)
        in_specs += [smem(lambda i: (i, 0, 0)), smem(lambda i: (jnp.minimum(i + 1, n - 1), 0, 0)),
                     row(TOP_K), pl.BlockSpec(memory_space=pl.ANY)]
        args += [pos3, pos3, gate, y]
        scratch = [pltpu.VMEM((2, TOP_K, ROW_TILE, D_MODEL), F32), pltpu.SemaphoreType.DMA((2,))]
    in_specs += [row(D_MODEL), row(PLE_DIM), _resident((1, D_MODEL)), _resident((D_MODEL, D_MODEL)),
                 _resident((PLE_DIM, D_MODEL))]
    args += [h, p, g, wg, wp]
    if final:
        in_specs.append(_resident((1, D_MODEL)))
        args.append(final_g)
    return pl.pallas_call(
        functools.partial(_ple_moe_kernel if moe else _ple_kernel, final=final),
        grid=(n,),
        in_specs=in_specs,
        out_specs=row(D_MODEL),
        out_shape=jax.ShapeDtypeStruct((TOKENS, D_MODEL), F32),
        scratch_shapes=scratch,
        compiler_params=_cparams(("arbitrary",)),
        name="ple_moe" if moe else "ple",
    )(*args)


def _t5_bucket(dist):
    max_exact = REL_BUCKETS // 2
    d = jnp.maximum(dist, 0)
    large = max_exact + (jnp.log(jnp.maximum(d, 1).astype(jnp.float32) / max_exact)
                         / math.log(REL_MAX_DIST / max_exact)
                         * (REL_BUCKETS - max_exact)).astype(jnp.int32)
    large = jnp.minimum(large, REL_BUCKETS - 1)
    return jnp.where(d < max_exact, d, large)


def kernel(x, p, w_in, sinks, rel_bias, lb_logits, hgrn_norm, w_branch_a, w_branch_b, w_out,
           norm_mix, norm_ffn, norm_ple, dense_w1, dense_w3, dense_w2, router_w, moe_w1,
           moe_w3, moe_w2, ple_proj, ple_gate, final_norm):
    qi = jnp.arange(ATT_BLOCK)[:, None]
    kj = jnp.arange(2 * ATT_BLOCK)[None, :]
    bucket = _t5_bucket(qi + ATT_BLOCK - kj)[None]
    band_bias = jnp.zeros((ATT_Q_HEADS, ATT_BLOCK, 2 * ATT_BLOCK), F32)
    for b in range(REL_BUCKETS):
        band_bias = jnp.where(bucket == b, rel_bias[b].astype(F32)[:, None, None], band_bias)

    bf = lambda w: w.astype(BF16)
    h = x.reshape(TOKENS, D_MODEL)
    pt = p.reshape(DEPTH, TOKENS, PLE_DIM)
    vec = lambda g: g.reshape(1, -1)
    n_moe = moe_w1.shape[0]
    w1_all = moe_w1.reshape(n_moe * N_EXPERTS * D_MODEL, FFN_EXPERT)
    w3_all = moe_w3.reshape(n_moe * N_EXPERTS * D_MODEL, FFN_EXPERT)
    w2_all = moe_w2.reshape(n_moe * N_EXPERTS * FFN_EXPERT, D_MODEL)

    def stacked(w):
        return w.reshape(w.shape[0] * w.shape[1], w.shape[2])

    def layer_rider(w, index, rows):
        blocks = w.shape[1] // rows
        return stacked(w), index * blocks, rows, blocks

    w_in_next = bf(w_in[0])
    for l in range(DEPTH):
        moe = l % 2 == 1
        riders = [layer_rider(w_branch_a, l, 16), layer_rider(w_branch_b, l, 16),
                  layer_rider(w_out, l, 32), layer_rider(ple_gate, l, 32)]
        if not moe:
            riders += [layer_rider(dense_w1, l // 2, 32), layer_rider(dense_w3, l // 2, 32),
                       layer_rider(dense_w2, l // 2, 176)]
        if l + 1 < DEPTH:
            riders.append(layer_rider(w_in, l + 1, 32))
        proj, cast = _in_proj(h, vec(norm_mix[l]), w_in_next, riders)
        wa, wb, wo, wpg = cast[:4]
        if l + 1 < DEPTH:
            w_in_next = cast[-1]
        final_g = vec(final_norm) if l == DEPTH - 1 else None
        if not moe:
            (h,) = _mixer(proj, h, sinks[l], band_bias, lb_logits, vec(hgrn_norm[l]), wa, wb, wo, l)
            h = _dense_ffn(h, vec(norm_ffn[l]), cast[4], cast[5], cast[6])
            y = None
        else:
            rw = jnp.pad(router_w[l // 2], ((0, 0), (0, ROUTER_LANES - N_EXPERTS)))
            rw_hi = bf(rw)
            rw_lo = bf(rw - rw_hi.astype(F32))
            moe_in = (vec(norm_ffn[l]), rw_hi, rw_lo, w1_all, w3_all, (l // 2) * MIXER_STEPS)
            h, u, idx, gate, w1b, w3b = _mixer(proj, h, sinks[l], band_bias, lb_logits, vec(hgrn_norm[l]),
                                               wa, wb, wo, l, moe_in)
            pos, tile_expert, n_used = _moe_plan(idx)
            xs, w2b = _dispatch(u, pos, w2_all, l // 2)
            y = _moe_experts(xs, tile_expert, n_used,
                             w1b.reshape(N_EXPERTS, D_MODEL, FFN_EXPERT),
                             w3b.reshape(N_EXPERTS, D_MODEL, FFN_EXPERT),
                             w2b.reshape(N_EXPERTS, FFN_EXPERT, D_MODEL))
            y = (pos, gate, y)
        h = _ple(h, y, pt[l], vec(norm_ple[l]), wpg, bf(ple_proj[l]), final_g)
    return h.reshape(BATCH, SEQ, D_MODEL)
```

```python
import functools
import math

import jax
import jax.numpy as jnp
from jax import lax
from jax.experimental import pallas as pl
from jax.experimental.pallas import tpu as pltpu

F32 = jnp.float32
BF16 = jnp.bfloat16

D_MODEL = 1024
BATCH = 4
SEQ = 4096
TOKENS = BATCH * SEQ
DEPTH = 4
ATT_Q_HEADS = 8
ATT_KV_HEADS = 2
ATT_HEAD_DIM = 64
ATT_GROUP = ATT_Q_HEADS // ATT_KV_HEADS
WINDOW = 128
ATT_BLOCK = 128
REL_BUCKETS = 32
REL_MAX_DIST = 128
HG_HEADS = 4
HG_DK = 128
HG_DV = 128
ATT_Q_W = ATT_Q_HEADS * ATT_HEAD_DIM
ATT_KV_W = ATT_KV_HEADS * ATT_HEAD_DIM
HG_K_W = HG_HEADS * HG_DK
HG_V_W = HG_HEADS * HG_DV
IN_SPLITS = (ATT_Q_W, ATT_KV_W, ATT_KV_W, HG_K_W, HG_K_W, HG_V_W, HG_V_W, D_MODEL, D_MODEL)
IN_WIDTH = sum(IN_SPLITS)
FFN_DENSE = 2816
N_EXPERTS = 8
TOP_K = 2
FFN_EXPERT = 3584
PLE_DIM = 256
EPS = 1e-6

V7X_VMEM_LIMIT_BYTES = 52 * 1024 * 1024
ROW_TILE = 512
HG_TILE = 128
DENSE_NF = 2
DENSE_F_TILE = FFN_DENSE // DENSE_NF
MOE_ROW_TILE = 512
DISPATCH_TILE = 1024
DISPATCH_STEPS = TOKENS // DISPATCH_TILE
DISPATCH_PARTS = 16
assert HG_TILE == ATT_BLOCK
MIXER_BLOCKS = 2
MIXER_TILE = MIXER_BLOCKS * ATT_BLOCK
MIXER_STEPS = TOKENS // MIXER_TILE
W13_ROWS = N_EXPERTS * D_MODEL // MIXER_STEPS
W2_ROWS = N_EXPERTS * FFN_EXPERT // DISPATCH_STEPS
MOE_NF = 2
MOE_F_TILE = FFN_EXPERT // MOE_NF
MOE_TILES = (TOKENS * TOP_K) // MOE_ROW_TILE + N_EXPERTS - 1
MOE_ROWS = MOE_TILES * MOE_ROW_TILE
COL_CHUNK = 512
ROUTER_LANES = 128


def _cparams(sem):
    return pltpu.CompilerParams(dimension_semantics=sem, vmem_limit_bytes=V7X_VMEM_LIMIT_BYTES)


def _rms(x, g):
    return x * lax.rsqrt(jnp.mean(x * x, axis=-1, keepdims=True) + EPS) * g


def _sigmoid(x):
    return 1.0 / (1.0 + jnp.exp(-x))


def _dot(a, b):
    return jnp.dot(a, b, preferred_element_type=F32)


def _dot_nt(a, b):
    return lax.dot_general(a, b, (((1,), (1,)), ((), ())), preferred_element_type=F32)


def _resident(shape):
    nd = len(shape)
    return pl.BlockSpec(shape, lambda *_: (0,) * nd)


def _cast_rider(w2d, first_block, rows, n_steps, step_of):
    cols = w2d.shape[1]
    in_spec = pl.BlockSpec((rows, cols), lambda *ids: (first_block + step_of(*ids), 0))
    out_spec = pl.BlockSpec((rows, cols), lambda *ids: (step_of(*ids), 0))
    return in_spec, out_spec, jax.ShapeDtypeStruct((rows * n_steps, cols), BF16)


IN_DTYPES = (BF16, BF16, BF16, BF16, F32, BF16, BF16, BF16, BF16)
ATT_SCALE = ATT_HEAD_DIM ** -0.5


def _in_proj_kernel(*refs, n_riders):
    h_ref, g_ref, w_ref = refs[:3]
    rider_in = refs[3:3 + n_riders]
    out_refs = refs[3 + n_riders:len(refs) - n_riders]
    rider_out = refs[len(refs) - n_riders:]
    for wi_ref, wo_ref in zip(rider_in, rider_out):
        wo_ref[...] = wi_ref[...].astype(BF16)
    xn = _rms(h_ref[...], g_ref[...]).astype(BF16)
    main_refs, (ak_sw_ref, av_sw_ref) = out_refs[:len(IN_SPLITS)], out_refs[len(IN_SPLITS):]
    off = 0
    for idx, (o_ref, width) in enumerate(zip(main_refs, IN_SPLITS)):
        for c in range(0, width, COL_CHUNK):
            cw = min(COL_CHUNK, width - c)
            z = _dot(xn, w_ref[:, off + c:off + c + cw])
            if idx == 0:
                z = z * ATT_SCALE
            o_ref[:, c:c + cw] = z.astype(o_ref.dtype)
            if idx in (1, 2):
                sw_ref = ak_sw_ref if idx == 1 else av_sw_ref
                sw_ref[...] = jnp.concatenate([z[:, ATT_HEAD_DIM:], z[:, :ATT_HEAD_DIM]], axis=1).astype(sw_ref.dtype)
        off += width


def _in_proj(h, g, w, riders):
    n_steps = TOKENS // ROW_TILE
    row = lambda width: pl.BlockSpec((ROW_TILE, width), lambda i: (i, 0))
    widths = IN_SPLITS + (ATT_KV_W, ATT_KV_W)
    dtypes = IN_DTYPES + (BF16, BF16)
    in_specs = [row(D_MODEL), _resident((1, D_MODEL)), _resident((D_MODEL, IN_WIDTH))]
    args = [h, g, w]
    out_specs = [row(wd) for wd in widths]
    out_shape = [jax.ShapeDtypeStruct((TOKENS, wd), dt) for wd, dt in zip(widths, dtypes)]
    for w2d, first_block, rows, n_blocks in riders:
        assert n_blocks <= n_steps
        r_in, r_out, r_shape = _cast_rider(w2d, first_block, rows, n_blocks,
                                           lambda i, nb=n_blocks: jnp.minimum(i, nb - 1))
        in_specs.append(r_in)
        args.append(w2d)
        out_specs.append(r_out)
        out_shape.append(r_shape)
    outs = pl.pallas_call(
        functools.partial(_in_proj_kernel, n_riders=len(riders)),
        grid=(n_steps,),
        in_specs=in_specs,
        out_specs=out_specs,
        out_shape=out_shape,
        compiler_params=_cparams(("arbitrary",)),
        name="in_proj",
    )(*args)
    return outs[:len(widths)], outs[len(widths):]


def _block_diag(a, a_sw, g):
    lane = lax.broadcasted_iota(jnp.int32, a.shape, 1)
    low = lane < ATT_HEAD_DIM
    own, other = (a, a_sw) if g == 0 else (a_sw, a)
    zero = jnp.zeros_like(a)
    first = jnp.where(low, own, zero)
    second = jnp.where(low, zero, other)
    cat = jnp.concatenate
    return cat([cat([first, zero], 1), cat([second, zero], 1),
                cat([zero, first], 1), cat([zero, second], 1)], 0)


def _attn_block(sink_ref, q_ref, kc_ref, kp_ref, kcs_ref, kps_ref, vc_ref, vp_ref, vcs_ref, vps_ref,
                bias_ref, first):
    band = 2 * ATT_BLOCK
    r = lax.broadcasted_iota(jnp.int32, (ATT_BLOCK, band), 0)
    j = lax.broadcasted_iota(jnp.int32, (ATT_BLOCK, band), 1)
    dist = r + ATT_BLOCK - j
    valid = (dist >= 0) & (dist < WINDOW)
    if first is not False:
        valid = valid & ((j >= ATT_BLOCK) | jnp.logical_not(first))
    cat = jnp.concatenate
    k, ks = cat([kp_ref[...], kc_ref[...]], 0), cat([kps_ref[...], kcs_ref[...]], 0)
    v, vs = cat([vp_ref[...], vc_ref[...]], 0), cat([vps_ref[...], vcs_ref[...]], 0)
    gw = ATT_GROUP * ATT_HEAD_DIM
    lane = lax.broadcasted_iota(jnp.int32, (ATT_BLOCK, gw), 1)
    outs = []
    for g in range(ATT_KV_HEADS):
        s_all = _dot_nt(q_ref[:, g * gw:(g + 1) * gw], _block_diag(k, ks, g))
        ps, dens = [], []
        for hh in range(ATT_GROUP):
            h = g * ATT_GROUP + hh
            s = s_all[:, hh * band:(hh + 1) * band] + bias_ref[h]
            s = jnp.where(valid, s, -1e30)
            sink = sink_ref[h]
            m = jnp.maximum(jnp.max(s, axis=-1, keepdims=True), sink)
            p = jnp.exp(s - m)
            dens.append(jnp.sum(p, axis=-1, keepdims=True) + jnp.exp(sink - m))
            ps.append(p.astype(BF16))
        o = _dot(cat(ps, 1), _block_diag(v, vs, g))
        den = dens[ATT_GROUP - 1]
        for hh in reversed(range(ATT_GROUP - 1)):
            den = jnp.where(lane < (hh + 1) * ATT_HEAD_DIM, dens[hh], den)
        outs.append(o / den)
    return cat(outs, 1)


SUBLANES = 8


def _cumsum_rows(x, t):
    within = t & (SUBLANES - 1)
    shift = 1
    while shift < SUBLANES:
        x = x + jnp.where(within >= shift, pltpu.roll(x, shift, 0), 0.0)
        shift *= 2
    tiles = []
    carry = jnp.zeros_like(x[0:1, :])
    for r in range(0, x.shape[0], SUBLANES):
        tile = x[r:r + SUBLANES, :] + carry
        tiles.append(tile)
        carry = tile[SUBLANES - 1:SUBLANES, :]
    return jnp.concatenate(tiles, 0)


def _hgrn_tile(q_ref, f_ref, v_ref, g_ref, lbl_ref, ng_ref, st_ref, layer):
    rows = [lbl_ref[i:i + 1, :] for i in range(DEPTH)]
    mx = functools.reduce(jnp.maximum, rows)
    ex = [jnp.exp(rw - mx) for rw in rows]
    tot = functools.reduce(lambda a, b: a + b, ex)
    lower = jnp.zeros_like(mx)
    for i in range(1, layer + 1):
        lower = lower + ex[i] / tot

    c = HG_TILE
    t = lax.broadcasted_iota(jnp.int32, (c, HG_DK), 0)
    ts = lax.broadcasted_iota(jnp.int32, (c, c), 0)
    ss = lax.broadcasted_iota(jnp.int32, (c, c), 1)
    same64 = (ts >> 6) == (ss >> 6)
    diag32 = ((ts >> 5) == (ss >> 5)) & (ss <= ts)
    ng = ng_ref[...]

    outs = []
    for h in range(HG_HEADS):
        sl = slice(h * HG_DK, (h + 1) * HG_DK)
        lb = lower[:, sl]
        f = lb + (1.0 - lb) * _sigmoid(f_ref[:, sl])
        k = 1.0 - f
        a = _cumsum_rows(jnp.log(f), t)
        a_last = a[c - 1:c, :]
        q = q_ref[:, sl].astype(F32)
        v = v_ref[:, sl]
        st = st_ref[h]

        o = _dot_nt((q * jnp.exp(a)).astype(BF16), st.astype(BF16))
        ke = (k * jnp.exp(a_last - a)).astype(BF16)
        vt = v.astype(F32).T.astype(BF16)
        st_ref[h] = st * jnp.exp(a_last) + _dot(vt, ke)

        m1 = a[63:64, :]
        q1 = jnp.where(t >= 64, q * jnp.exp(a - m1), 0.0)
        k1 = jnp.where(t < 64, k * jnp.exp(m1 - a), 0.0)
        p = _dot_nt(q1.astype(BF16), k1.astype(BF16))

        m2 = jnp.where(t < 64, a[31:32, :], a[95:96, :])
        second = (t & 63) >= 32
        q2 = jnp.where(second, q * jnp.exp(a - m2), 0.0)
        k2 = jnp.where(second, 0.0, k * jnp.exp(m2 - a))
        p = p + jnp.where(same64, _dot_nt(q2.astype(BF16), k2.astype(BF16)), 0.0)

        m3 = jnp.where(t < 64,
                       jnp.where(t < 32, a[15:16, :], a[47:48, :]),
                       jnp.where(t < 96, a[79:80, :], a[111:112, :]))
        q3 = q * jnp.exp(a - m3)
        k3 = k * jnp.exp(m3 - a)
        p = p + jnp.where(diag32, _dot_nt(q3.astype(BF16), k3.astype(BF16)), 0.0)

        o = o + _dot(p.astype(BF16), v)
        o = o * lax.rsqrt(jnp.mean(o * o, axis=-1, keepdims=True) + EPS) * ng
        gate = g_ref[:, sl].astype(F32)
        outs.append(o * (gate * _sigmoid(gate)))
    return jnp.concatenate(outs, 1)


def _route(u, rwh_ref, rwl_ref):
    u_hi = u.astype(BF16)
    u_lo = (u - u_hi.astype(F32)).astype(BF16)
    logits = _dot(u_hi, rwh_ref[...]) + (_dot(u_hi, rwl_ref[...]) + _dot(u_lo, rwh_ref[...]))
    lane = lax.broadcasted_iota(jnp.int32, logits.shape, 1).astype(F32)
    none = float(logits.shape[1])
    l0 = jnp.where(lane < N_EXPERTS, logits, -jnp.inf)
    m1 = jnp.max(l0, axis=-1, keepdims=True)
    i1 = jnp.min(jnp.where(l0 == m1, lane, none), axis=-1, keepdims=True)
    l1 = jnp.where(lane == i1, -jnp.inf, l0)
    m2 = jnp.max(l1, axis=-1, keepdims=True)
    i2 = jnp.min(jnp.where(l1 == m2, lane, none), axis=-1, keepdims=True)
    e2 = jnp.exp(m2 - m1)
    den = 1.0 + e2
    idx = jnp.concatenate([i1, i2], axis=1).astype(jnp.int32)
    return idx, jnp.concatenate([1.0 / den, e2 / den], axis=1)


N_ATT_IN = 11
N_HGRN_IN = 6
N_MERGE_IN = 6


def _mixer_kernel(*refs, layer, moe):
    refs = list(refs)
    att_in = refs[:N_ATT_IN]
    hg_in = refs[N_ATT_IN:N_ATT_IN + N_HGRN_IN]
    ga_ref, gb_ref, h_ref, wa_ref, wb_ref, wo_ref = refs[N_ATT_IN + N_HGRN_IN:N_ATT_IN + N_HGRN_IN + N_MERGE_IN]
    rest = refs[N_ATT_IN + N_HGRN_IN + N_MERGE_IN:]
    if moe:
        (gn_ref, rwh_ref, rwl_ref, w1_ref, w3_ref, ho_ref, u_ref, idx_ref, gate_ref, w1o_ref, w3o_ref,
         st_ref) = rest
        w1o_ref[...] = w1_ref[...].astype(BF16)
        w3o_ref[...] = w3_ref[...].astype(BF16)
    else:
        ho_ref, st_ref = rest

    @pl.when(pl.program_id(1) == 0)
    def _():
        st_ref[...] = jnp.zeros_like(st_ref)

    sink_ref, q_ref, kc_ref, kp_ref, kcs_ref, kps_ref, vc_ref, vp_ref, vcs_ref, vps_ref, bias_ref = att_in
    hq_ref, hf_ref, hi_ref, hgt_ref, lbl_ref, ng_ref = hg_in
    for sub in range(MIXER_BLOCKS):
        rows = pl.ds(sub * ATT_BLOCK, ATT_BLOCK)
        if sub == 0:
            prev = (kp_ref, kps_ref, vp_ref, vps_ref)
            first = pl.program_id(1) == 0
        else:
            before = pl.ds((sub - 1) * ATT_BLOCK, ATT_BLOCK)
            prev = tuple(r.at[before] for r in (kc_ref, kcs_ref, vc_ref, vcs_ref))
            first = False
        att = _attn_block(sink_ref, q_ref.at[rows], kc_ref.at[rows], prev[0], kcs_ref.at[rows], prev[1],
                          vc_ref.at[rows], prev[2], vcs_ref.at[rows], prev[3], bias_ref, first)
        hgo = _hgrn_tile(hq_ref.at[rows], hf_ref.at[rows], hi_ref.at[rows], hgt_ref.at[rows], lbl_ref, ng_ref,
                         st_ref, layer)
        ya = _dot(att.astype(BF16), wa_ref[...])
        yb = _dot(hgo.astype(BF16), wb_ref[...])
        merged = _sigmoid(ga_ref[rows, :]).astype(F32) * ya + _sigmoid(gb_ref[rows, :]).astype(F32) * yb
        hn = h_ref[rows, :] + _dot(merged.astype(BF16), wo_ref[...])
        ho_ref[rows, :] = hn
        if moe:
            u = _rms(hn, gn_ref[...])
            bits = pltpu.bitcast(u.astype(BF16).astype(F32), jnp.uint32)
            u_ref[rows, :] = (bits[:, :D_MODEL // 2] >> 16) | (bits[:, D_MODEL // 2:] & jnp.uint32(0xFFFF0000))
            idx_ref[rows, :], gate_ref[rows, :] = _route(u, rwh_ref, rwl_ref)


def _mixer(proj, h, sinks, band_bias, lb_logits, norm_g, wa, wb, wo, layer, moe_in=None):
    aq, ak, av, hq, hf, hi, hg, ga, gb, ak_sw, av_sw = proj
    moe = moe_in is not None
    nb = SEQ // MIXER_TILE
    step = lambda b, n: b * nb + n
    cur = lambda b, n: (step(b, n), 0)
    prev = lambda b, n: (jnp.maximum(step(b, n) * MIXER_BLOCKS - 1, 0), 0)
    blk = lambda width: pl.BlockSpec((MIXER_TILE, width), cur)
    kv_cur, kv_prev = blk(ATT_KV_W), pl.BlockSpec((ATT_BLOCK, ATT_KV_W), prev)
    in_specs = [
        pl.BlockSpec(memory_space=pltpu.SMEM), blk(ATT_Q_W),
        kv_cur, kv_prev, kv_cur, kv_prev, kv_cur, kv_prev, kv_cur, kv_prev,
        _resident((ATT_Q_HEADS, ATT_BLOCK, 2 * ATT_BLOCK)),
        blk(HG_K_W), blk(HG_K_W), blk(HG_V_W), blk(HG_V_W), _resident((DEPTH, HG_K_W)), _resident((1, HG_DV)),
        blk(D_MODEL), blk(D_MODEL), blk(D_MODEL),
        _resident((ATT_Q_W, D_MODEL)), _resident((HG_V_W, D_MODEL)), _resident((D_MODEL, D_MODEL)),
    ]
    args = [sinks, aq, ak, ak, ak_sw, ak_sw, av, av, av_sw, av_sw, band_bias,
            hq, hf, hi, hg, lb_logits, norm_g, ga, gb, h, wa, wb, wo]
    assert len(in_specs) == N_ATT_IN + N_HGRN_IN + N_MERGE_IN
    out_specs = [blk(D_MODEL)]
    out_shape = [jax.ShapeDtypeStruct((TOKENS, D_MODEL), F32)]
    if moe:
        gn, rw_hi, rw_lo, w1_all, w3_all, first_block = moe_in
        in_specs += [_resident((1, D_MODEL)), _resident((D_MODEL, ROUTER_LANES)),
                     _resident((D_MODEL, ROUTER_LANES))]
        args += [gn, rw_hi, rw_lo]
        out_specs += [blk(D_MODEL // 2), blk(TOP_K), blk(TOP_K)]
        out_shape += [jax.ShapeDtypeStruct((TOKENS, D_MODEL // 2), jnp.uint32),
                      jax.ShapeDtypeStruct((TOKENS, TOP_K), jnp.int32),
                      jax.ShapeDtypeStruct((TOKENS, TOP_K), F32)]
        for w_all in (w1_all, w3_all):
            r_in, r_out, r_shape = _cast_rider(w_all, first_block, W13_ROWS, MIXER_STEPS, step)
            in_specs.append(r_in)
            args.append(w_all)
            out_specs.append(r_out)
            out_shape.append(r_shape)
    return pl.pallas_call(
        functools.partial(_mixer_kernel, layer=layer, moe=moe),
        grid=(BATCH, nb),
        in_specs=in_specs,
        out_specs=out_specs,
        out_shape=out_shape,
        scratch_shapes=[pltpu.VMEM((HG_HEADS, HG_DV, HG_DK), F32)],
        compiler_params=_cparams(("arbitrary", "arbitrary")),
        name="mixer_moe" if moe else "mixer",
    )(*args)


def _swiglu_part(x, w1, w3, w2):
    a = _dot(x, w1)
    b = _dot(x, w3)
    return _dot((a * _sigmoid(a) * b).astype(BF16), w2)


def _dense_ffn_kernel(h_ref, g_ref, w1_ref, w3_ref, w2_ref, o_ref, x_scr, acc_scr):
    f = pl.program_id(1)

    @pl.when(f == 0)
    def _():
        x = _rms(h_ref[...], g_ref[...]).astype(BF16)
        x_scr[...] = x
        acc_scr[...] = _swiglu_part(x, w1_ref[...], w3_ref[...], w2_ref[...])

    @pl.when(f == 1)
    def _():
        o_ref[...] = h_ref[...] + (acc_scr[...] + _swiglu_part(x_scr[...], w1_ref[...], w3_ref[...], w2_ref[...]))


def _dense_ffn(h, g, w1, w3, w2):
    return pl.pallas_call(
        _dense_ffn_kernel,
        grid=(TOKENS // ROW_TILE, DENSE_NF),
        in_specs=[
            pl.BlockSpec((ROW_TILE, D_MODEL), lambda i, f: (i, 0)),
            _resident((1, D_MODEL)),
            pl.BlockSpec((D_MODEL, DENSE_F_TILE), lambda i, f: (0, f)),
            pl.BlockSpec((D_MODEL, DENSE_F_TILE), lambda i, f: (0, f)),
            pl.BlockSpec((DENSE_F_TILE, D_MODEL), lambda i, f: (f, 0)),
        ],
        out_specs=pl.BlockSpec((ROW_TILE, D_MODEL), lambda i, f: (i, 0)),
        out_shape=jax.ShapeDtypeStruct((TOKENS, D_MODEL), F32),
        scratch_shapes=[pltpu.VMEM((ROW_TILE, D_MODEL), BF16), pltpu.VMEM((ROW_TILE, D_MODEL), F32)],
        compiler_params=_cparams(("arbitrary", "arbitrary")),
        name="dense_ffn",
    )(h, g, w1, w3, w2)


def _dispatch_kernel(pos_ref, u_ref, w2_ref, xs_in_ref, xs_ref, w2o_ref, sem):
    del xs_in_ref
    tok = DISPATCH_TILE // DISPATCH_PARTS
    r2 = W2_ROWS // DISPATCH_PARTS
    for part in range(DISPATCH_PARTS):
        s2 = slice(part * r2, (part + 1) * r2)
        w2o_ref[s2, :] = w2_ref[s2, :].astype(BF16)
        for r in range(part * tok, (part + 1) * tok):
            for k in range(TOP_K):
                d = pos_ref[0, 0, TOP_K * r + k]
                pltpu.make_async_copy(u_ref.at[pl.ds(r, 1)], xs_ref.at[pl.ds(d, 1)], sem).start(priority=k)

    for k in range(TOP_K):
        pltpu.make_async_copy(u_ref, xs_ref.at[pl.ds(0, DISPATCH_TILE)], sem).wait()


def _dispatch(u_packed, pos, w2_all, moe_layer):
    n = DISPATCH_STEPS
    xs0 = jnp.zeros((MOE_ROWS, D_MODEL // 2), jnp.uint32)
    w2_in, w2_out, w2_shape = _cast_rider(w2_all, moe_layer * n, W2_ROWS, n, lambda i: i)
    return pl.pallas_call(
        _dispatch_kernel,
        grid=(n,),
        in_specs=[
            pl.BlockSpec((1, 1, TOP_K * DISPATCH_TILE), lambda i: (i, 0, 0), memory_space=pltpu.SMEM),
            pl.BlockSpec((DISPATCH_TILE, D_MODEL // 2), lambda i: (i, 0)),
            w2_in,
            pl.BlockSpec(memory_space=pl.ANY),
        ],
        out_specs=[pl.BlockSpec(memory_space=pl.ANY), w2_out],
        out_shape=[jax.ShapeDtypeStruct((MOE_ROWS, D_MODEL // 2), jnp.uint32), w2_shape],
        scratch_shapes=[pltpu.SemaphoreType.DMA(())],
        input_output_aliases={3: 0},
        compiler_params=_cparams(("arbitrary",)),
        name="moe_dispatch",
    )(pos.reshape(n, 1, TOP_K * DISPATCH_TILE), u_packed, w2_all, xs0)


def _moe_kernel(te_ref, nu_ref, x_ref, w1_ref, w3_ref, w2_ref, y_ref, xb, acc):
    i = pl.program_id(0)
    f = pl.program_id(1)
    valid = i < nu_ref[0]

    @pl.when((f == 0) & valid)
    def _():
        w = x_ref[...]
        x = jnp.concatenate([pltpu.bitcast(w << 16, F32).astype(BF16),
                             pltpu.bitcast(w & jnp.uint32(0xFFFF0000), F32).astype(BF16)], axis=1)
        xb[...] = x
        acc[...] = _swiglu_part(x, w1_ref[0], w3_ref[0], w2_ref[0])

    @pl.when((f == 1) & valid)
    def _():
        y_ref[...] = acc[...] + _swiglu_part(xb[...], w1_ref[0], w3_ref[0], w2_ref[0])

    @pl.when((f == 1) & jnp.logical_not(valid))
    def _():
        y_ref[...] = jnp.zeros_like(y_ref)


def _moe_experts(xs, tile_expert, n_used, w1, w3, w2):
    row_blk = lambda i, f, te, nu: (jnp.minimum(i, nu[0] - 1), 0)
    f_blk = lambda i, f, nu: jnp.where(i < nu[0], f, MOE_NF - 1)
    grid_spec = pltpu.PrefetchScalarGridSpec(
        num_scalar_prefetch=2,
        grid=(MOE_TILES, MOE_NF),
        in_specs=[
            pl.BlockSpec((MOE_ROW_TILE, D_MODEL // 2), row_blk),
            pl.BlockSpec((1, D_MODEL, MOE_F_TILE), lambda i, f, te, nu: (te[i], 0, f_blk(i, f, nu))),
            pl.BlockSpec((1, D_MODEL, MOE_F_TILE), lambda i, f, te, nu: (te[i], 0, f_blk(i, f, nu))),
            pl.BlockSpec((1, MOE_F_TILE, D_MODEL), lambda i, f, te, nu: (te[i], f_blk(i, f, nu), 0)),
        ],
        out_specs=pl.BlockSpec((MOE_ROW_TILE, D_MODEL), lambda i, f, te, nu: (i, 0)),
        scratch_shapes=[
            pltpu.VMEM((MOE_ROW_TILE, D_MODEL), BF16),
            pltpu.VMEM((MOE_ROW_TILE, D_MODEL), F32),
        ],
    )
    return pl.pallas_call(
        _moe_kernel,
        grid_spec=grid_spec,
        out_shape=jax.ShapeDtypeStruct((MOE_ROWS, D_MODEL), F32),
        compiler_params=_cparams(("arbitrary", "arbitrary")),
        name="moe_experts",
    )(tile_expert, n_used, xs, w1, w3, w2)


def _moe_plan(idx):
    e_flat = idx.reshape(-1)
    onehot = (e_flat[:, None] == jnp.arange(N_EXPERTS, dtype=jnp.int32)[None, :]).astype(jnp.int32)
    csum = jnp.cumsum(onehot, axis=0)
    counts = csum[-1]
    rank = jnp.sum((csum - onehot) * onehot, axis=1)
    tiles_per = (counts + MOE_ROW_TILE - 1) // MOE_ROW_TILE
    tile_end = jnp.cumsum(tiles_per)
    tile_start = tile_end - tiles_per
    pos = jnp.sum(onehot * tile_start[None, :], axis=1) * MOE_ROW_TILE + rank
    tile_id = jnp.arange(MOE_TILES, dtype=jnp.int32)
    te = jnp.sum((tile_id[:, None] >= tile_end[None, :]).astype(jnp.int32), axis=1)
    n_used = tile_end[-1]
    te = jnp.minimum(te, N_EXPERTS - 1)
    te = jnp.where(tile_id < n_used, te, te[jnp.maximum(n_used - 1, 0)])
    return pos.astype(jnp.int32), te.astype(jnp.int32), n_used.reshape(1).astype(jnp.int32)


def _ple_tile(h, p, g_ref, wg_ref, wp_ref, fn_ref):
    u = _rms(h, g_ref[...]).astype(BF16)
    emb_gate = _sigmoid(_dot(u, wg_ref[...]))
    h = h + emb_gate * _dot(p.astype(BF16), wp_ref[...])
    if fn_ref is not None:
        h = _rms(h, fn_ref[...])
    return h


def _ple_kernel(h_ref, p_ref, g_ref, wg_ref, wp_ref, *rest, final):
    fn_ref, o_ref = rest if final else (None,) + rest
    o_ref[...] = _ple_tile(h_ref[...], p_ref[...], g_ref, wg_ref, wp_ref, fn_ref)


def _ple_moe_kernel(pos_ref, posn_ref, gate_ref, y_hbm, h_ref, p_ref, g_ref, wg_ref, wp_ref, *rest, final):
    if final:
        fn_ref, o_ref, yg, sems = rest
    else:
        fn_ref, (o_ref, yg, sems) = None, rest
    j = pl.program_id(0)

    def start_row(idx_ref, slot, r):
        for k in range(TOP_K):
            s = idx_ref[0, slot, TOP_K * r + k]
            pltpu.make_async_copy(y_hbm.at[pl.ds(s, 1)], yg.at[slot, k, pl.ds(r, 1)],
                                  sems.at[slot]).start(priority=k)

    def wait_rows(slot):
        for k in range(TOP_K):
            pltpu.make_async_copy(y_hbm.at[pl.ds(0, ROW_TILE)], yg.at[slot, k], sems.at[slot]).wait()

    @pl.when(j == 0)
    def _():
        for slot in range(2):
            def body(r, carry, slot=slot):
                start_row(pos_ref, slot, r)
                return carry
            lax.fori_loop(0, ROW_TILE, body, 0, unroll=8)

    for slot in range(2):
        rows = pl.ds(slot * ROW_TILE, ROW_TILE)
        wait_rows(slot)
        h = h_ref[rows, :]
        gate = gate_ref[rows, :]
        for k in range(TOP_K):
            h = h + gate[:, k:k + 1] * yg[slot, k]
        o_ref[rows, :] = _ple_tile(h, p_ref[rows, :], g_ref, wg_ref, wp_ref, fn_ref)
        for r in range(ROW_TILE):
            start_row(posn_ref, slot, r)

    @pl.when(j == pl.num_programs(0) - 1)
    def _():
        for slot in range(2):
            wait_rows(slot)


def _ple(h, moe_in, p, g, wg, wp, final_g):
    moe = moe_in is not None
    final = final_g is not None
    tile = 2 * ROW_TILE if moe else ROW_TILE
    n = TOKENS // tile
    row = lambda width: pl.BlockSpec((tile, width), lambda i: (i, 0))
    in_specs, args, scratch = [], [], []
    if moe:
        pos, gate, y = moe_in
        pos3 = pos.reshape(n, 2, TOP_K * ROW_TILE)
        smem = lambda imap: pl.BlockSpec((1, 2, TOP_K * ROW_TILE), imap, memory_space=pltpu.SMEM)
        in_specs += [smem(lambda i: (i, 0, 0)), smem(lambda i: (jnp.minimum(i + 1, n - 1), 0, 0)),
                     row(TOP_K), pl.BlockSpec(memory_space=pl.ANY)]
        args += [pos3, pos3, gate, y]
        scratch = [pltpu.VMEM((2, TOP_K, ROW_TILE, D_MODEL), F32), pltpu.SemaphoreType.DMA((2,))]
    in_specs += [row(D_MODEL), row(PLE_DIM), _resident((1, D_MODEL)), _resident((D_MODEL, D_MODEL)),
                 _resident((PLE_DIM, D_MODEL))]
    args += [h, p, g, wg, wp]
    if final:
        in_specs.append(_resident((1, D_MODEL)))
        args.append(final_g)
    return pl.pallas_call(
        functools.partial(_ple_moe_kernel if moe else _ple_kernel, final=final),
        grid=(n,),
        in_specs=in_specs,
        out_specs=row(D_MODEL),
        out_shape=jax.ShapeDtypeStruct((TOKENS, D_MODEL), F32),
        scratch_shapes=scratch,
        compiler_params=_cparams(("arbitrary",)),
        name="ple_moe" if moe else "ple",
    )(*args)


def _t5_bucket(dist):
    max_exact = REL_BUCKETS // 2
    d = jnp.maximum(dist, 0)
    large = max_exact + (jnp.log(jnp.maximum(d, 1).astype(jnp.float32) / max_exact)
                         / math.log(REL_MAX_DIST / max_exact)
                         * (REL_BUCKETS - max_exact)).astype(jnp.int32)
    large = jnp.minimum(large, REL_BUCKETS - 1)
    return jnp.where(d < max_exact, d, large)


def kernel(x, p, w_in, sinks, rel_bias, lb_logits, hgrn_norm, w_branch_a, w_branch_b, w_out,
           norm_mix, norm_ffn, norm_ple, dense_w1, dense_w3, dense_w2, router_w, moe_w1,
           moe_w3, moe_w2, ple_proj, ple_gate, final_norm):
    qi = jnp.arange(ATT_BLOCK)[:, None]
    kj = jnp.arange(2 * ATT_BLOCK)[None, :]
    bucket = _t5_bucket(qi + ATT_BLOCK - kj)[None]
    band_bias = jnp.zeros((ATT_Q_HEADS, ATT_BLOCK, 2 * ATT_BLOCK), F32)
    for b in range(REL_BUCKETS):
        band_bias = jnp.where(bucket == b, rel_bias[b].astype(F32)[:, None, None], band_bias)

    bf = lambda w: w.astype(BF16)
    h = x.reshape(TOKENS, D_MODEL)
    pt = p.reshape(DEPTH, TOKENS, PLE_DIM)
    vec = lambda g: g.reshape(1, -1)
    n_moe = moe_w1.shape[0]
    w1_all = moe_w1.reshape(n_moe * N_EXPERTS * D_MODEL, FFN_EXPERT)
    w3_all = moe_w3.reshape(n_moe * N_EXPERTS * D_MODEL, FFN_EXPERT)
    w2_all = moe_w2.reshape(n_moe * N_EXPERTS * FFN_EXPERT, D_MODEL)

    def stacked(w):
        return w.reshape(w.shape[0] * w.shape[1], w.shape[2])

    def layer_rider(w, index, rows):
        blocks = w.shape[1] // rows
        return stacked(w), index * blocks, rows, blocks

    w_in_next = bf(w_in[0])
    for l in range(DEPTH):
        moe = l % 2 == 1
        riders = [layer_rider(w_branch_a, l, 16), layer_rider(w_branch_b, l, 16),
                  layer_rider(w_out, l, 32), layer_rider(ple_gate, l, 32)]
        if not moe:
            riders += [layer_rider(dense_w1, l // 2, 32), layer_rider(dense_w3, l // 2, 32),
                       layer_rider(dense_w2, l // 2, 176)]
        if l + 1 < DEPTH:
            riders.append(layer_rider(w_in, l + 1, 32))
        proj, cast = _in_proj(h, vec(norm_mix[l]), w_in_next, riders)
        wa, wb, wo, wpg = cast[:4]
        if l + 1 < DEPTH:
            w_in_next = cast[-1]
        final_g = vec(final_norm) if l == DEPTH - 1 else None
        if not moe:
            (h,) = _mixer(proj, h, sinks[l], band_bias, lb_logits, vec(hgrn_norm[l]), wa, wb, wo, l)
            h = _dense_ffn(h, vec(norm_ffn[l]), cast[4], cast[5], cast[6])
            y = None
        else:
            rw = jnp.pad(router_w[l // 2], ((0, 0), (0, ROUTER_LANES - N_EXPERTS)))
            rw_hi = bf(rw)
            rw_lo = bf(rw - rw_hi.astype(F32))
            moe_in = (vec(norm_ffn[l]), rw_hi, rw_lo, w1_all, w3_all, (l // 2) * MIXER_STEPS)
            h, u, idx, gate, w1b, w3b = _mixer(proj, h, sinks[l], band_bias, lb_logits, vec(hgrn_norm[l]),
                                               wa, wb, wo, l, moe_in)
            pos, tile_expert, n_used = _moe_plan(idx)
            xs, w2b = _dispatch(u, pos, w2_all, l // 2)
            y = _moe_experts(xs, tile_expert, n_used,
                             w1b.reshape(N_EXPERTS, D_MODEL, FFN_EXPERT),
                             w3b.reshape(N_EXPERTS, D_MODEL, FFN_EXPERT),
                             w2b.reshape(N_EXPERTS, FFN_EXPERT, D_MODEL))
            y = (pos, gate, y)
        h = _ple(h, y, pt[l], vec(norm_ple[l]), wpg, bf(ple_proj[l]), final_g)
    return h.reshape(BATCH, SEQ, D_MODEL)
```

```python
import functools
import math

import jax
import jax.numpy as jnp
from jax import lax
from jax.experimental import pallas as pl
from jax.experimental.pallas import tpu as pltpu

F32 = jnp.float32
BF16 = jnp.bfloat16

D_MODEL = 1024
BATCH = 4
SEQ = 4096
TOKENS = BATCH * SEQ
DEPTH = 4
ATT_Q_HEADS = 8
ATT_KV_HEADS = 2
ATT_HEAD_DIM = 64
ATT_GROUP = ATT_Q_HEADS // ATT_KV_HEADS
WINDOW = 128
ATT_BLOCK = 128
REL_BUCKETS = 32
REL_MAX_DIST = 128
HG_HEADS = 4
HG_DK = 128
HG_DV = 128
ATT_Q_W = ATT_Q_HEADS * ATT_HEAD_DIM
ATT_KV_W = ATT_KV_HEADS * ATT_HEAD_DIM
HG_K_W = HG_HEADS * HG_DK
HG_V_W = HG_HEADS * HG_DV
IN_SPLITS = (ATT_Q_W, ATT_KV_W, ATT_KV_W, HG_K_W, HG_K_W, HG_V_W, HG_V_W, D_MODEL, D_MODEL)
IN_WIDTH = sum(IN_SPLITS)
FFN_DENSE = 2816
N_EXPERTS = 8
TOP_K = 2
FFN_EXPERT = 3584
PLE_DIM = 256
EPS = 1e-6

V7X_VMEM_LIMIT_BYTES = 52 * 1024 * 1024
ROW_TILE = 512
HG_TILE = 128
HG_SUB = 32
DENSE_NF = 2
DENSE_F_TILE = FFN_DENSE // DENSE_NF
MOE_ROW_TILE = 512
DISPATCH_TILE = 1024
DISPATCH_STEPS = TOKENS // DISPATCH_TILE
DISPATCH_PARTS = 16
assert HG_TILE == ATT_BLOCK
MIXER_BLOCKS = 2
MIXER_TILE = MIXER_BLOCKS * ATT_BLOCK
MIXER_STEPS = TOKENS // MIXER_TILE
W13_ROWS = N_EXPERTS * D_MODEL // MIXER_STEPS
W2_ROWS = N_EXPERTS * FFN_EXPERT // DISPATCH_STEPS
MOE_NF = 2
MOE_F_TILE = FFN_EXPERT // MOE_NF
MOE_TILES = (TOKENS * TOP_K) // MOE_ROW_TILE + N_EXPERTS - 1
MOE_ROWS = MOE_TILES * MOE_ROW_TILE
COL_CHUNK = 512
ROUTER_LANES = 128


def _cparams(sem):
    return pltpu.CompilerParams(dimension_semantics=sem, vmem_limit_bytes=V7X_VMEM_LIMIT_BYTES)


def _rms(x, g):
    return x * lax.rsqrt(jnp.mean(x * x, axis=-1, keepdims=True) + EPS) * g


def _sigmoid(x):
    return 1.0 / (1.0 + jnp.exp(-x))


def _dot(a, b):
    return jnp.dot(a, b, preferred_element_type=F32)


def _dot_nt(a, b):
    return lax.dot_general(a, b, (((1,), (1,)), ((), ())), preferred_element_type=F32)


def _resident(shape):
    nd = len(shape)
    return pl.BlockSpec(shape, lambda *_: (0,) * nd)


def _cast_rider(w2d, first_block, rows, n_steps, step_of):
    cols = w2d.shape[1]
    in_spec = pl.BlockSpec((rows, cols), lambda *ids: (first_block + step_of(*ids), 0))
    out_spec = pl.BlockSpec((rows, cols), lambda *ids: (step_of(*ids), 0))
    return in_spec, out_spec, jax.ShapeDtypeStruct((rows * n_steps, cols), BF16)


IN_DTYPES = (BF16, BF16, BF16, BF16, F32, BF16, BF16, BF16, BF16)
ATT_SCALE = ATT_HEAD_DIM ** -0.5


def _in_proj_kernel(*refs, n_riders):
    h_ref, g_ref, w_ref = refs[:3]
    rider_in = refs[3:3 + n_riders]
    out_refs = refs[3 + n_riders:len(refs) - n_riders]
    rider_out = refs[len(refs) - n_riders:]
    for wi_ref, wo_ref in zip(rider_in, rider_out):
        wo_ref[...] = wi_ref[...].astype(BF16)
    xn = _rms(h_ref[...], g_ref[...]).astype(BF16)
    main_refs, (ak_sw_ref, av_sw_ref) = out_refs[:len(IN_SPLITS)], out_refs[len(IN_SPLITS):]
    off = 0
    for idx, (o_ref, width) in enumerate(zip(main_refs, IN_SPLITS)):
        for c in range(0, width, COL_CHUNK):
            cw = min(COL_CHUNK, width - c)
            z = _dot(xn, w_ref[:, off + c:off + c + cw])
            if idx == 0:
                z = z * ATT_SCALE
            o_ref[:, c:c + cw] = z.astype(o_ref.dtype)
            if idx in (1, 2):
                sw_ref = ak_sw_ref if idx == 1 else av_sw_ref
                sw_ref[...] = jnp.concatenate([z[:, ATT_HEAD_DIM:], z[:, :ATT_HEAD_DIM]], axis=1).astype(sw_ref.dtype)
        off += width


def _in_proj(h, g, w, riders):
    n_steps = TOKENS // ROW_TILE
    row = lambda width: pl.BlockSpec((ROW_TILE, width), lambda i: (i, 0))
    widths = IN_SPLITS + (ATT_KV_W, ATT_KV_W)
    dtypes = IN_DTYPES + (BF16, BF16)
    in_specs = [row(D_MODEL), _resident((1, D_MODEL)), _resident((D_MODEL, IN_WIDTH))]
    args = [h, g, w]
    out_specs = [row(wd) for wd in widths]
    out_shape = [jax.ShapeDtypeStruct((TOKENS, wd), dt) for wd, dt in zip(widths, dtypes)]
    for w2d, first_block, rows, n_blocks in riders:
        assert n_blocks <= n_steps
        r_in, r_out, r_shape = _cast_rider(w2d, first_block, rows, n_blocks,
                                           lambda i, nb=n_blocks: jnp.minimum(i, nb - 1))
        in_specs.append(r_in)
        args.append(w2d)
        out_specs.append(r_out)
        out_shape.append(r_shape)
    outs = pl.pallas_call(
        functools.partial(_in_proj_kernel, n_riders=len(riders)),
        grid=(n_steps,),
        in_specs=in_specs,
        out_specs=out_specs,
        out_shape=out_shape,
        compiler_params=_cparams(("arbitrary",)),
        name="in_proj",
    )(*args)
    return outs[:len(widths)], outs[len(widths):]


def _block_diag(a, a_sw, g):
    lane = lax.broadcasted_iota(jnp.int32, a.shape, 1)
    low = lane < ATT_HEAD_DIM
    own, other = (a, a_sw) if g == 0 else (a_sw, a)
    zero = jnp.zeros_like(a)
    first = jnp.where(low, own, zero)
    second = jnp.where(low, zero, other)
    cat = jnp.concatenate
    return cat([cat([first, zero], 1), cat([second, zero], 1),
                cat([zero, first], 1), cat([zero, second], 1)], 0)


def _attn_block(sink_ref, q_ref, kc_ref, kp_ref, kcs_ref, kps_ref, vc_ref, vp_ref, vcs_ref, vps_ref,
                bias_ref, first):
    band = 2 * ATT_BLOCK
    r = lax.broadcasted_iota(jnp.int32, (ATT_BLOCK, band), 0)
    j = lax.broadcasted_iota(jnp.int32, (ATT_BLOCK, band), 1)
    dist = r + ATT_BLOCK - j
    valid = (dist >= 0) & (dist < WINDOW)
    if first is not False:
        valid = valid & ((j >= ATT_BLOCK) | jnp.logical_not(first))
    cat = jnp.concatenate
    k, ks = cat([kp_ref[...], kc_ref[...]], 0), cat([kps_ref[...], kcs_ref[...]], 0)
    v, vs = cat([vp_ref[...], vc_ref[...]], 0), cat([vps_ref[...], vcs_ref[...]], 0)
    gw = ATT_GROUP * ATT_HEAD_DIM
    lane = lax.broadcasted_iota(jnp.int32, (ATT_BLOCK, gw), 1)
    outs = []
    for g in range(ATT_KV_HEADS):
        s_all = _dot_nt(q_ref[:, g * gw:(g + 1) * gw], _block_diag(k, ks, g))
        ps, dens = [], []
        for hh in range(ATT_GROUP):
            h = g * ATT_GROUP + hh
            s = s_all[:, hh * band:(hh + 1) * band] + bias_ref[h]
            s = jnp.where(valid, s, -1e30)
            sink = sink_ref[h]
            m = jnp.maximum(jnp.max(s, axis=-1, keepdims=True), sink)
            p = jnp.exp(s - m)
            dens.append(jnp.sum(p, axis=-1, keepdims=True) + jnp.exp(sink - m))
            ps.append(p.astype(BF16))
        o = _dot(cat(ps, 1), _block_diag(v, vs, g))
        den = dens[ATT_GROUP - 1]
        for hh in reversed(range(ATT_GROUP - 1)):
            den = jnp.where(lane < (hh + 1) * ATT_HEAD_DIM, dens[hh], den)
        outs.append(o / den)
    return cat(outs, 1)


SUBLANES = 8


def _cumsum_rows(x, t):
    within = t & (SUBLANES - 1)
    shift = 1
    while shift < SUBLANES:
        x = x + jnp.where(within >= shift, pltpu.roll(x, shift, 0), 0.0)
        shift *= 2
    tiles = []
    carry = jnp.zeros_like(x[0:1, :])
    for r in range(0, x.shape[0], SUBLANES):
        tile = x[r:r + SUBLANES, :] + carry
        tiles.append(tile)
        carry = tile[SUBLANES - 1:SUBLANES, :]
    return jnp.concatenate(tiles, 0)


def _hgrn_tile(q_ref, f_ref, v_ref, g_ref, lbl_ref, ng_ref, st_ref, layer):
    rows = [lbl_ref[i:i + 1, :] for i in range(DEPTH)]
    mx = functools.reduce(jnp.maximum, rows)
    ex = [jnp.exp(rw - mx) for rw in rows]
    tot = functools.reduce(lambda a, b: a + b, ex)
    lower = jnp.zeros_like(mx)
    for i in range(1, layer + 1):
        lower = lower + ex[i] / tot

    c = HG_TILE
    t = lax.broadcasted_iota(jnp.int32, (c, HG_DK), 0)
    ts = lax.broadcasted_iota(jnp.int32, (c, c), 0)
    ss = lax.broadcasted_iota(jnp.int32, (c, c), 1)
    same64 = (ts >> 6) == (ss >> 6)
    diag32 = ((ts >> 5) == (ss >> 5)) & (ss <= ts)
    ng = ng_ref[...]

    outs = []
    for h in range(HG_HEADS):
        sl = slice(h * HG_DK, (h + 1) * HG_DK)
        lb = lower[:, sl]
        f = lb + (1.0 - lb) * _sigmoid(f_ref[:, sl])
        k = 1.0 - f
        a = _cumsum_rows(jnp.log(f), t)
        a_last = a[c - 1:c, :]
        q = q_ref[:, sl].astype(F32)
        v = v_ref[:, sl]
        st = st_ref[h]

        o = _dot_nt((q * jnp.exp(a)).astype(BF16), st.astype(BF16))
        ke = (k * jnp.exp(a_last - a)).astype(BF16)
        vt = v.astype(F32).T.astype(BF16)
        st_ref[h] = st * jnp.exp(a_last) + _dot(vt, ke)

        zeros = jnp.zeros((HG_SUB, HG_DK), BF16)
        ops = [[] for _ in range(6)]
        for r in range(0, c, HG_SUB):
            rows = slice(r, r + HG_SUB)
            ab, qb, kb = a[rows], q[rows], k[rows]
            m1 = a[c // 2 - 1:c // 2, :]
            half = (r // (c // 2)) * (c // 2)
            m2 = a[half + c // 4 - 1:half + c // 4, :]
            m3 = a[r + HG_SUB // 2 - 1:r + HG_SUB // 2, :]
            upper1 = r >= c // 2
            upper2 = r - half >= c // 4
            ops[0].append((qb * jnp.exp(ab - m1)).astype(BF16) if upper1 else zeros)
            ops[1].append(zeros if upper1 else (kb * jnp.exp(m1 - ab)).astype(BF16))
            ops[2].append((qb * jnp.exp(ab - m2)).astype(BF16) if upper2 else zeros)
            ops[3].append(zeros if upper2 else (kb * jnp.exp(m2 - ab)).astype(BF16))
            ops[4].append((qb * jnp.exp(ab - m3)).astype(BF16))
            ops[5].append((kb * jnp.exp(m3 - ab)).astype(BF16))
        q1, k1, q2, k2, q3, k3 = (jnp.concatenate(blocks, 0) for blocks in ops)
        p = _dot_nt(q1, k1)
        p = p + jnp.where(same64, _dot_nt(q2, k2), 0.0)
        p = p + jnp.where(diag32, _dot_nt(q3, k3), 0.0)

        o = o + _dot(p.astype(BF16), v)
        o = o * lax.rsqrt(jnp.mean(o * o, axis=-1, keepdims=True) + EPS) * ng
        gate = g_ref[:, sl].astype(F32)
        outs.append(o * (gate * _sigmoid(gate)))
    return jnp.concatenate(outs, 1)


def _route(u, rwh_ref, rwl_ref):
    u_hi = u.astype(BF16)
    u_lo = (u - u_hi.astype(F32)).astype(BF16)
    logits = _dot(u_hi, rwh_ref[...]) + (_dot(u_hi, rwl_ref[...]) + _dot(u_lo, rwh_ref[...]))
    lane = lax.broadcasted_iota(jnp.int32, logits.shape, 1).astype(F32)
    none = float(logits.shape[1])
    l0 = jnp.where(lane < N_EXPERTS, logits, -jnp.inf)
    m1 = jnp.max(l0, axis=-1, keepdims=True)
    i1 = jnp.min(jnp.where(l0 == m1, lane, none), axis=-1, keepdims=True)
    l1 = jnp.where(lane == i1, -jnp.inf, l0)
    m2 = jnp.max(l1, axis=-1, keepdims=True)
    i2 = jnp.min(jnp.where(l1 == m2, lane, none), axis=-1, keepdims=True)
    e2 = jnp.exp(m2 - m1)
    den = 1.0 + e2
    idx = jnp.concatenate([i1, i2], axis=1).astype(jnp.int32)
    return idx, jnp.concatenate([1.0 / den, e2 / den], axis=1)


N_ATT_IN = 11
N_HGRN_IN = 6
N_MERGE_IN = 6


def _mixer_kernel(*refs, layer, moe):
    refs = list(refs)
    att_in = refs[:N_ATT_IN]
    hg_in = refs[N_ATT_IN:N_ATT_IN + N_HGRN_IN]
    ga_ref, gb_ref, h_ref, wa_ref, wb_ref, wo_ref = refs[N_ATT_IN + N_HGRN_IN:N_ATT_IN + N_HGRN_IN + N_MERGE_IN]
    rest = refs[N_ATT_IN + N_HGRN_IN + N_MERGE_IN:]
    if moe:
        (gn_ref, rwh_ref, rwl_ref, w1_ref, w3_ref, ho_ref, u_ref, idx_ref, gate_ref, w1o_ref, w3o_ref,
         st_ref) = rest
        w1o_ref[...] = w1_ref[...].astype(BF16)
        w3o_ref[...] = w3_ref[...].astype(BF16)
    else:
        ho_ref, st_ref = rest

    @pl.when(pl.program_id(1) == 0)
    def _():
        st_ref[...] = jnp.zeros_like(st_ref)

    sink_ref, q_ref, kc_ref, kp_ref, kcs_ref, kps_ref, vc_ref, vp_ref, vcs_ref, vps_ref, bias_ref = att_in
    hq_ref, hf_ref, hi_ref, hgt_ref, lbl_ref, ng_ref = hg_in
    for sub in range(MIXER_BLOCKS):
        rows = pl.ds(sub * ATT_BLOCK, ATT_BLOCK)
        if sub == 0:
            prev = (kp_ref, kps_ref, vp_ref, vps_ref)
            first = pl.program_id(1) == 0
        else:
            before = pl.ds((sub - 1) * ATT_BLOCK, ATT_BLOCK)
            prev = tuple(r.at[before] for r in (kc_ref, kcs_ref, vc_ref, vcs_ref))
            first = False
        att = _attn_block(sink_ref, q_ref.at[rows], kc_ref.at[rows], prev[0], kcs_ref.at[rows], prev[1],
                          vc_ref.at[rows], prev[2], vcs_ref.at[rows], prev[3], bias_ref, first)
        hgo = _hgrn_tile(hq_ref.at[rows], hf_ref.at[rows], hi_ref.at[rows], hgt_ref.at[rows], lbl_ref, ng_ref,
                         st_ref, layer)
        ya = _dot(att.astype(BF16), wa_ref[...])
        yb = _dot(hgo.astype(BF16), wb_ref[...])
        merged = _sigmoid(ga_ref[rows, :]).astype(F32) * ya + _sigmoid(gb_ref[rows, :]).astype(F32) * yb
        hn = h_ref[rows, :] + _dot(merged.astype(BF16), wo_ref[...])
        ho_ref[rows, :] = hn
        if moe:
            u = _rms(hn, gn_ref[...])
            bits = pltpu.bitcast(u.astype(BF16).astype(F32), jnp.uint32)
            u_ref[rows, :] = (bits[:, :D_MODEL // 2] >> 16) | (bits[:, D_MODEL // 2:] & jnp.uint32(0xFFFF0000))
            idx_ref[rows, :], gate_ref[rows, :] = _route(u, rwh_ref, rwl_ref)


def _mixer(proj, h, sinks, band_bias, lb_logits, norm_g, wa, wb, wo, layer, moe_in=None):
    aq, ak, av, hq, hf, hi, hg, ga, gb, ak_sw, av_sw = proj
    moe = moe_in is not None
    nb = SEQ // MIXER_TILE
    step = lambda b, n: b * nb + n
    cur = lambda b, n: (step(b, n), 0)
    prev = lambda b, n: (jnp.maximum(step(b, n) * MIXER_BLOCKS - 1, 0), 0)
    blk = lambda width: pl.BlockSpec((MIXER_TILE, width), cur)
    kv_cur, kv_prev = blk(ATT_KV_W), pl.BlockSpec((ATT_BLOCK, ATT_KV_W), prev)
    in_specs = [
        pl.BlockSpec(memory_space=pltpu.SMEM), blk(ATT_Q_W),
        kv_cur, kv_prev, kv_cur, kv_prev, kv_cur, kv_prev, kv_cur, kv_prev,
        _resident((ATT_Q_HEADS, ATT_BLOCK, 2 * ATT_BLOCK)),
        blk(HG_K_W), blk(HG_K_W), blk(HG_V_W), blk(HG_V_W), _resident((DEPTH, HG_K_W)), _resident((1, HG_DV)),
        blk(D_MODEL), blk(D_MODEL), blk(D_MODEL),
        _resident((ATT_Q_W, D_MODEL)), _resident((HG_V_W, D_MODEL)), _resident((D_MODEL, D_MODEL)),
    ]
    args = [sinks, aq, ak, ak, ak_sw, ak_sw, av, av, av_sw, av_sw, band_bias,
            hq, hf, hi, hg, lb_logits, norm_g, ga, gb, h, wa, wb, wo]
    assert len(in_specs) == N_ATT_IN + N_HGRN_IN + N_MERGE_IN
    out_specs = [blk(D_MODEL)]
    out_shape = [jax.ShapeDtypeStruct((TOKENS, D_MODEL), F32)]
    if moe:
        gn, rw_hi, rw_lo, w1_all, w3_all, first_block = moe_in
        in_specs += [_resident((1, D_MODEL)), _resident((D_MODEL, ROUTER_LANES)),
                     _resident((D_MODEL, ROUTER_LANES))]
        args += [gn, rw_hi, rw_lo]
        out_specs += [blk(D_MODEL // 2), blk(TOP_K), blk(TOP_K)]
        out_shape += [jax.ShapeDtypeStruct((TOKENS, D_MODEL // 2), jnp.uint32),
                      jax.ShapeDtypeStruct((TOKENS, TOP_K), jnp.int32),
                      jax.ShapeDtypeStruct((TOKENS, TOP_K), F32)]
        for w_all in (w1_all, w3_all):
            r_in, r_out, r_shape = _cast_rider(w_all, first_block, W13_ROWS, MIXER_STEPS, step)
            in_specs.append(r_in)
            args.append(w_all)
            out_specs.append(r_out)
            out_shape.append(r_shape)
    return pl.pallas_call(
        functools.partial(_mixer_kernel, layer=layer, moe=moe),
        grid=(BATCH, nb),
        in_specs=in_specs,
        out_specs=out_specs,
        out_shape=out_shape,
        scratch_shapes=[pltpu.VMEM((HG_HEADS, HG_DV, HG_DK), F32)],
        compiler_params=_cparams(("arbitrary", "arbitrary")),
        name="mixer_moe" if moe else "mixer",
    )(*args)


def _swiglu_part(x, w1, w3, w2):
    a = _dot(x, w1)
    b = _dot(x, w3)
    return _dot((a * _sigmoid(a) * b).astype(BF16), w2)


def _dense_ffn_kernel(h_ref, g_ref, w1_ref, w3_ref, w2_ref, o_ref, x_scr, acc_scr):
    f = pl.program_id(1)

    @pl.when(f == 0)
    def _():
        x = _rms(h_ref[...], g_ref[...]).astype(BF16)
        x_scr[...] = x
        acc_scr[...] = _swiglu_part(x, w1_ref[...], w3_ref[...], w2_ref[...])

    @pl.when(f == 1)
    def _():
        o_ref[...] = h_ref[...] + (acc_scr[...] + _swiglu_part(x_scr[...], w1_ref[...], w3_ref[...], w2_ref[...]))


def _dense_ffn(h, g, w1, w3, w2):
    return pl.pallas_call(
        _dense_ffn_kernel,
        grid=(TOKENS // ROW_TILE, DENSE_NF),
        in_specs=[
            pl.BlockSpec((ROW_TILE, D_MODEL), lambda i, f: (i, 0)),
            _resident((1, D_MODEL)),
            pl.BlockSpec((D_MODEL, DENSE_F_TILE), lambda i, f: (0, f)),
            pl.BlockSpec((D_MODEL, DENSE_F_TILE), lambda i, f: (0, f)),
            pl.BlockSpec((DENSE_F_TILE, D_MODEL), lambda i, f: (f, 0)),
        ],
        out_specs=pl.BlockSpec((ROW_TILE, D_MODEL), lambda i, f: (i, 0)),
        out_shape=jax.ShapeDtypeStruct((TOKENS, D_MODEL), F32),
        scratch_shapes=[pltpu.VMEM((ROW_TILE, D_MODEL), BF16), pltpu.VMEM((ROW_TILE, D_MODEL), F32)],
        compiler_params=_cparams(("arbitrary", "arbitrary")),
        name="dense_ffn",
    )(h, g, w1, w3, w2)


def _dispatch_kernel(pos_ref, u_ref, w2_ref, xs_in_ref, xs_ref, w2o_ref, sem):
    del xs_in_ref
    tok = DISPATCH_TILE // DISPATCH_PARTS
    r2 = W2_ROWS // DISPATCH_PARTS
    for part in range(DISPATCH_PARTS):
        s2 = slice(part * r2, (part + 1) * r2)
        w2o_ref[s2, :] = w2_ref[s2, :].astype(BF16)
        for r in range(part * tok, (part + 1) * tok):
            for k in range(TOP_K):
                d = pos_ref[0, 0, TOP_K * r + k]
                pltpu.make_async_copy(u_ref.at[pl.ds(r, 1)], xs_ref.at[pl.ds(d, 1)], sem).start(priority=k)

    for k in range(TOP_K):
        pltpu.make_async_copy(u_ref, xs_ref.at[pl.ds(0, DISPATCH_TILE)], sem).wait()


def _dispatch(u_packed, pos, w2_all, moe_layer):
    n = DISPATCH_STEPS
    xs0 = jnp.zeros((MOE_ROWS, D_MODEL // 2), jnp.uint32)
    w2_in, w2_out, w2_shape = _cast_rider(w2_all, moe_layer * n, W2_ROWS, n, lambda i: i)
    return pl.pallas_call(
        _dispatch_kernel,
        grid=(n,),
        in_specs=[
            pl.BlockSpec((1, 1, TOP_K * DISPATCH_TILE), lambda i: (i, 0, 0), memory_space=pltpu.SMEM),
            pl.BlockSpec((DISPATCH_TILE, D_MODEL // 2), lambda i: (i, 0)),
            w2_in,
            pl.BlockSpec(memory_space=pl.ANY),
        ],
        out_specs=[pl.BlockSpec(memory_space=pl.ANY), w2_out],
        out_shape=[jax.ShapeDtypeStruct((MOE_ROWS, D_MODEL // 2), jnp.uint32), w2_shape],
        scratch_shapes=[pltpu.SemaphoreType.DMA(())],
        input_output_aliases={3: 0},
        compiler_params=_cparams(("arbitrary",)),
        name="moe_dispatch",
    )(pos.reshape(n, 1, TOP_K * DISPATCH_TILE), u_packed, w2_all, xs0)


def _moe_kernel(te_ref, nu_ref, x_ref, w1_ref, w3_ref, w2_ref, y_ref, xb, acc):
    i = pl.program_id(0)
    f = pl.program_id(1)
    valid = i < nu_ref[0]

    @pl.when((f == 0) & valid)
    def _():
        w = x_ref[...]
        x = jnp.concatenate([pltpu.bitcast(w << 16, F32).astype(BF16),
                             pltpu.bitcast(w & jnp.uint32(0xFFFF0000), F32).astype(BF16)], axis=1)
        xb[...] = x
        acc[...] = _swiglu_part(x, w1_ref[0], w3_ref[0], w2_ref[0])

    @pl.when((f == 1) & valid)
    def _():
        y_ref[...] = acc[...] + _swiglu_part(xb[...], w1_ref[0], w3_ref[0], w2_ref[0])

    @pl.when((f == 1) & jnp.logical_not(valid))
    def _():
        y_ref[...] = jnp.zeros_like(y_ref)


def _moe_experts(xs, tile_expert, n_used, w1, w3, w2):
    row_blk = lambda i, f, te, nu: (jnp.minimum(i, nu[0] - 1), 0)
    f_blk = lambda i, f, nu: jnp.where(i < nu[0], f, MOE_NF - 1)
    grid_spec = pltpu.PrefetchScalarGridSpec(
        num_scalar_prefetch=2,
        grid=(MOE_TILES, MOE_NF),
        in_specs=[
            pl.BlockSpec((MOE_ROW_TILE, D_MODEL // 2), row_blk),
            pl.BlockSpec((1, D_MODEL, MOE_F_TILE), lambda i, f, te, nu: (te[i], 0, f_blk(i, f, nu))),
            pl.BlockSpec((1, D_MODEL, MOE_F_TILE), lambda i, f, te, nu: (te[i], 0, f_blk(i, f, nu))),
            pl.BlockSpec((1, MOE_F_TILE, D_MODEL), lambda i, f, te, nu: (te[i], f_blk(i, f, nu), 0)),
        ],
        out_specs=pl.BlockSpec((MOE_ROW_TILE, D_MODEL), lambda i, f, te, nu: (i, 0)),
        scratch_shapes=[
            pltpu.VMEM((MOE_ROW_TILE, D_MODEL), BF16),
            pltpu.VMEM((MOE_ROW_TILE, D_MODEL), F32),
        ],
    )
    return pl.pallas_call(
        _moe_kernel,
        grid_spec=grid_spec,
        out_shape=jax.ShapeDtypeStruct((MOE_ROWS, D_MODEL), F32),
        compiler_params=_cparams(("arbitrary", "arbitrary")),
        name="moe_experts",
    )(tile_expert, n_used, xs, w1, w3, w2)


def _moe_plan(idx):
    e_flat = idx.reshape(-1)
    onehot = (e_flat[:, None] == jnp.arange(N_EXPERTS, dtype=jnp.int32)[None, :]).astype(jnp.int32)
    csum = jnp.cumsum(onehot, axis=0)
    counts = csum[-1]
    rank = jnp.sum((csum - onehot) * onehot, axis=1)
    tiles_per = (counts + MOE_ROW_TILE - 1) // MOE_ROW_TILE
    tile_end = jnp.cumsum(tiles_per)
    tile_start = tile_end - tiles_per
    pos = jnp.sum(onehot * tile_start[None, :], axis=1) * MOE_ROW_TILE + rank
    tile_id = jnp.arange(MOE_TILES, dtype=jnp.int32)
    te = jnp.sum((tile_id[:, None] >= tile_end[None, :]).astype(jnp.int32), axis=1)
    n_used = tile_end[-1]
    te = jnp.minimum(te, N_EXPERTS - 1)
    te = jnp.where(tile_id < n_used, te, te[jnp.maximum(n_used - 1, 0)])
    return pos.astype(jnp.int32), te.astype(jnp.int32), n_used.reshape(1).astype(jnp.int32)


def _ple_tile(h, p, g_ref, wg_ref, wp_ref, fn_ref):
    u = _rms(h, g_ref[...]).astype(BF16)
    emb_gate = _sigmoid(_dot(u, wg_ref[...]))
    h = h + emb_gate * _dot(p.astype(BF16), wp_ref[...])
    if fn_ref is not None:
        h = _rms(h, fn_ref[...])
    return h


def _ple_kernel(h_ref, p_ref, g_ref, wg_ref, wp_ref, *rest, final):
    fn_ref, o_ref = rest if final else (None,) + rest
    o_ref[...] = _ple_tile(h_ref[...], p_ref[...], g_ref, wg_ref, wp_ref, fn_ref)


def _ple_moe_kernel(pos_ref, posn_ref, gate_ref, y_hbm, h_ref, p_ref, g_ref, wg_ref, wp_ref, *rest, final):
    if final:
        fn_ref, o_ref, yg, sems = rest
    else:
        fn_ref, (o_ref, yg, sems) = None, rest
    j = pl.program_id(0)

    def start_row(idx_ref, slot, r):
        for k in range(TOP_K):
            s = idx_ref[0, slot, TOP_K * r + k]
            pltpu.make_async_copy(y_hbm.at[pl.ds(s, 1)], yg.at[slot, k, pl.ds(r, 1)],
                                  sems.at[slot]).start(priority=k)

    def wait_rows(slot):
        for k in range(TOP_K):
            pltpu.make_async_copy(y_hbm.at[pl.ds(0, ROW_TILE)], yg.at[slot, k], sems.at[slot]).wait()

    @pl.when(j == 0)
    def _():
        for slot in range(2):
            def body(r, carry, slot=slot):
                start_row(pos_ref, slot, r)
                return carry
            lax.fori_loop(0, ROW_TILE, body, 0, unroll=8)

    for slot in range(2):
        rows = pl.ds(slot * ROW_TILE, ROW_TILE)
        wait_rows(slot)
        h = h_ref[rows, :]
        gate = gate_ref[rows, :]
        for k in range(TOP_K):
            h = h + gate[:, k:k + 1] * yg[slot, k]
        o_ref[rows, :] = _ple_tile(h, p_ref[rows, :], g_ref, wg_ref, wp_ref, fn_ref)
        for r in range(ROW_TILE):
            start_row(posn_ref, slot, r)

    @pl.when(j == pl.num_programs(0) - 1)
    def _():
        for slot in range(2):
            wait_rows(slot)


def _ple(h, moe_in, p, g, wg, wp, final_g):
    moe = moe_in is not None
    final = final_g is not None
    tile = 2 * ROW_TILE if moe else ROW_TILE
    n = TOKENS // tile
    row = lambda width: pl.BlockSpec((tile, width), lambda i: (i, 0))
    in_specs, args, scratch = [], [], []
    if moe:
        pos, gate, y = moe_in
        pos3 = pos.reshape(n, 2, TOP_K * ROW_TILE)
        smem = lambda imap: pl.BlockSpec((1, 2, TOP_K * ROW_TILE), imap, memory_space=pltpu.SMEM)
        in_specs += [smem(lambda i: (i, 0, 0)), smem(lambda i: (jnp.minimum(i + 1, n - 1), 0, 0)),
                     row(TOP_K), pl.BlockSpec(memory_space=pl.ANY)]
        args += [pos3, pos3, gate, y]
        scratch = [pltpu.VMEM((2, TOP_K, ROW_TILE, D_MODEL), F32), pltpu.SemaphoreType.DMA((2,))]
    in_specs += [row(D_MODEL), row(PLE_DIM), _resident((1, D_MODEL)), _resident((D_MODEL, D_MODEL)),
                 _resident((PLE_DIM, D_MODEL))]
    args += [h, p, g, wg, wp]
    if final:
        in_specs.append(_resident((1, D_MODEL)))
        args.append(final_g)
    return pl.pallas_call(
        functools.partial(_ple_moe_kernel if moe else _ple_kernel, final=final),
        grid=(n,),
        in_specs=in_specs,
        out_specs=row(D_MODEL),
        out_shape=jax.ShapeDtypeStruct((TOKENS, D_MODEL), F32),
        scratch_shapes=scratch,
        compiler_params=_cparams(("arbitrary",)),
        name="ple_moe" if moe else "ple",
    )(*args)


def _t5_bucket(dist):
    max_exact = REL_BUCKETS // 2
    d = jnp.maximum(dist, 0)
    large = max_exact + (jnp.log(jnp.maximum(d, 1).astype(jnp.float32) / max_exact)
                         / math.log(REL_MAX_DIST / max_exact)
                         * (REL_BUCKETS - max_exact)).astype(jnp.int32)
    large = jnp.minimum(large, REL_BUCKETS - 1)
    return jnp.where(d < max_exact, d, large)


def kernel(x, p, w_in, sinks, rel_bias, lb_logits, hgrn_norm, w_branch_a, w_branch_b, w_out,
           norm_mix, norm_ffn, norm_ple, dense_w1, dense_w3, dense_w2, router_w, moe_w1,
           moe_w3, moe_w2, ple_proj, ple_gate, final_norm):
    qi = jnp.arange(ATT_BLOCK)[:, None]
    kj = jnp.arange(2 * ATT_BLOCK)[None, :]
    bucket = _t5_bucket(qi + ATT_BLOCK - kj)[None]
    band_bias = jnp.zeros((ATT_Q_HEADS, ATT_BLOCK, 2 * ATT_BLOCK), F32)
    for b in range(REL_BUCKETS):
        band_bias = jnp.where(bucket == b, rel_bias[b].astype(F32)[:, None, None], band_bias)

    bf = lambda w: w.astype(BF16)
    h = x.reshape(TOKENS, D_MODEL)
    pt = p.reshape(DEPTH, TOKENS, PLE_DIM)
    vec = lambda g: g.reshape(1, -1)
    n_moe = moe_w1.shape[0]
    w1_all = moe_w1.reshape(n_moe * N_EXPERTS * D_MODEL, FFN_EXPERT)
    w3_all = moe_w3.reshape(n_moe * N_EXPERTS * D_MODEL, FFN_EXPERT)
    w2_all = moe_w2.reshape(n_moe * N_EXPERTS * FFN_EXPERT, D_MODEL)

    def stacked(w):
        return w.reshape(w.shape[0] * w.shape[1], w.shape[2])

    def layer_rider(w, index, rows):
        blocks = w.shape[1] // rows
        return stacked(w), index * blocks, rows, blocks

    w_in_next = bf(w_in[0])
    for l in range(DEPTH):
        moe = l % 2 == 1
        riders = [layer_rider(w_branch_a, l, 16), layer_rider(w_branch_b, l, 16),
                  layer_rider(w_out, l, 32), layer_rider(ple_gate, l, 32)]
        if not moe:
            riders += [layer_rider(dense_w1, l // 2, 32), layer_rider(dense_w3, l // 2, 32),
                       layer_rider(dense_w2, l // 2, 176)]
        if l + 1 < DEPTH:
            riders.append(layer_rider(w_in, l + 1, 32))
        proj, cast = _in_proj(h, vec(norm_mix[l]), w_in_next, riders)
        wa, wb, wo, wpg = cast[:4]
        if l + 1 < DEPTH:
            w_in_next = cast[-1]
        final_g = vec(final_norm) if l == DEPTH - 1 else None
        if not moe:
            (h,) = _mixer(proj, h, sinks[l], band_bias, lb_logits, vec(hgrn_norm[l]), wa, wb, wo, l)
            h = _dense_ffn(h, vec(norm_ffn[l]), cast[4], cast[5], cast[6])
            y = None
        else:
            rw = jnp.pad(router_w[l // 2], ((0, 0), (0, ROUTER_LANES - N_EXPERTS)))
            rw_hi = bf(rw)
            rw_lo = bf(rw - rw_hi.astype(F32))
            moe_in = (vec(norm_ffn[l]), rw_hi, rw_lo, w1_all, w3_all, (l // 2) * MIXER_STEPS)
            h, u, idx, gate, w1b, w3b = _mixer(proj, h, sinks[l], band_bias, lb_logits, vec(hgrn_norm[l]),
                                               wa, wb, wo, l, moe_in)
            pos, tile_expert, n_used = _moe_plan(idx)
            xs, w2b = _dispatch(u, pos, w2_all, l // 2)
            y = _moe_experts(xs, tile_expert, n_used,
                             w1b.reshape(N_EXPERTS, D_MODEL, FFN_EXPERT),
                             w3b.reshape(N_EXPERTS, D_MODEL, FFN_EXPERT),
                             w2b.reshape(N_EXPERTS, FFN_EXPERT, D_MODEL))
            y = (pos, gate, y)
        h = _ple(h, y, pt[l], vec(norm_ple[l]), wpg, bf(ple_proj[l]), final_g)
    return h.reshape(BATCH, SEQ, D_MODEL)
```

```python
import functools
import math

import jax
import jax.numpy as jnp
from jax import lax
from jax.experimental import pallas as pl
from jax.experimental.pallas import tpu as pltpu

F32 = jnp.float32
BF16 = jnp.bfloat16

D_MODEL = 1024
BATCH = 4
SEQ = 4096
TOKENS = BATCH * SEQ
DEPTH = 4
ATT_Q_HEADS = 8
ATT_KV_HEADS = 2
ATT_HEAD_DIM = 64
ATT_GROUP = ATT_Q_HEADS // ATT_KV_HEADS
WINDOW = 128
ATT_BLOCK = 128
REL_BUCKETS = 32
REL_MAX_DIST = 128
HG_HEADS = 4
HG_DK = 128
HG_DV = 128
ATT_Q_W = ATT_Q_HEADS * ATT_HEAD_DIM
ATT_KV_W = ATT_KV_HEADS * ATT_HEAD_DIM
HG_K_W = HG_HEADS * HG_DK
HG_V_W = HG_HEADS * HG_DV
IN_SPLITS = (ATT_Q_W, ATT_KV_W, ATT_KV_W, HG_K_W, HG_K_W, HG_V_W, HG_V_W, D_MODEL, D_MODEL)
IN_WIDTH = sum(IN_SPLITS)
FFN_DENSE = 2816
N_EXPERTS = 8
TOP_K = 2
FFN_EXPERT = 3584
PLE_DIM = 256
EPS = 1e-6

V7X_VMEM_LIMIT_BYTES = 52 * 1024 * 1024
ROW_TILE = 512
HG_TILE = 128
HG_SUB = 32
MXU_WIDTH = 256
DENSE_F_GROUPS = ((0, 6 * MXU_WIDTH), (6 * MXU_WIDTH, FFN_DENSE))
assert FFN_DENSE % MXU_WIDTH == 0
MOE_ROW_TILE = 512
DISPATCH_TILE = 1024
DISPATCH_STEPS = TOKENS // DISPATCH_TILE
DISPATCH_PARTS = 16
assert HG_TILE == ATT_BLOCK
MIXER_BLOCKS = 2
MIXER_TILE = MIXER_BLOCKS * ATT_BLOCK
MIXER_STEPS = TOKENS // MIXER_TILE
W13_ROWS = N_EXPERTS * D_MODEL // MIXER_STEPS
W2_ROWS = N_EXPERTS * FFN_EXPERT // DISPATCH_STEPS
MOE_NF = 2
MOE_F_TILE = FFN_EXPERT // MOE_NF
MOE_TILES = (TOKENS * TOP_K) // MOE_ROW_TILE + N_EXPERTS - 1
MOE_ROWS = MOE_TILES * MOE_ROW_TILE
COL_CHUNK = 512
ROUTER_LANES = 128


def _cparams(sem):
    return pltpu.CompilerParams(dimension_semantics=sem, vmem_limit_bytes=V7X_VMEM_LIMIT_BYTES)


def _rms(x, g):
    return x * lax.rsqrt(jnp.mean(x * x, axis=-1, keepdims=True) + EPS) * g


def _sigmoid(x):
    return 1.0 / (1.0 + jnp.exp(-x))


def _dot(a, b):
    return jnp.dot(a, b, preferred_element_type=F32)


def _dot_nt(a, b):
    return lax.dot_general(a, b, (((1,), (1,)), ((), ())), preferred_element_type=F32)


def _resident(shape):
    nd = len(shape)
    return pl.BlockSpec(shape, lambda *_: (0,) * nd)


def _cast_rider(w2d, first_block, rows, n_steps, step_of):
    cols = w2d.shape[1]
    in_spec = pl.BlockSpec((rows, cols), lambda *ids: (first_block + step_of(*ids), 0))
    out_spec = pl.BlockSpec((rows, cols), lambda *ids: (step_of(*ids), 0))
    return in_spec, out_spec, jax.ShapeDtypeStruct((rows * n_steps, cols), BF16)


PROJ_WIDTHS = (ATT_Q_W, 4 * ATT_KV_W, 3 * HG_K_W, HG_K_W, 2 * D_MODEL)
PROJ_DTYPES = (BF16, BF16, BF16, F32, BF16)
PROJ_AQ, PROJ_KV, PROJ_HG, PROJ_HF, PROJ_GATES = range(5)
PROJ_PLACE = ((PROJ_AQ, 0), (PROJ_KV, 0), (PROJ_KV, 2 * ATT_KV_W), (PROJ_HG, 0), (PROJ_HF, 0),
              (PROJ_HG, HG_K_W), (PROJ_HG, 2 * HG_K_W), (PROJ_GATES, 0), (PROJ_GATES, D_MODEL))
ATT_SCALE = ATT_HEAD_DIM ** -0.5


def _in_proj_kernel(*refs, n_riders):
    h_ref, g_ref, w_ref = refs[:3]
    rider_in = refs[3:3 + n_riders]
    out_refs = refs[3 + n_riders:len(refs) - n_riders]
    rider_out = refs[len(refs) - n_riders:]
    for wi_ref, wo_ref in zip(rider_in, rider_out):
        wo_ref[...] = wi_ref[...].astype(BF16)
    xn = _rms(h_ref[...], g_ref[...]).astype(BF16)
    off = 0
    for idx, width in enumerate(IN_SPLITS):
        which, col = PROJ_PLACE[idx]
        o_ref = out_refs[which]
        for c in range(0, width, COL_CHUNK):
            cw = min(COL_CHUNK, width - c)
            z = _dot(xn, w_ref[:, off + c:off + c + cw])
            if which == PROJ_AQ:
                z = z * ATT_SCALE
            o_ref[:, col + c:col + c + cw] = z.astype(o_ref.dtype)
            if which == PROJ_KV:
                swapped = jnp.concatenate([z[:, ATT_HEAD_DIM:], z[:, :ATT_HEAD_DIM]], axis=1)
                o_ref[:, col + ATT_KV_W:col + 2 * ATT_KV_W] = swapped.astype(o_ref.dtype)
        off += width


def _in_proj(h, g, w, riders):
    n_steps = TOKENS // ROW_TILE
    row = lambda width: pl.BlockSpec((ROW_TILE, width), lambda i: (i, 0))
    widths, dtypes = PROJ_WIDTHS, PROJ_DTYPES
    in_specs = [row(D_MODEL), _resident((1, D_MODEL)), _resident((D_MODEL, IN_WIDTH))]
    args = [h, g, w]
    out_specs = [row(wd) for wd in widths]
    out_shape = [jax.ShapeDtypeStruct((TOKENS, wd), dt) for wd, dt in zip(widths, dtypes)]
    for w2d, first_block, rows, n_blocks in riders:
        assert n_blocks <= n_steps
        r_in, r_out, r_shape = _cast_rider(w2d, first_block, rows, n_blocks,
                                           lambda i, nb=n_blocks: jnp.minimum(i, nb - 1))
        in_specs.append(r_in)
        args.append(w2d)
        out_specs.append(r_out)
        out_shape.append(r_shape)
    outs = pl.pallas_call(
        functools.partial(_in_proj_kernel, n_riders=len(riders)),
        grid=(n_steps,),
        in_specs=in_specs,
        out_specs=out_specs,
        out_shape=out_shape,
        compiler_params=_cparams(("arbitrary",)),
        name="in_proj",
    )(*args)
    return outs[:len(widths)], outs[len(widths):]


def _block_diag(a, a_sw, g):
    lane = lax.broadcasted_iota(jnp.int32, a.shape, 1)
    low = lane < ATT_HEAD_DIM
    own, other = (a, a_sw) if g == 0 else (a_sw, a)
    zero = jnp.zeros_like(a)
    first = jnp.where(low, own, zero)
    second = jnp.where(low, zero, other)
    cat = jnp.concatenate
    return cat([cat([first, zero], 1), cat([second, zero], 1),
                cat([zero, first], 1), cat([zero, second], 1)], 0)


def _attn_block(sink_ref, q_ref, kc_ref, kp_ref, kcs_ref, kps_ref, vc_ref, vp_ref, vcs_ref, vps_ref,
                bias_ref, first):
    band = 2 * ATT_BLOCK
    r = lax.broadcasted_iota(jnp.int32, (ATT_BLOCK, band), 0)
    j = lax.broadcasted_iota(jnp.int32, (ATT_BLOCK, band), 1)
    dist = r + ATT_BLOCK - j
    valid = (dist >= 0) & (dist < WINDOW)
    if first is not False:
        valid = valid & ((j >= ATT_BLOCK) | jnp.logical_not(first))
    cat = jnp.concatenate
    k, ks = cat([kp_ref[...], kc_ref[...]], 0), cat([kps_ref[...], kcs_ref[...]], 0)
    v, vs = cat([vp_ref[...], vc_ref[...]], 0), cat([vps_ref[...], vcs_ref[...]], 0)
    gw = ATT_GROUP * ATT_HEAD_DIM
    lane = lax.broadcasted_iota(jnp.int32, (ATT_BLOCK, gw), 1)
    outs = []
    for g in range(ATT_KV_HEADS):
        s_all = _dot_nt(q_ref[:, g * gw:(g + 1) * gw], _block_diag(k, ks, g))
        ps, dens = [], []
        for hh in range(ATT_GROUP):
            h = g * ATT_GROUP + hh
            s = s_all[:, hh * band:(hh + 1) * band] + bias_ref[h]
            s = jnp.where(valid, s, -1e30)
            sink = sink_ref[h]
            m = jnp.maximum(jnp.max(s, axis=-1, keepdims=True), sink)
            p = jnp.exp(s - m)
            dens.append(jnp.sum(p, axis=-1, keepdims=True) + jnp.exp(sink - m))
            ps.append(p.astype(BF16))
        o = _dot(cat(ps, 1), _block_diag(v, vs, g))
        den = dens[ATT_GROUP - 1]
        for hh in reversed(range(ATT_GROUP - 1)):
            den = jnp.where(lane < (hh + 1) * ATT_HEAD_DIM, dens[hh], den)
        outs.append(o / den)
    return cat(outs, 1)


SUBLANES = 8


def _cumsum_rows(x, t):
    within = t & (SUBLANES - 1)
    shift = 1
    while shift < SUBLANES:
        x = x + jnp.where(within >= shift, pltpu.roll(x, shift, 0), 0.0)
        shift *= 2
    tiles = []
    carry = jnp.zeros_like(x[0:1, :])
    for r in range(0, x.shape[0], SUBLANES):
        tile = x[r:r + SUBLANES, :] + carry
        tiles.append(tile)
        carry = tile[SUBLANES - 1:SUBLANES, :]
    return jnp.concatenate(tiles, 0)


def _hgrn_tile(q_ref, f_ref, v_ref, g_ref, lbl_ref, ng_ref, st_ref, layer):
    rows = [lbl_ref[i:i + 1, :] for i in range(DEPTH)]
    mx = functools.reduce(jnp.maximum, rows)
    ex = [jnp.exp(rw - mx) for rw in rows]
    tot = functools.reduce(lambda a, b: a + b, ex)
    lower = jnp.zeros_like(mx)
    for i in range(1, layer + 1):
        lower = lower + ex[i] / tot

    c = HG_TILE
    t = lax.broadcasted_iota(jnp.int32, (c, HG_DK), 0)
    ts = lax.broadcasted_iota(jnp.int32, (c, c), 0)
    ss = lax.broadcasted_iota(jnp.int32, (c, c), 1)
    same64 = (ts >> 6) == (ss >> 6)
    diag32 = ((ts >> 5) == (ss >> 5)) & (ss <= ts)
    ng = ng_ref[...]

    outs = []
    for h in range(HG_HEADS):
        sl = slice(h * HG_DK, (h + 1) * HG_DK)
        lb = lower[:, sl]
        f = lb + (1.0 - lb) * _sigmoid(f_ref[:, sl])
        k = 1.0 - f
        a = _cumsum_rows(jnp.log(f), t)
        a_last = a[c - 1:c, :]
        q = q_ref[:, sl].astype(F32)
        v = v_ref[:, sl]
        st = st_ref[h]

        o = _dot_nt((q * jnp.exp(a)).astype(BF16), st.astype(BF16))
        ke = (k * jnp.exp(a_last - a)).astype(BF16)
        vt = v.astype(F32).T.astype(BF16)
        st_ref[h] = st * jnp.exp(a_last) + _dot(vt, ke)

        zeros = jnp.zeros((HG_SUB, HG_DK), BF16)
        ops = [[] for _ in range(6)]
        for r in range(0, c, HG_SUB):
            rows = slice(r, r + HG_SUB)
            ab, qb, kb = a[rows], q[rows], k[rows]
            m1 = a[c // 2 - 1:c // 2, :]
            half = (r // (c // 2)) * (c // 2)
            m2 = a[half + c // 4 - 1:half + c // 4, :]
            m3 = a[r + HG_SUB // 2 - 1:r + HG_SUB // 2, :]
            upper1 = r >= c // 2
            upper2 = r - half >= c // 4
            ops[0].append((qb * jnp.exp(ab - m1)).astype(BF16) if upper1 else zeros)
            ops[1].append(zeros if upper1 else (kb * jnp.exp(m1 - ab)).astype(BF16))
            ops[2].append((qb * jnp.exp(ab - m2)).astype(BF16) if upper2 else zeros)
            ops[3].append(zeros if upper2 else (kb * jnp.exp(m2 - ab)).astype(BF16))
            ops[4].append((qb * jnp.exp(ab - m3)).astype(BF16))
            ops[5].append((kb * jnp.exp(m3 - ab)).astype(BF16))
        q1, k1, q2, k2, q3, k3 = (jnp.concatenate(blocks, 0) for blocks in ops)
        p = _dot_nt(q1, k1)
        p = p + jnp.where(same64, _dot_nt(q2, k2), 0.0)
        p = p + jnp.where(diag32, _dot_nt(q3, k3), 0.0)

        o = o + _dot(p.astype(BF16), v)
        o = o * lax.rsqrt(jnp.mean(o * o, axis=-1, keepdims=True) + EPS) * ng
        gate = g_ref[:, sl].astype(F32)
        outs.append(o * (gate * _sigmoid(gate)))
    return jnp.concatenate(outs, 1)


def _route(u, rwh_ref, rwl_ref):
    u_hi = u.astype(BF16)
    u_lo = (u - u_hi.astype(F32)).astype(BF16)
    logits = _dot(u_hi, rwh_ref[...]) + (_dot(u_hi, rwl_ref[...]) + _dot(u_lo, rwh_ref[...]))
    lane = lax.broadcasted_iota(jnp.int32, logits.shape, 1).astype(F32)
    none = float(logits.shape[1])
    l0 = jnp.where(lane < N_EXPERTS, logits, -jnp.inf)
    m1 = jnp.max(l0, axis=-1, keepdims=True)
    i1 = jnp.min(jnp.where(l0 == m1, lane, none), axis=-1, keepdims=True)
    l1 = jnp.where(lane == i1, -jnp.inf, l0)
    m2 = jnp.max(l1, axis=-1, keepdims=True)
    i2 = jnp.min(jnp.where(l1 == m2, lane, none), axis=-1, keepdims=True)
    e2 = jnp.exp(m2 - m1)
    den = 1.0 + e2
    idx = jnp.concatenate([i1, i2], axis=1).astype(jnp.int32)
    return idx, jnp.concatenate([1.0 / den, e2 / den], axis=1)


N_ATT_IN = 5
N_HGRN_IN = 4
N_MERGE_IN = 5


def _mixer_kernel(*refs, layer, moe):
    refs = list(refs)
    att_in = refs[:N_ATT_IN]
    hg_in = refs[N_ATT_IN:N_ATT_IN + N_HGRN_IN]
    gab_ref, h_ref, wa_ref, wb_ref, wo_ref = refs[N_ATT_IN + N_HGRN_IN:N_ATT_IN + N_HGRN_IN + N_MERGE_IN]
    rest = refs[N_ATT_IN + N_HGRN_IN + N_MERGE_IN:]
    if moe:
        (gn_ref, rwh_ref, rwl_ref, w1_ref, w3_ref, ho_ref, u_ref, idx_ref, gate_ref, w1o_ref, w3o_ref,
         st_ref) = rest
        w1o_ref[...] = w1_ref[...].astype(BF16)
        w3o_ref[...] = w3_ref[...].astype(BF16)
    else:
        ho_ref, st_ref = rest

    @pl.when(pl.program_id(1) == 0)
    def _():
        st_ref[...] = jnp.zeros_like(st_ref)

    sink_ref, q_ref, kv_ref, kvp_ref, bias_ref = att_in
    hqvg_ref, hf_ref, lbl_ref, ng_ref = hg_in
    kv_cols = [pl.ds(i * ATT_KV_W, ATT_KV_W) for i in range(4)]
    for sub in range(MIXER_BLOCKS):
        rows = pl.ds(sub * ATT_BLOCK, ATT_BLOCK)
        cur = [kv_ref.at[rows, cols] for cols in kv_cols]
        if sub == 0:
            prev = [kvp_ref.at[pl.ds(0, ATT_BLOCK), cols] for cols in kv_cols]
            first = pl.program_id(1) == 0
        else:
            before = pl.ds((sub - 1) * ATT_BLOCK, ATT_BLOCK)
            prev = [kv_ref.at[before, cols] for cols in kv_cols]
            first = False
        att = _attn_block(sink_ref, q_ref.at[rows], cur[0], prev[0], cur[1], prev[1],
                          cur[2], prev[2], cur[3], prev[3], bias_ref, first)
        hq, hi, hgt = (hqvg_ref.at[rows, pl.ds(i * HG_K_W, HG_K_W)] for i in range(3))
        hgo = _hgrn_tile(hq, hf_ref.at[rows], hi, hgt, lbl_ref, ng_ref, st_ref, layer)
        ya = _dot(att.astype(BF16), wa_ref[...])
        yb = _dot(hgo.astype(BF16), wb_ref[...])
        ga = gab_ref[rows, pl.ds(0, D_MODEL)]
        gb = gab_ref[rows, pl.ds(D_MODEL, D_MODEL)]
        merged = _sigmoid(ga).astype(F32) * ya + _sigmoid(gb).astype(F32) * yb
        hn = h_ref[rows, :] + _dot(merged.astype(BF16), wo_ref[...])
        ho_ref[rows, :] = hn
        if moe:
            u = _rms(hn, gn_ref[...])
            bits = pltpu.bitcast(u.astype(BF16).astype(F32), jnp.uint32)
            u_ref[rows, :] = (bits[:, :D_MODEL // 2] >> 16) | (bits[:, D_MODEL // 2:] & jnp.uint32(0xFFFF0000))
            idx_ref[rows, :], gate_ref[rows, :] = _route(u, rwh_ref, rwl_ref)


def _mixer(proj, h, sinks, band_bias, lb_logits, norm_g, wa, wb, wo, layer, moe_in=None):
    aq, kv, hqvg, hf, gab = proj
    moe = moe_in is not None
    nb = SEQ // MIXER_TILE
    step = lambda b, n: b * nb + n
    cur = lambda b, n: (step(b, n), 0)
    prev = lambda b, n: (jnp.maximum(step(b, n) * MIXER_BLOCKS - 1, 0), 0)
    blk = lambda width: pl.BlockSpec((MIXER_TILE, width), cur)
    in_specs = [
        pl.BlockSpec(memory_space=pltpu.SMEM), blk(PROJ_WIDTHS[PROJ_AQ]),
        blk(PROJ_WIDTHS[PROJ_KV]), pl.BlockSpec((ATT_BLOCK, PROJ_WIDTHS[PROJ_KV]), prev),
        _resident((ATT_Q_HEADS, ATT_BLOCK, 2 * ATT_BLOCK)),
        blk(PROJ_WIDTHS[PROJ_HG]), blk(PROJ_WIDTHS[PROJ_HF]), _resident((DEPTH, HG_K_W)), _resident((1, HG_DV)),
        blk(PROJ_WIDTHS[PROJ_GATES]), blk(D_MODEL),
        _resident((ATT_Q_W, D_MODEL)), _resident((HG_V_W, D_MODEL)), _resident((D_MODEL, D_MODEL)),
    ]
    args = [sinks, aq, kv, kv, band_bias, hqvg, hf, lb_logits, norm_g, gab, h, wa, wb, wo]
    assert len(in_specs) == N_ATT_IN + N_HGRN_IN + N_MERGE_IN
    out_specs = [blk(D_MODEL)]
    out_shape = [jax.ShapeDtypeStruct((TOKENS, D_MODEL), F32)]
    if moe:
        gn, rw_hi, rw_lo, w1_all, w3_all, first_block = moe_in
        in_specs += [_resident((1, D_MODEL)), _resident((D_MODEL, ROUTER_LANES)),
                     _resident((D_MODEL, ROUTER_LANES))]
        args += [gn, rw_hi, rw_lo]
        out_specs += [blk(D_MODEL // 2), blk(TOP_K), blk(TOP_K)]
        out_shape += [jax.ShapeDtypeStruct((TOKENS, D_MODEL // 2), jnp.uint32),
                      jax.ShapeDtypeStruct((TOKENS, TOP_K), jnp.int32),
                      jax.ShapeDtypeStruct((TOKENS, TOP_K), F32)]
        for w_all in (w1_all, w3_all):
            r_in, r_out, r_shape = _cast_rider(w_all, first_block, W13_ROWS, MIXER_STEPS, step)
            in_specs.append(r_in)
            args.append(w_all)
            out_specs.append(r_out)
            out_shape.append(r_shape)
    return pl.pallas_call(
        functools.partial(_mixer_kernel, layer=layer, moe=moe),
        grid=(BATCH, nb),
        in_specs=in_specs,
        out_specs=out_specs,
        out_shape=out_shape,
        scratch_shapes=[pltpu.VMEM((HG_HEADS, HG_DV, HG_DK), F32)],
        compiler_params=_cparams(("arbitrary", "arbitrary")),
        name="mixer_moe" if moe else "mixer",
    )(*args)


def _swiglu_part(x, w1, w3, w2):
    a = _dot(x, w1)
    b = _dot(x, w3)
    return _dot((a * _sigmoid(a) * b).astype(BF16), w2)


def _dense_ffn_kernel(h_ref, g_ref, w1_ref, w3_ref, w2_ref, o_ref):
    h = h_ref[...]
    x = _rms(h, g_ref[...]).astype(BF16)
    acc = None
    for lo, hi in DENSE_F_GROUPS:
        part = _swiglu_part(x, w1_ref[:, lo:hi], w3_ref[:, lo:hi], w2_ref[lo:hi, :])
        acc = part if acc is None else acc + part
    o_ref[...] = h + acc


def _dense_ffn(h, g, w1, w3, w2):
    once = lambda shape: pl.BlockSpec(shape, lambda i: (0, 0), pipeline_mode=pl.Buffered(1))
    return pl.pallas_call(
        _dense_ffn_kernel,
        grid=(TOKENS // ROW_TILE,),
        in_specs=[
            pl.BlockSpec((ROW_TILE, D_MODEL), lambda i: (i, 0)),
            _resident((1, D_MODEL)),
            once((D_MODEL, FFN_DENSE)), once((D_MODEL, FFN_DENSE)), once((FFN_DENSE, D_MODEL)),
        ],
        out_specs=pl.BlockSpec((ROW_TILE, D_MODEL), lambda i: (i, 0)),
        out_shape=jax.ShapeDtypeStruct((TOKENS, D_MODEL), F32),
        compiler_params=_cparams(("arbitrary",)),
        name="dense_ffn",
    )(h, g, w1, w3, w2)


def _dispatch_kernel(pos_ref, u_ref, w2_ref, xs_in_ref, xs_ref, w2o_ref, sem):
    del xs_in_ref
    tok = DISPATCH_TILE // DISPATCH_PARTS
    r2 = W2_ROWS // DISPATCH_PARTS
    for part in range(DISPATCH_PARTS):
        s2 = slice(part * r2, (part + 1) * r2)
        w2o_ref[s2, :] = w2_ref[s2, :].astype(BF16)
        for r in range(part * tok, (part + 1) * tok):
            for k in range(TOP_K):
                d = pos_ref[0, 0, TOP_K * r + k]
                pltpu.make_async_copy(u_ref.at[pl.ds(r, 1)], xs_ref.at[pl.ds(d, 1)], sem).start(priority=k)

    for k in range(TOP_K):
        pltpu.make_async_copy(u_ref, xs_ref.at[pl.ds(0, DISPATCH_TILE)], sem).wait()


def _dispatch(u_packed, pos, w2_all, moe_layer):
    n = DISPATCH_STEPS
    xs0 = jnp.zeros((MOE_ROWS, D_MODEL // 2), jnp.uint32)
    w2_in, w2_out, w2_shape = _cast_rider(w2_all, moe_layer * n, W2_ROWS, n, lambda i: i)
    return pl.pallas_call(
        _dispatch_kernel,
        grid=(n,),
        in_specs=[
            pl.BlockSpec((1, 1, TOP_K * DISPATCH_TILE), lambda i: (i, 0, 0), memory_space=pltpu.SMEM),
            pl.BlockSpec((DISPATCH_TILE, D_MODEL // 2), lambda i: (i, 0)),
            w2_in,
            pl.BlockSpec(memory_space=pl.ANY),
        ],
        out_specs=[pl.BlockSpec(memory_space=pl.ANY), w2_out],
        out_shape=[jax.ShapeDtypeStruct((MOE_ROWS, D_MODEL // 2), jnp.uint32), w2_shape],
        scratch_shapes=[pltpu.SemaphoreType.DMA(())],
        input_output_aliases={3: 0},
        compiler_params=_cparams(("arbitrary",)),
        name="moe_dispatch",
    )(pos.reshape(n, 1, TOP_K * DISPATCH_TILE), u_packed, w2_all, xs0)


def _moe_kernel(te_ref, nu_ref, x_ref, w1_ref, w3_ref, w2_ref, y_ref, xb, acc):
    i = pl.program_id(0)
    f = pl.program_id(1)
    valid = i < nu_ref[0]

    @pl.when((f == 0) & valid)
    def _():
        w = x_ref[...]
        x = jnp.concatenate([pltpu.bitcast(w << 16, F32).astype(BF16),
                             pltpu.bitcast(w & jnp.uint32(0xFFFF0000), F32).astype(BF16)], axis=1)
        xb[...] = x
        acc[...] = _swiglu_part(x, w1_ref[0], w3_ref[0], w2_ref[0])

    @pl.when((f == 1) & valid)
    def _():
        y_ref[...] = acc[...] + _swiglu_part(xb[...], w1_ref[0], w3_ref[0], w2_ref[0])

    @pl.when((f == 1) & jnp.logical_not(valid))
    def _():
        y_ref[...] = jnp.zeros_like(y_ref)


def _moe_experts(xs, tile_expert, n_used, w1, w3, w2):
    row_blk = lambda i, f, te, nu: (jnp.minimum(i, nu[0] - 1), 0)
    f_blk = lambda i, f, nu: jnp.where(i < nu[0], f, MOE_NF - 1)
    grid_spec = pltpu.PrefetchScalarGridSpec(
        num_scalar_prefetch=2,
        grid=(MOE_TILES, MOE_NF),
        in_specs=[
            pl.BlockSpec((MOE_ROW_TILE, D_MODEL // 2), row_blk),
            pl.BlockSpec((1, D_MODEL, MOE_F_TILE), lambda i, f, te, nu: (te[i], 0, f_blk(i, f, nu))),
            pl.BlockSpec((1, D_MODEL, MOE_F_TILE), lambda i, f, te, nu: (te[i], 0, f_blk(i, f, nu))),
            pl.BlockSpec((1, MOE_F_TILE, D_MODEL), lambda i, f, te, nu: (te[i], f_blk(i, f, nu), 0)),
        ],
        out_specs=pl.BlockSpec((MOE_ROW_TILE, D_MODEL), lambda i, f, te, nu: (i, 0)),
        scratch_shapes=[
            pltpu.VMEM((MOE_ROW_TILE, D_MODEL), BF16),
            pltpu.VMEM((MOE_ROW_TILE, D_MODEL), F32),
        ],
    )
    return pl.pallas_call(
        _moe_kernel,
        grid_spec=grid_spec,
        out_shape=jax.ShapeDtypeStruct((MOE_ROWS, D_MODEL), F32),
        compiler_params=_cparams(("arbitrary", "arbitrary")),
        name="moe_experts",
    )(tile_expert, n_used, xs, w1, w3, w2)


def _moe_plan(idx):
    e_flat = idx.reshape(-1)
    onehot = (e_flat[:, None] == jnp.arange(N_EXPERTS, dtype=jnp.int32)[None, :]).astype(jnp.int32)
    csum = jnp.cumsum(onehot, axis=0)
    counts = csum[-1]
    rank = jnp.sum((csum - onehot) * onehot, axis=1)
    tiles_per = (counts + MOE_ROW_TILE - 1) // MOE_ROW_TILE
    tile_end = jnp.cumsum(tiles_per)
    tile_start = tile_end - tiles_per
    pos = jnp.sum(onehot * tile_start[None, :], axis=1) * MOE_ROW_TILE + rank
    tile_id = jnp.arange(MOE_TILES, dtype=jnp.int32)
    te = jnp.sum((tile_id[:, None] >= tile_end[None, :]).astype(jnp.int32), axis=1)
    n_used = tile_end[-1]
    te = jnp.minimum(te, N_EXPERTS - 1)
    te = jnp.where(tile_id < n_used, te, te[jnp.maximum(n_used - 1, 0)])
    return pos.astype(jnp.int32), te.astype(jnp.int32), n_used.reshape(1).astype(jnp.int32)


def _ple_tile(h, p, g_ref, wg_ref, wp_ref, fn_ref):
    u = _rms(h, g_ref[...]).astype(BF16)
    emb_gate = _sigmoid(_dot(u, wg_ref[...]))
    h = h + emb_gate * _dot(p.astype(BF16), wp_ref[...])
    if fn_ref is not None:
        h = _rms(h, fn_ref[...])
    return h


def _ple_kernel(h_ref, p_ref, g_ref, wg_ref, wp_ref, *rest, final):
    fn_ref, o_ref = rest if final else (None,) + rest
    o_ref[...] = _ple_tile(h_ref[...], p_ref[...], g_ref, wg_ref, wp_ref, fn_ref)


def _ple_moe_kernel(pos_ref, posn_ref, gate_ref, y_hbm, h_ref, p_ref, g_ref, wg_ref, wp_ref, *rest, final):
    if final:
        fn_ref, o_ref, yg, sems = rest
    else:
        fn_ref, (o_ref, yg, sems) = None, rest
    j = pl.program_id(0)

    def start_row(idx_ref, slot, r):
        for k in range(TOP_K):
            s = idx_ref[0, slot, TOP_K * r + k]
            pltpu.make_async_copy(y_hbm.at[pl.ds(s, 1)], yg.at[slot, k, pl.ds(r, 1)],
                                  sems.at[slot]).start(priority=k)

    def wait_rows(slot):
        for k in range(TOP_K):
            pltpu.make_async_copy(y_hbm.at[pl.ds(0, ROW_TILE)], yg.at[slot, k], sems.at[slot]).wait()

    @pl.when(j == 0)
    def _():
        for slot in range(2):
            def body(r, carry, slot=slot):
                start_row(pos_ref, slot, r)
                return carry
            lax.fori_loop(0, ROW_TILE, body, 0, unroll=8)

    for slot in range(2):
        rows = pl.ds(slot * ROW_TILE, ROW_TILE)
        wait_rows(slot)
        h = h_ref[rows, :]
        gate = gate_ref[rows, :]
        for k in range(TOP_K):
            h = h + gate[:, k:k + 1] * yg[slot, k]
        o_ref[rows, :] = _ple_tile(h, p_ref[rows, :], g_ref, wg_ref, wp_ref, fn_ref)
        for r in range(ROW_TILE):
            start_row(posn_ref, slot, r)

    @pl.when(j == pl.num_programs(0) - 1)
    def _():
        for slot in range(2):
            wait_rows(slot)


def _ple(h, moe_in, p, g, wg, wp, final_g):
    moe = moe_in is not None
    final = final_g is not None
    tile = 2 * ROW_TILE if moe else ROW_TILE
    n = TOKENS // tile
    row = lambda width: pl.BlockSpec((tile, width), lambda i: (i, 0))
    in_specs, args, scratch = [], [], []
    if moe:
        pos, gate, y = moe_in
        pos3 = pos.reshape(n, 2, TOP_K * ROW_TILE)
        smem = lambda imap: pl.BlockSpec((1, 2, TOP_K * ROW_TILE), imap, memory_space=pltpu.SMEM)
        in_specs += [smem(lambda i: (i, 0, 0)), smem(lambda i: (jnp.minimum(i + 1, n - 1), 0, 0)),
                     row(TOP_K), pl.BlockSpec(memory_space=pl.ANY)]
        args += [pos3, pos3, gate, y]
        scratch = [pltpu.VMEM((2, TOP_K, ROW_TILE, D_MODEL), F32), pltpu.SemaphoreType.DMA((2,))]
    in_specs += [row(D_MODEL), row(PLE_DIM), _resident((1, D_MODEL)), _resident((D_MODEL, D_MODEL)),
                 _resident((PLE_DIM, D_MODEL))]
    args += [h, p, g, wg, wp]
    if final:
        in_specs.append(_resident((1, D_MODEL)))
        args.append(final_g)
    return pl.pallas_call(
        functools.partial(_ple_moe_kernel if moe else _ple_kernel, final=final),
        grid=(n,),
        in_specs=in_specs,
        out_specs=row(D_MODEL),
        out_shape=jax.ShapeDtypeStruct((TOKENS, D_MODEL), F32),
        scratch_shapes=scratch,
        compiler_params=_cparams(("arbitrary",)),
        name="ple_moe" if moe else "ple",
    )(*args)


def _t5_bucket(dist):
    max_exact = REL_BUCKETS // 2
    d = jnp.maximum(dist, 0)
    large = max_exact + (jnp.log(jnp.maximum(d, 1).astype(jnp.float32) / max_exact)
                         / math.log(REL_MAX_DIST / max_exact)
                         * (REL_BUCKETS - max_exact)).astype(jnp.int32)
    large = jnp.minimum(large, REL_BUCKETS - 1)
    return jnp.where(d < max_exact, d, large)


def kernel(x, p, w_in, sinks, rel_bias, lb_logits, hgrn_norm, w_branch_a, w_branch_b, w_out,
           norm_mix, norm_ffn, norm_ple, dense_w1, dense_w3, dense_w2, router_w, moe_w1,
           moe_w3, moe_w2, ple_proj, ple_gate, final_norm):
    qi = jnp.arange(ATT_BLOCK)[:, None]
    kj = jnp.arange(2 * ATT_BLOCK)[None, :]
    bucket = _t5_bucket(qi + ATT_BLOCK - kj)[None]
    band_bias = jnp.zeros((ATT_Q_HEADS, ATT_BLOCK, 2 * ATT_BLOCK), F32)
    for b in range(REL_BUCKETS):
        band_bias = jnp.where(bucket == b, rel_bias[b].astype(F32)[:, None, None], band_bias)

    bf = lambda w: w.astype(BF16)
    h = x.reshape(TOKENS, D_MODEL)
    pt = p.reshape(DEPTH, TOKENS, PLE_DIM)
    vec = lambda g: g.reshape(1, -1)
    n_moe = moe_w1.shape[0]
    w1_all = moe_w1.reshape(n_moe * N_EXPERTS * D_MODEL, FFN_EXPERT)
    w3_all = moe_w3.reshape(n_moe * N_EXPERTS * D_MODEL, FFN_EXPERT)
    w2_all = moe_w2.reshape(n_moe * N_EXPERTS * FFN_EXPERT, D_MODEL)

    def stacked(w):
        return w.reshape(w.shape[0] * w.shape[1], w.shape[2])

    def layer_rider(w, index, rows):
        blocks = w.shape[1] // rows
        return stacked(w), index * blocks, rows, blocks

    w_in_next = bf(w_in[0])
    for l in range(DEPTH):
        moe = l % 2 == 1
        riders = [layer_rider(w_branch_a, l, 16), layer_rider(w_branch_b, l, 16),
                  layer_rider(w_out, l, 32), layer_rider(ple_gate, l, 32)]
        if not moe:
            riders += [layer_rider(dense_w1, l // 2, 32), layer_rider(dense_w3, l // 2, 32),
                       layer_rider(dense_w2, l // 2, 176)]
        if l + 1 < DEPTH:
            riders.append(layer_rider(w_in, l + 1, 32))
        proj, cast = _in_proj(h, vec(norm_mix[l]), w_in_next, riders)
        wa, wb, wo, wpg = cast[:4]
        if l + 1 < DEPTH:
            w_in_next = cast[-1]
        final_g = vec(final_norm) if l == DEPTH - 1 else None
        if not moe:
            (h,) = _mixer(proj, h, sinks[l], band_bias, lb_logits, vec(hgrn_norm[l]), wa, wb, wo, l)
            h = _dense_ffn(h, vec(norm_ffn[l]), cast[4], cast[5], cast[6])
            y = None
        else:
            rw = jnp.pad(router_w[l // 2], ((0, 0), (0, ROUTER_LANES - N_EXPERTS)))
            rw_hi = bf(rw)
            rw_lo = bf(rw - rw_hi.astype(F32))
            moe_in = (vec(norm_ffn[l]), rw_hi, rw_lo, w1_all, w3_all, (l // 2) * MIXER_STEPS)
            h, u, idx, gate, w1b, w3b = _mixer(proj, h, sinks[l], band_bias, lb_logits, vec(hgrn_norm[l]),
                                               wa, wb, wo, l, moe_in)
            pos, tile_expert, n_used = _moe_plan(idx)
            xs, w2b = _dispatch(u, pos, w2_all, l // 2)
            y = _moe_experts(xs, tile_expert, n_used,
                             w1b.reshape(N_EXPERTS, D_MODEL, FFN_EXPERT),
                             w3b.reshape(N_EXPERTS, D_MODEL, FFN_EXPERT),
                             w2b.reshape(N_EXPERTS, FFN_EXPERT, D_MODEL))
            y = (pos, gate, y)
        h = _ple(h, y, pt[l], vec(norm_ple[l]), wpg, bf(ple_proj[l]), final_g)
    return h.reshape(BATCH, SEQ, D_MODEL)
```

```python
import functools
import math

import jax
import jax.numpy as jnp
from jax import lax
from jax.experimental import pallas as pl
from jax.experimental.pallas import tpu as pltpu

F32 = jnp.float32
BF16 = jnp.bfloat16

D_MODEL = 1024
BATCH = 4
SEQ = 4096
TOKENS = BATCH * SEQ
DEPTH = 4
ATT_Q_HEADS = 8
ATT_KV_HEADS = 2
ATT_HEAD_DIM = 64
ATT_GROUP = ATT_Q_HEADS // ATT_KV_HEADS
WINDOW = 128
ATT_BLOCK = 128
REL_BUCKETS = 32
REL_MAX_DIST = 128
HG_HEADS = 4
HG_DK = 128
HG_DV = 128
ATT_Q_W = ATT_Q_HEADS * ATT_HEAD_DIM
ATT_KV_W = ATT_KV_HEADS * ATT_HEAD_DIM
HG_K_W = HG_HEADS * HG_DK
HG_V_W = HG_HEADS * HG_DV
IN_SPLITS = (ATT_Q_W, ATT_KV_W, ATT_KV_W, HG_K_W, HG_K_W, HG_V_W, HG_V_W, D_MODEL, D_MODEL)
IN_WIDTH = sum(IN_SPLITS)
FFN_DENSE = 2816
N_EXPERTS = 8
TOP_K = 2
FFN_EXPERT = 3584
PLE_DIM = 256
EPS = 1e-6

V7X_VMEM_LIMIT_BYTES = 52 * 1024 * 1024
ROW_TILE = 512
HG_TILE = 128
HG_SUB = 32
MXU_WIDTH = 256
DENSE_F_GROUPS = ((0, 6 * MXU_WIDTH), (6 * MXU_WIDTH, FFN_DENSE))
assert FFN_DENSE % MXU_WIDTH == 0
MOE_ROW_TILE = 512
DISPATCH_TILE = 1024
DISPATCH_STEPS = TOKENS // DISPATCH_TILE
DISPATCH_PARTS = 16
assert HG_TILE == ATT_BLOCK
MIXER_BLOCKS = 2
MIXER_TILE = MIXER_BLOCKS * ATT_BLOCK
MIXER_STEPS = TOKENS // MIXER_TILE
W13_ROWS = N_EXPERTS * D_MODEL // MIXER_STEPS
W2_ROWS = N_EXPERTS * FFN_EXPERT // DISPATCH_STEPS
MOE_NF = 2
MOE_F_TILE = FFN_EXPERT // MOE_NF
MOE_TILES = (TOKENS * TOP_K) // MOE_ROW_TILE + N_EXPERTS - 1
MOE_ROWS = MOE_TILES * MOE_ROW_TILE
COL_CHUNK = 512
ROUTER_LANES = 128


def _cparams(sem):
    return pltpu.CompilerParams(dimension_semantics=sem, vmem_limit_bytes=V7X_VMEM_LIMIT_BYTES)


def _rms(x, g):
    return x * lax.rsqrt(jnp.mean(x * x, axis=-1, keepdims=True) + EPS) * g


def _sigmoid(x):
    return 1.0 / (1.0 + jnp.exp(-x))


def _dot(a, b):
    return jnp.dot(a, b, preferred_element_type=F32)


def _dot_nt(a, b):
    return lax.dot_general(a, b, (((1,), (1,)), ((), ())), preferred_element_type=F32)


def _resident(shape):
    nd = len(shape)
    return pl.BlockSpec(shape, lambda *_: (0,) * nd)


def _cast_rider(w2d, first_block, rows, n_steps, step_of):
    cols = w2d.shape[1]
    in_spec = pl.BlockSpec((rows, cols), lambda *ids: (first_block + step_of(*ids), 0))
    out_spec = pl.BlockSpec((rows, cols), lambda *ids: (step_of(*ids), 0))
    return in_spec, out_spec, jax.ShapeDtypeStruct((rows * n_steps, cols), BF16)


PROJ_WIDTHS = (ATT_Q_W, 4 * ATT_KV_W, 3 * HG_K_W, HG_K_W, 2 * D_MODEL)
PROJ_DTYPES = (BF16, BF16, BF16, F32, BF16)
PROJ_AQ, PROJ_KV, PROJ_HG, PROJ_HF, PROJ_GATES = range(5)
PROJ_PLACE = ((PROJ_AQ, 0), (PROJ_KV, 0), (PROJ_KV, 2 * ATT_KV_W), (PROJ_HG, 0), (PROJ_HF, 0),
              (PROJ_HG, HG_K_W), (PROJ_HG, 2 * HG_K_W), (PROJ_GATES, 0), (PROJ_GATES, D_MODEL))
ATT_SCALE = ATT_HEAD_DIM ** -0.5


def _in_proj_kernel(*refs, n_riders):
    h_ref, g_ref, w_ref = refs[:3]
    rider_in = refs[3:3 + n_riders]
    out_refs = refs[3 + n_riders:len(refs) - n_riders]
    rider_out = refs[len(refs) - n_riders:]
    for wi_ref, wo_ref in zip(rider_in, rider_out):
        wo_ref[...] = wi_ref[...].astype(BF16)
    xn = _rms(h_ref[...], g_ref[...]).astype(BF16)
    off = 0
    kv_done = False
    for idx, width in enumerate(IN_SPLITS):
        which, col = PROJ_PLACE[idx]
        o_ref = out_refs[which]
        if which == PROJ_KV:
            if not kv_done:
                kv_done = True
                z2 = _dot(xn, w_ref[:, off:off + 2 * ATT_KV_W])
                for part in range(2):
                    z = z2[:, part * ATT_KV_W:(part + 1) * ATT_KV_W]
                    base = 2 * part * ATT_KV_W
                    o_ref[:, base:base + ATT_KV_W] = z.astype(o_ref.dtype)
                    swapped = jnp.concatenate([z[:, ATT_HEAD_DIM:], z[:, :ATT_HEAD_DIM]], axis=1)
                    o_ref[:, base + ATT_KV_W:base + 2 * ATT_KV_W] = swapped.astype(o_ref.dtype)
            off += width
            continue
        for c in range(0, width, COL_CHUNK):
            cw = min(COL_CHUNK, width - c)
            z = _dot(xn, w_ref[:, off + c:off + c + cw])
            if which == PROJ_AQ:
                z = z * ATT_SCALE
            o_ref[:, col + c:col + c + cw] = z.astype(o_ref.dtype)
        off += width


def _in_proj(h, g, w, riders):
    n_steps = TOKENS // ROW_TILE
    row = lambda width: pl.BlockSpec((ROW_TILE, width), lambda i: (i, 0))
    widths, dtypes = PROJ_WIDTHS, PROJ_DTYPES
    in_specs = [row(D_MODEL), _resident((1, D_MODEL)), _resident((D_MODEL, IN_WIDTH))]
    args = [h, g, w]
    out_specs = [row(wd) for wd in widths]
    out_shape = [jax.ShapeDtypeStruct((TOKENS, wd), dt) for wd, dt in zip(widths, dtypes)]
    for w2d, first_block, rows, n_blocks in riders:
        assert n_blocks <= n_steps
        r_in, r_out, r_shape = _cast_rider(w2d, first_block, rows, n_blocks,
                                           lambda i, nb=n_blocks: jnp.minimum(i, nb - 1))
        in_specs.append(r_in)
        args.append(w2d)
        out_specs.append(r_out)
        out_shape.append(r_shape)
    outs = pl.pallas_call(
        functools.partial(_in_proj_kernel, n_riders=len(riders)),
        grid=(n_steps,),
        in_specs=in_specs,
        out_specs=out_specs,
        out_shape=out_shape,
        compiler_params=_cparams(("arbitrary",)),
        name="in_proj",
    )(*args)
    return outs[:len(widths)], outs[len(widths):]


def _block_diag(a, a_sw, g):
    lane = lax.broadcasted_iota(jnp.int32, a.shape, 1)
    low = lane < ATT_HEAD_DIM
    own, other = (a, a_sw) if g == 0 else (a_sw, a)
    zero = jnp.zeros_like(a)
    first = jnp.where(low, own, zero)
    second = jnp.where(low, zero, other)
    cat = jnp.concatenate
    return cat([cat([first, zero], 1), cat([second, zero], 1),
                cat([zero, first], 1), cat([zero, second], 1)], 0)


def _attn_block(sink_ref, q_ref, kc_ref, kp_ref, kcs_ref, kps_ref, vc_ref, vp_ref, vcs_ref, vps_ref,
                bias_ref, first):
    band = 2 * ATT_BLOCK
    r = lax.broadcasted_iota(jnp.int32, (ATT_BLOCK, band), 0)
    j = lax.broadcasted_iota(jnp.int32, (ATT_BLOCK, band), 1)
    dist = r + ATT_BLOCK - j
    valid = (dist >= 0) & (dist < WINDOW)
    if first is not False:
        valid = valid & ((j >= ATT_BLOCK) | jnp.logical_not(first))
    cat = jnp.concatenate
    k, ks = cat([kp_ref[...], kc_ref[...]], 0), cat([kps_ref[...], kcs_ref[...]], 0)
    v, vs = cat([vp_ref[...], vc_ref[...]], 0), cat([vps_ref[...], vcs_ref[...]], 0)
    gw = ATT_GROUP * ATT_HEAD_DIM
    lane = lax.broadcasted_iota(jnp.int32, (ATT_BLOCK, gw), 1)
    outs = []
    for g in range(ATT_KV_HEADS):
        s_all = _dot_nt(q_ref[:, g * gw:(g + 1) * gw], _block_diag(k, ks, g))
        ps, dens = [], []
        for hh in range(ATT_GROUP):
            h = g * ATT_GROUP + hh
            s = s_all[:, hh * band:(hh + 1) * band] + bias_ref[h]
            s = jnp.where(valid, s, -1e30)
            sink = sink_ref[h]
            m = jnp.maximum(jnp.max(s, axis=-1, keepdims=True), sink)
            p = jnp.exp(s - m)
            dens.append(jnp.sum(p, axis=-1, keepdims=True) + jnp.exp(sink - m))
            ps.append(p.astype(BF16))
        o = _dot(cat(ps, 1), _block_diag(v, vs, g))
        den = dens[ATT_GROUP - 1]
        for hh in reversed(range(ATT_GROUP - 1)):
            den = jnp.where(lane < (hh + 1) * ATT_HEAD_DIM, dens[hh], den)
        outs.append(o / den)
    return cat(outs, 1)


SUBLANES = 8


def _cumsum_rows(x, t):
    within = t & (SUBLANES - 1)
    shift = 1
    while shift < SUBLANES:
        x = x + jnp.where(within >= shift, pltpu.roll(x, shift, 0), 0.0)
        shift *= 2
    tiles = []
    carry = jnp.zeros_like(x[0:1, :])
    for r in range(0, x.shape[0], SUBLANES):
        tile = x[r:r + SUBLANES, :] + carry
        tiles.append(tile)
        carry = tile[SUBLANES - 1:SUBLANES, :]
    return jnp.concatenate(tiles, 0)


def _hgrn_tile(q_ref, f_ref, v_ref, g_ref, lbl_ref, ng_ref, st_ref, layer):
    rows = [lbl_ref[i:i + 1, :] for i in range(DEPTH)]
    mx = functools.reduce(jnp.maximum, rows)
    ex = [jnp.exp(rw - mx) for rw in rows]
    tot = functools.reduce(lambda a, b: a + b, ex)
    lower = jnp.zeros_like(mx)
    for i in range(1, layer + 1):
        lower = lower + ex[i] / tot

    c = HG_TILE
    t = lax.broadcasted_iota(jnp.int32, (c, HG_DK), 0)
    ts = lax.broadcasted_iota(jnp.int32, (c, c), 0)
    ss = lax.broadcasted_iota(jnp.int32, (c, c), 1)
    same64 = (ts >> 6) == (ss >> 6)
    diag32 = ((ts >> 5) == (ss >> 5)) & (ss <= ts)
    ng = ng_ref[...]

    outs = []
    for h in range(HG_HEADS):
        sl = slice(h * HG_DK, (h + 1) * HG_DK)
        lb = lower[:, sl]
        f = lb + (1.0 - lb) * _sigmoid(f_ref[:, sl])
        k = 1.0 - f
        a = _cumsum_rows(jnp.log(f), t)
        a_last = a[c - 1:c, :]
        q = q_ref[:, sl].astype(F32)
        v = v_ref[:, sl]
        st = st_ref[h]

        o = _dot_nt((q * jnp.exp(a)).astype(BF16), st.astype(BF16))
        ke = (k * jnp.exp(a_last - a)).astype(BF16)
        vt = v.astype(F32).T.astype(BF16)
        st_ref[h] = st * jnp.exp(a_last) + _dot(vt, ke)

        zeros = jnp.zeros((HG_SUB, HG_DK), BF16)
        ops = [[] for _ in range(6)]
        for r in range(0, c, HG_SUB):
            rows = slice(r, r + HG_SUB)
            ab, qb, kb = a[rows], q[rows], k[rows]
            m1 = a[c // 2 - 1:c // 2, :]
            half = (r // (c // 2)) * (c // 2)
            m2 = a[half + c // 4 - 1:half + c // 4, :]
            m3 = a[r + HG_SUB // 2 - 1:r + HG_SUB // 2, :]
            upper1 = r >= c // 2
            upper2 = r - half >= c // 4
            ops[0].append((qb * jnp.exp(ab - m1)).astype(BF16) if upper1 else zeros)
            ops[1].append(zeros if upper1 else (kb * jnp.exp(m1 - ab)).astype(BF16))
            ops[2].append((qb * jnp.exp(ab - m2)).astype(BF16) if upper2 else zeros)
            ops[3].append(zeros if upper2 else (kb * jnp.exp(m2 - ab)).astype(BF16))
            ops[4].append((qb * jnp.exp(ab - m3)).astype(BF16))
            ops[5].append((kb * jnp.exp(m3 - ab)).astype(BF16))
        q1, k1, q2, k2, q3, k3 = (jnp.concatenate(blocks, 0) for blocks in ops)
        p = _dot_nt(q1, k1)
        p = p + jnp.where(same64, _dot_nt(q2, k2), 0.0)
        p = p + jnp.where(diag32, _dot_nt(q3, k3), 0.0)

        o = o + _dot(p.astype(BF16), v)
        o = o * lax.rsqrt(jnp.mean(o * o, axis=-1, keepdims=True) + EPS) * ng
        gate = g_ref[:, sl].astype(F32)
        outs.append(o * (gate * _sigmoid(gate)))
    return jnp.concatenate(outs, 1)


def _route(u, rwh_ref, rwl_ref):
    u_hi = u.astype(BF16)
    u_lo = (u - u_hi.astype(F32)).astype(BF16)
    logits = _dot(u_hi, rwh_ref[...]) + (_dot(u_hi, rwl_ref[...]) + _dot(u_lo, rwh_ref[...]))
    lane = lax.broadcasted_iota(jnp.int32, logits.shape, 1).astype(F32)
    none = float(logits.shape[1])
    l0 = jnp.where(lane < N_EXPERTS, logits, -jnp.inf)
    m1 = jnp.max(l0, axis=-1, keepdims=True)
    i1 = jnp.min(jnp.where(l0 == m1, lane, none), axis=-1, keepdims=True)
    l1 = jnp.where(lane == i1, -jnp.inf, l0)
    m2 = jnp.max(l1, axis=-1, keepdims=True)
    i2 = jnp.min(jnp.where(l1 == m2, lane, none), axis=-1, keepdims=True)
    e2 = jnp.exp(m2 - m1)
    den = 1.0 + e2
    idx = jnp.concatenate([i1, i2], axis=1).astype(jnp.int32)
    return idx, jnp.concatenate([1.0 / den, e2 / den], axis=1)


N_ATT_IN = 5
N_HGRN_IN = 4
N_MERGE_IN = 5


def _mixer_kernel(*refs, layer, moe):
    refs = list(refs)
    att_in = refs[:N_ATT_IN]
    hg_in = refs[N_ATT_IN:N_ATT_IN + N_HGRN_IN]
    gab_ref, h_ref, wa_ref, wb_ref, wo_ref = refs[N_ATT_IN + N_HGRN_IN:N_ATT_IN + N_HGRN_IN + N_MERGE_IN]
    rest = refs[N_ATT_IN + N_HGRN_IN + N_MERGE_IN:]
    if moe:
        (gn_ref, rwh_ref, rwl_ref, w1_ref, w3_ref, ho_ref, u_ref, idx_ref, gate_ref, w1o_ref, w3o_ref,
         st_ref) = rest
        w1o_ref[...] = w1_ref[...].astype(BF16)
        w3o_ref[...] = w3_ref[...].astype(BF16)
    else:
        ho_ref, st_ref = rest

    @pl.when(pl.program_id(1) == 0)
    def _():
        st_ref[...] = jnp.zeros_like(st_ref)

    sink_ref, q_ref, kv_ref, kvp_ref, bias_ref = att_in
    hqvg_ref, hf_ref, lbl_ref, ng_ref = hg_in
    kv_cols = [pl.ds(i * ATT_KV_W, ATT_KV_W) for i in range(4)]
    for sub in range(MIXER_BLOCKS):
        rows = pl.ds(sub * ATT_BLOCK, ATT_BLOCK)
        cur = [kv_ref.at[rows, cols] for cols in kv_cols]
        if sub == 0:
            prev = [kvp_ref.at[pl.ds(0, ATT_BLOCK), cols] for cols in kv_cols]
            first = pl.program_id(1) == 0
        else:
            before = pl.ds((sub - 1) * ATT_BLOCK, ATT_BLOCK)
            prev = [kv_ref.at[before, cols] for cols in kv_cols]
            first = False
        att = _attn_block(sink_ref, q_ref.at[rows], cur[0], prev[0], cur[1], prev[1],
                          cur[2], prev[2], cur[3], prev[3], bias_ref, first)
        hq, hi, hgt = (hqvg_ref.at[rows, pl.ds(i * HG_K_W, HG_K_W)] for i in range(3))
        hgo = _hgrn_tile(hq, hf_ref.at[rows], hi, hgt, lbl_ref, ng_ref, st_ref, layer)
        ya = _dot(att.astype(BF16), wa_ref[...])
        yb = _dot(hgo.astype(BF16), wb_ref[...])
        ga = gab_ref[rows, pl.ds(0, D_MODEL)]
        gb = gab_ref[rows, pl.ds(D_MODEL, D_MODEL)]
        merged = _sigmoid(ga).astype(F32) * ya + _sigmoid(gb).astype(F32) * yb
        hn = h_ref[rows, :] + _dot(merged.astype(BF16), wo_ref[...])
        ho_ref[rows, :] = hn
        if moe:
            u = _rms(hn, gn_ref[...])
            bits = pltpu.bitcast(u.astype(BF16).astype(F32), jnp.uint32)
            u_ref[rows, :] = (bits[:, :D_MODEL // 2] >> 16) | (bits[:, D_MODEL // 2:] & jnp.uint32(0xFFFF0000))
            idx_ref[rows, :], gate_ref[rows, :] = _route(u, rwh_ref, rwl_ref)


def _mixer(proj, h, sinks, band_bias, lb_logits, norm_g, wa, wb, wo, layer, moe_in=None):
    aq, kv, hqvg, hf, gab = proj
    moe = moe_in is not None
    nb = SEQ // MIXER_TILE
    step = lambda b, n: b * nb + n
    cur = lambda b, n: (step(b, n), 0)
    prev = lambda b, n: (jnp.maximum(step(b, n) * MIXER_BLOCKS - 1, 0), 0)
    blk = lambda width: pl.BlockSpec((MIXER_TILE, width), cur)
    in_specs = [
        pl.BlockSpec(memory_space=pltpu.SMEM), blk(PROJ_WIDTHS[PROJ_AQ]),
        blk(PROJ_WIDTHS[PROJ_KV]), pl.BlockSpec((ATT_BLOCK, PROJ_WIDTHS[PROJ_KV]), prev),
        _resident((ATT_Q_HEADS, ATT_BLOCK, 2 * ATT_BLOCK)),
        blk(PROJ_WIDTHS[PROJ_HG]), blk(PROJ_WIDTHS[PROJ_HF]), _resident((DEPTH, HG_K_W)), _resident((1, HG_DV)),
        blk(PROJ_WIDTHS[PROJ_GATES]), blk(D_MODEL),
        _resident((ATT_Q_W, D_MODEL)), _resident((HG_V_W, D_MODEL)), _resident((D_MODEL, D_MODEL)),
    ]
    args = [sinks, aq, kv, kv, band_bias, hqvg, hf, lb_logits, norm_g, gab, h, wa, wb, wo]
    assert len(in_specs) == N_ATT_IN + N_HGRN_IN + N_MERGE_IN
    out_specs = [blk(D_MODEL)]
    out_shape = [jax.ShapeDtypeStruct((TOKENS, D_MODEL), F32)]
    if moe:
        gn, rw_hi, rw_lo, w1_all, w3_all, first_block = moe_in
        in_specs += [_resident((1, D_MODEL)), _resident((D_MODEL, ROUTER_LANES)),
                     _resident((D_MODEL, ROUTER_LANES))]
        args += [gn, rw_hi, rw_lo]
        out_specs += [blk(D_MODEL // 2), blk(TOP_K), blk(TOP_K)]
        out_shape += [jax.ShapeDtypeStruct((TOKENS, D_MODEL // 2), jnp.uint32),
                      jax.ShapeDtypeStruct((TOKENS, TOP_K), jnp.int32),
                      jax.ShapeDtypeStruct((TOKENS, TOP_K), F32)]
        for w_all in (w1_all, w3_all):
            r_in, r_out, r_shape = _cast_rider(w_all, first_block, W13_ROWS, MIXER_STEPS, step)
            in_specs.append(r_in)
            args.append(w_all)
            out_specs.append(r_out)
            out_shape.append(r_shape)
    return pl.pallas_call(
        functools.partial(_mixer_kernel, layer=layer, moe=moe),
        grid=(BATCH, nb),
        in_specs=in_specs,
        out_specs=out_specs,
        out_shape=out_shape,
        scratch_shapes=[pltpu.VMEM((HG_HEADS, HG_DV, HG_DK), F32)],
        compiler_params=_cparams(("arbitrary", "arbitrary")),
        name="mixer_moe" if moe else "mixer",
    )(*args)


def _swiglu_part(x, w1, w3, w2):
    a = _dot(x, w1)
    b = _dot(x, w3)
    return _dot((a * _sigmoid(a) * b).astype(BF16), w2)


def _dense_ffn_kernel(h_ref, g_ref, w1_ref, w3_ref, w2_ref, o_ref):
    h = h_ref[...]
    x = _rms(h, g_ref[...]).astype(BF16)
    acc = None
    for lo, hi in DENSE_F_GROUPS:
        part = _swiglu_part(x, w1_ref[:, lo:hi], w3_ref[:, lo:hi], w2_ref[lo:hi, :])
        acc = part if acc is None else acc + part
    o_ref[...] = h + acc


def _dense_ffn(h, g, w1, w3, w2):
    once = lambda shape: pl.BlockSpec(shape, lambda i: (0, 0), pipeline_mode=pl.Buffered(1))
    return pl.pallas_call(
        _dense_ffn_kernel,
        grid=(TOKENS // ROW_TILE,),
        in_specs=[
            pl.BlockSpec((ROW_TILE, D_MODEL), lambda i: (i, 0)),
            _resident((1, D_MODEL)),
            once((D_MODEL, FFN_DENSE)), once((D_MODEL, FFN_DENSE)), once((FFN_DENSE, D_MODEL)),
        ],
        out_specs=pl.BlockSpec((ROW_TILE, D_MODEL), lambda i: (i, 0)),
        out_shape=jax.ShapeDtypeStruct((TOKENS, D_MODEL), F32),
        compiler_params=_cparams(("arbitrary",)),
        name="dense_ffn",
    )(h, g, w1, w3, w2)


def _dispatch_kernel(pos_ref, u_ref, w2_ref, xs_in_ref, xs_ref, w2o_ref, sem):
    del xs_in_ref
    tok = DISPATCH_TILE // DISPATCH_PARTS
    r2 = W2_ROWS // DISPATCH_PARTS
    for part in range(DISPATCH_PARTS):
        s2 = slice(part * r2, (part + 1) * r2)
        w2o_ref[s2, :] = w2_ref[s2, :].astype(BF16)
        for r in range(part * tok, (part + 1) * tok):
            for k in range(TOP_K):
                d = pos_ref[0, 0, TOP_K * r + k]
                pltpu.make_async_copy(u_ref.at[pl.ds(r, 1)], xs_ref.at[pl.ds(d, 1)], sem).start(priority=k)

    for k in range(TOP_K):
        pltpu.make_async_copy(u_ref, xs_ref.at[pl.ds(0, DISPATCH_TILE)], sem).wait()


def _dispatch(u_packed, pos, w2_all, moe_layer):
    n = DISPATCH_STEPS
    xs0 = jnp.zeros((MOE_ROWS, D_MODEL // 2), jnp.uint32)
    w2_in, w2_out, w2_shape = _cast_rider(w2_all, moe_layer * n, W2_ROWS, n, lambda i: i)
    return pl.pallas_call(
        _dispatch_kernel,
        grid=(n,),
        in_specs=[
            pl.BlockSpec((1, 1, TOP_K * DISPATCH_TILE), lambda i: (i, 0, 0), memory_space=pltpu.SMEM),
            pl.BlockSpec((DISPATCH_TILE, D_MODEL // 2), lambda i: (i, 0)),
            w2_in,
            pl.BlockSpec(memory_space=pl.ANY),
        ],
        out_specs=[pl.BlockSpec(memory_space=pl.ANY), w2_out],
        out_shape=[jax.ShapeDtypeStruct((MOE_ROWS, D_MODEL // 2), jnp.uint32), w2_shape],
        scratch_shapes=[pltpu.SemaphoreType.DMA(())],
        input_output_aliases={3: 0},
        compiler_params=_cparams(("arbitrary",)),
        name="moe_dispatch",
    )(pos.reshape(n, 1, TOP_K * DISPATCH_TILE), u_packed, w2_all, xs0)


def _moe_kernel(te_ref, nu_ref, x_ref, w1_ref, w3_ref, w2_ref, y_ref, xb, acc):
    i = pl.program_id(0)
    f = pl.program_id(1)
    valid = i < nu_ref[0]

    @pl.when((f == 0) & valid)
    def _():
        w = x_ref[...]
        x = jnp.concatenate([pltpu.bitcast(w << 16, F32).astype(BF16),
                             pltpu.bitcast(w & jnp.uint32(0xFFFF0000), F32).astype(BF16)], axis=1)
        xb[...] = x
        acc[...] = _swiglu_part(x, w1_ref[0], w3_ref[0], w2_ref[0])

    @pl.when((f == 1) & valid)
    def _():
        y_ref[...] = acc[...] + _swiglu_part(xb[...], w1_ref[0], w3_ref[0], w2_ref[0])

    @pl.when((f == 1) & jnp.logical_not(valid))
    def _():
        y_ref[...] = jnp.zeros_like(y_ref)


def _moe_experts(xs, tile_expert, n_used, w1, w3, w2):
    row_blk = lambda i, f, te, nu: (jnp.minimum(i, nu[0] - 1), 0)
    f_blk = lambda i, f, nu: jnp.where(i < nu[0], f, MOE_NF - 1)
    grid_spec = pltpu.PrefetchScalarGridSpec(
        num_scalar_prefetch=2,
        grid=(MOE_TILES, MOE_NF),
        in_specs=[
            pl.BlockSpec((MOE_ROW_TILE, D_MODEL // 2), row_blk),
            pl.BlockSpec((1, D_MODEL, MOE_F_TILE), lambda i, f, te, nu: (te[i], 0, f_blk(i, f, nu))),
            pl.BlockSpec((1, D_MODEL, MOE_F_TILE), lambda i, f, te, nu: (te[i], 0, f_blk(i, f, nu))),
            pl.BlockSpec((1, MOE_F_TILE, D_MODEL), lambda i, f, te, nu: (te[i], f_blk(i, f, nu), 0)),
        ],
        out_specs=pl.BlockSpec((MOE_ROW_TILE, D_MODEL), lambda i, f, te, nu: (i, 0)),
        scratch_shapes=[
            pltpu.VMEM((MOE_ROW_TILE, D_MODEL), BF16),
            pltpu.VMEM((MOE_ROW_TILE, D_MODEL), F32),
        ],
    )
    return pl.pallas_call(
        _moe_kernel,
        grid_spec=grid_spec,
        out_shape=jax.ShapeDtypeStruct((MOE_ROWS, D_MODEL), F32),
        compiler_params=_cparams(("arbitrary", "arbitrary")),
        name="moe_experts",
    )(tile_expert, n_used, xs, w1, w3, w2)


def _moe_plan(idx):
    e_flat = idx.reshape(-1)
    onehot = (e_flat[:, None] == jnp.arange(N_EXPERTS, dtype=jnp.int32)[None, :]).astype(jnp.int32)
    csum = jnp.cumsum(onehot, axis=0)
    counts = csum[-1]
    rank = jnp.sum((csum - onehot) * onehot, axis=1)
    tiles_per = (counts + MOE_ROW_TILE - 1) // MOE_ROW_TILE
    tile_end = jnp.cumsum(tiles_per)
    tile_start = tile_end - tiles_per
    pos = jnp.sum(onehot * tile_start[None, :], axis=1) * MOE_ROW_TILE + rank
    tile_id = jnp.arange(MOE_TILES, dtype=jnp.int32)
    te = jnp.sum((tile_id[:, None] >= tile_end[None, :]).astype(jnp.int32), axis=1)
    n_used = tile_end[-1]
    te = jnp.minimum(te, N_EXPERTS - 1)
    te = jnp.where(tile_id < n_used, te, te[jnp.maximum(n_used - 1, 0)])
    return pos.astype(jnp.int32), te.astype(jnp.int32), n_used.reshape(1).astype(jnp.int32)


def _ple_tile(h, p, g_ref, wg_ref, wp_ref, fn_ref):
    u = _rms(h, g_ref[...]).astype(BF16)
    emb_gate = _sigmoid(_dot(u, wg_ref[...]))
    h = h + emb_gate * _dot(p.astype(BF16), wp_ref[...])
    if fn_ref is not None:
        h = _rms(h, fn_ref[...])
    return h


def _ple_kernel(h_ref, p_ref, g_ref, wg_ref, wp_ref, *rest, final):
    fn_ref, o_ref = rest if final else (None,) + rest
    o_ref[...] = _ple_tile(h_ref[...], p_ref[...], g_ref, wg_ref, wp_ref, fn_ref)


def _ple_moe_kernel(pos_ref, posn_ref, gate_ref, y_hbm, h_ref, p_ref, g_ref, wg_ref, wp_ref, *rest, final):
    if final:
        fn_ref, o_ref, yg, sems = rest
    else:
        fn_ref, (o_ref, yg, sems) = None, rest
    j = pl.program_id(0)

    def start_row(idx_ref, slot, r):
        for k in range(TOP_K):
            s = idx_ref[0, slot, TOP_K * r + k]
            pltpu.make_async_copy(y_hbm.at[pl.ds(s, 1)], yg.at[slot, k, pl.ds(r, 1)],
                                  sems.at[slot]).start(priority=k)

    def wait_rows(slot):
        for k in range(TOP_K):
            pltpu.make_async_copy(y_hbm.at[pl.ds(0, ROW_TILE)], yg.at[slot, k], sems.at[slot]).wait()

    @pl.when(j == 0)
    def _():
        for slot in range(2):
            def body(r, carry, slot=slot):
                start_row(pos_ref, slot, r)
                return carry
            lax.fori_loop(0, ROW_TILE, body, 0, unroll=8)

    for slot in range(2):
        rows = pl.ds(slot * ROW_TILE, ROW_TILE)
        wait_rows(slot)
        h = h_ref[rows, :]
        gate = gate_ref[rows, :]
        for k in range(TOP_K):
            h = h + gate[:, k:k + 1] * yg[slot, k]
        o_ref[rows, :] = _ple_tile(h, p_ref[rows, :], g_ref, wg_ref, wp_ref, fn_ref)
        for r in range(ROW_TILE):
            start_row(posn_ref, slot, r)

    @pl.when(j == pl.num_programs(0) - 1)
    def _():
        for slot in range(2):
            wait_rows(slot)


def _ple(h, moe_in, p, g, wg, wp, final_g):
    moe = moe_in is not None
    final = final_g is not None
    tile = 2 * ROW_TILE if moe else ROW_TILE
    n = TOKENS // tile
    row = lambda width: pl.BlockSpec((tile, width), lambda i: (i, 0))
    in_specs, args, scratch = [], [], []
    if moe:
        pos, gate, y = moe_in
        pos3 = pos.reshape(n, 2, TOP_K * ROW_TILE)
        smem = lambda imap: pl.BlockSpec((1, 2, TOP_K * ROW_TILE), imap, memory_space=pltpu.SMEM)
        in_specs += [smem(lambda i: (i, 0, 0)), smem(lambda i: (jnp.minimum(i + 1, n - 1), 0, 0)),
                     row(TOP_K), pl.BlockSpec(memory_space=pl.ANY)]
        args += [pos3, pos3, gate, y]
        scratch = [pltpu.VMEM((2, TOP_K, ROW_TILE, D_MODEL), F32), pltpu.SemaphoreType.DMA((2,))]
    in_specs += [row(D_MODEL), row(PLE_DIM), _resident((1, D_MODEL)), _resident((D_MODEL, D_MODEL)),
                 _resident((PLE_DIM, D_MODEL))]
    args += [h, p, g, wg, wp]
    if final:
        in_specs.append(_resident((1, D_MODEL)))
        args.append(final_g)
    return pl.pallas_call(
        functools.partial(_ple_moe_kernel if moe else _ple_kernel, final=final),
        grid=(n,),
        in_specs=in_specs,
        out_specs=row(D_MODEL),
        out_shape=jax.ShapeDtypeStruct((TOKENS, D_MODEL), F32),
        scratch_shapes=scratch,
        compiler_params=_cparams(("arbitrary",)),
        name="ple_moe" if moe else "ple",
    )(*args)


def _t5_bucket(dist):
    max_exact = REL_BUCKETS // 2
    d = jnp.maximum(dist, 0)
    large = max_exact + (jnp.log(jnp.maximum(d, 1).astype(jnp.float32) / max_exact)
                         / math.log(REL_MAX_DIST / max_exact)
                         * (REL_BUCKETS - max_exact)).astype(jnp.int32)
    large = jnp.minimum(large, REL_BUCKETS - 1)
    return jnp.where(d < max_exact, d, large)


def kernel(x, p, w_in, sinks, rel_bias, lb_logits, hgrn_norm, w_branch_a, w_branch_b, w_out,
           norm_mix, norm_ffn, norm_ple, dense_w1, dense_w3, dense_w2, router_w, moe_w1,
           moe_w3, moe_w2, ple_proj, ple_gate, final_norm):
    qi = jnp.arange(ATT_BLOCK)[:, None]
    kj = jnp.arange(2 * ATT_BLOCK)[None, :]
    bucket = _t5_bucket(qi + ATT_BLOCK - kj)[None]
    band_bias = jnp.zeros((ATT_Q_HEADS, ATT_BLOCK, 2 * ATT_BLOCK), F32)
    for b in range(REL_BUCKETS):
        band_bias = jnp.where(bucket == b, rel_bias[b].astype(F32)[:, None, None], band_bias)

    bf = lambda w: w.astype(BF16)
    h = x.reshape(TOKENS, D_MODEL)
    pt = p.reshape(DEPTH, TOKENS, PLE_DIM)
    vec = lambda g: g.reshape(1, -1)
    n_moe = moe_w1.shape[0]
    w1_all = moe_w1.reshape(n_moe * N_EXPERTS * D_MODEL, FFN_EXPERT)
    w3_all = moe_w3.reshape(n_moe * N_EXPERTS * D_MODEL, FFN_EXPERT)
    w2_all = moe_w2.reshape(n_moe * N_EXPERTS * FFN_EXPERT, D_MODEL)

    def stacked(w):
        return w.reshape(w.shape[0] * w.shape[1], w.shape[2])

    def layer_rider(w, index, rows):
        blocks = w.shape[1] // rows
        return stacked(w), index * blocks, rows, blocks

    w_in_next = bf(w_in[0])
    for l in range(DEPTH):
        moe = l % 2 == 1
        riders = [layer_rider(w_branch_a, l, 16), layer_rider(w_branch_b, l, 16),
                  layer_rider(w_out, l, 32), layer_rider(ple_gate, l, 32)]
        if not moe:
            riders += [layer_rider(dense_w1, l // 2, 32), layer_rider(dense_w3, l // 2, 32),
                       layer_rider(dense_w2, l // 2, 176)]
        if l + 1 < DEPTH:
            riders.append(layer_rider(w_in, l + 1, 32))
        proj, cast = _in_proj(h, vec(norm_mix[l]), w_in_next, riders)
        wa, wb, wo, wpg = cast[:4]
        if l + 1 < DEPTH:
            w_in_next = cast[-1]
        final_g = vec(final_norm) if l == DEPTH - 1 else None
        if not moe:
            (h,) = _mixer(proj, h, sinks[l], band_bias, lb_logits, vec(hgrn_norm[l]), wa, wb, wo, l)
            h = _dense_ffn(h, vec(norm_ffn[l]), cast[4], cast[5], cast[6])
            y = None
        else:
            rw = jnp.pad(router_w[l // 2], ((0, 0), (0, ROUTER_LANES - N_EXPERTS)))
            rw_hi = bf(rw)
            rw_lo = bf(rw - rw_hi.astype(F32))
            moe_in = (vec(norm_ffn[l]), rw_hi, rw_lo, w1_all, w3_all, (l // 2) * MIXER_STEPS)
            h, u, idx, gate, w1b, w3b = _mixer(proj, h, sinks[l], band_bias, lb_logits, vec(hgrn_norm[l]),
                                               wa, wb, wo, l, moe_in)
            pos, tile_expert, n_used = _moe_plan(idx)
            xs, w2b = _dispatch(u, pos, w2_all, l // 2)
            y = _moe_experts(xs, tile_expert, n_used,
                             w1b.reshape(N_EXPERTS, D_MODEL, FFN_EXPERT),
                             w3b.reshape(N_EXPERTS, D_MODEL, FFN_EXPERT),
                             w2b.reshape(N_EXPERTS, FFN_EXPERT, D_MODEL))
            y = (pos, gate, y)
        h = _ple(h, y, pt[l], vec(norm_ple[l]), wpg, bf(ple_proj[l]), final_g)
    return h.reshape(BATCH, SEQ, D_MODEL)
```

```python
import functools
import math

import jax
import jax.numpy as jnp
from jax import lax
from jax.experimental import pallas as pl
from jax.experimental.pallas import tpu as pltpu

F32 = jnp.float32
BF16 = jnp.bfloat16

D_MODEL = 1024
BATCH = 4
SEQ = 4096
TOKENS = BATCH * SEQ
DEPTH = 4
ATT_Q_HEADS = 8
ATT_KV_HEADS = 2
ATT_HEAD_DIM = 64
ATT_GROUP = ATT_Q_HEADS // ATT_KV_HEADS
WINDOW = 128
ATT_BLOCK = 128
REL_BUCKETS = 32
REL_MAX_DIST = 128
HG_HEADS = 4
HG_DK = 128
HG_DV = 128
ATT_Q_W = ATT_Q_HEADS * ATT_HEAD_DIM
ATT_KV_W = ATT_KV_HEADS * ATT_HEAD_DIM
HG_K_W = HG_HEADS * HG_DK
HG_V_W = HG_HEADS * HG_DV
IN_SPLITS = (ATT_Q_W, ATT_KV_W, ATT_KV_W, HG_K_W, HG_K_W, HG_V_W, HG_V_W, D_MODEL, D_MODEL)
IN_WIDTH = sum(IN_SPLITS)
FFN_DENSE = 2816
N_EXPERTS = 8
TOP_K = 2
FFN_EXPERT = 3584
PLE_DIM = 256
EPS = 1e-6

V7X_VMEM_LIMIT_BYTES = 52 * 1024 * 1024
ROW_TILE = 512
HG_TILE = 128
HG_SUB = 32
MXU_WIDTH = 256
DENSE_F_GROUPS = ((0, 6 * MXU_WIDTH), (6 * MXU_WIDTH, FFN_DENSE))
assert FFN_DENSE % MXU_WIDTH == 0
MOE_ROW_TILE = 512
DISPATCH_TILE = 1024
DISPATCH_STEPS = TOKENS // DISPATCH_TILE
DISPATCH_PARTS = 16
assert HG_TILE == ATT_BLOCK
MIXER_BLOCKS = 2
MIXER_TILE = MIXER_BLOCKS * ATT_BLOCK
MIXER_STEPS = TOKENS // MIXER_TILE
W13_ROWS = N_EXPERTS * D_MODEL // MIXER_STEPS
W2_ROWS = N_EXPERTS * FFN_EXPERT // DISPATCH_STEPS
MOE_NF = 2
MOE_F_TILE = FFN_EXPERT // MOE_NF
MOE_TILES = (TOKENS * TOP_K) // MOE_ROW_TILE + N_EXPERTS - 1
MOE_ROWS = MOE_TILES * MOE_ROW_TILE
COL_CHUNK = 512
ROUTER_LANES = 128


def _cparams(sem):
    return pltpu.CompilerParams(dimension_semantics=sem, vmem_limit_bytes=V7X_VMEM_LIMIT_BYTES)


def _rms(x, g):
    return x * lax.rsqrt(jnp.mean(x * x, axis=-1, keepdims=True) + EPS) * g


def _sigmoid(x):
    return 1.0 / (1.0 + jnp.exp(-x))


def _dot(a, b):
    return jnp.dot(a, b, preferred_element_type=F32)


def _dot_nt(a, b):
    return lax.dot_general(a, b, (((1,), (1,)), ((), ())), preferred_element_type=F32)


def _resident(shape):
    nd = len(shape)
    return pl.BlockSpec(shape, lambda *_: (0,) * nd)


def _cast_rider(w2d, first_block, rows, n_steps, step_of):
    cols = w2d.shape[1]
    in_spec = pl.BlockSpec((rows, cols), lambda *ids: (first_block + step_of(*ids), 0))
    out_spec = pl.BlockSpec((rows, cols), lambda *ids: (step_of(*ids), 0))
    return in_spec, out_spec, jax.ShapeDtypeStruct((rows * n_steps, cols), BF16)


PROJ_WIDTHS = (ATT_Q_W, 4 * ATT_KV_W, 3 * HG_K_W, HG_K_W, 2 * D_MODEL)
PROJ_DTYPES = (BF16, BF16, BF16, F32, BF16)
PROJ_AQ, PROJ_KV, PROJ_HG, PROJ_HF, PROJ_GATES = range(5)
PROJ_PLACE = ((PROJ_AQ, 0), (PROJ_KV, 0), (PROJ_KV, 2 * ATT_KV_W), (PROJ_HG, 0), (PROJ_HF, 0),
              (PROJ_HG, HG_K_W), (PROJ_HG, 2 * HG_K_W), (PROJ_GATES, 0), (PROJ_GATES, D_MODEL))
ATT_SCALE = ATT_HEAD_DIM ** -0.5


def _in_proj_kernel(*refs, n_riders):
    h_ref, g_ref, w_ref = refs[:3]
    rider_in = refs[3:3 + n_riders]
    out_refs = refs[3 + n_riders:len(refs) - n_riders]
    rider_out = refs[len(refs) - n_riders:]
    for wi_ref, wo_ref in zip(rider_in, rider_out):
        wo_ref[...] = wi_ref[...].astype(BF16)
    xn = _rms(h_ref[...], g_ref[...]).astype(BF16)
    off = 0
    kv_done = False
    for idx, width in enumerate(IN_SPLITS):
        which, col = PROJ_PLACE[idx]
        o_ref = out_refs[which]
        if which == PROJ_KV:
            if not kv_done:
                kv_done = True
                z2 = _dot(xn, w_ref[:, off:off + 2 * ATT_KV_W])
                for part in range(2):
                    z = z2[:, part * ATT_KV_W:(part + 1) * ATT_KV_W]
                    base = 2 * part * ATT_KV_W
                    o_ref[:, base:base + ATT_KV_W] = z.astype(o_ref.dtype)
                    swapped = jnp.concatenate([z[:, ATT_HEAD_DIM:], z[:, :ATT_HEAD_DIM]], axis=1)
                    o_ref[:, base + ATT_KV_W:base + 2 * ATT_KV_W] = swapped.astype(o_ref.dtype)
            off += width
            continue
        for c in range(0, width, COL_CHUNK):
            cw = min(COL_CHUNK, width - c)
            z = _dot(xn, w_ref[:, off + c:off + c + cw])
            if which == PROJ_AQ:
                z = z * ATT_SCALE
            o_ref[:, col + c:col + c + cw] = z.astype(o_ref.dtype)
        off += width


def _in_proj(h, g, w, riders):
    n_steps = TOKENS // ROW_TILE
    row = lambda width: pl.BlockSpec((ROW_TILE, width), lambda i: (i, 0))
    widths, dtypes = PROJ_WIDTHS, PROJ_DTYPES
    in_specs = [row(D_MODEL), _resident((1, D_MODEL)), _resident((D_MODEL, IN_WIDTH))]
    args = [h, g, w]
    out_specs = [row(wd) for wd in widths]
    out_shape = [jax.ShapeDtypeStruct((TOKENS, wd), dt) for wd, dt in zip(widths, dtypes)]
    for w2d, first_block, rows, n_blocks in riders:
        assert n_blocks <= n_steps
        r_in, r_out, r_shape = _cast_rider(w2d, first_block, rows, n_blocks,
                                           lambda i, nb=n_blocks: jnp.minimum(i, nb - 1))
        in_specs.append(r_in)
        args.append(w2d)
        out_specs.append(r_out)
        out_shape.append(r_shape)
    outs = pl.pallas_call(
        functools.partial(_in_proj_kernel, n_riders=len(riders)),
        grid=(n_steps,),
        in_specs=in_specs,
        out_specs=out_specs,
        out_shape=out_shape,
        compiler_params=_cparams(("arbitrary",)),
        name="in_proj",
    )(*args)
    return outs[:len(widths)], outs[len(widths):]


def _block_diag(a, a_sw, g):
    lane = lax.broadcasted_iota(jnp.int32, a.shape, 1)
    low = lane < ATT_HEAD_DIM
    own, other = (a, a_sw) if g == 0 else (a_sw, a)
    zero = jnp.zeros_like(a)
    first = jnp.where(low, own, zero)
    second = jnp.where(low, zero, other)
    cat = jnp.concatenate
    return cat([cat([first, zero], 1), cat([second, zero], 1),
                cat([zero, first], 1), cat([zero, second], 1)], 0)


def _attn_block(sink_ref, q_ref, kc_ref, kp_ref, kcs_ref, kps_ref, vc_ref, vp_ref, vcs_ref, vps_ref,
                bias_ref, first):
    band = 2 * ATT_BLOCK
    r = lax.broadcasted_iota(jnp.int32, (ATT_BLOCK, band), 0)
    j = lax.broadcasted_iota(jnp.int32, (ATT_BLOCK, band), 1)
    dist = r + ATT_BLOCK - j
    valid = (dist >= 0) & (dist < WINDOW)
    if first is not False:
        valid = valid & ((j >= ATT_BLOCK) | jnp.logical_not(first))
    cat = jnp.concatenate
    k, ks = cat([kp_ref[...], kc_ref[...]], 0), cat([kps_ref[...], kcs_ref[...]], 0)
    v, vs = cat([vp_ref[...], vc_ref[...]], 0), cat([vps_ref[...], vcs_ref[...]], 0)
    gw = ATT_GROUP * ATT_HEAD_DIM
    lane = lax.broadcasted_iota(jnp.int32, (ATT_BLOCK, gw), 1)
    outs = []
    for g in range(ATT_KV_HEADS):
        s_all = _dot_nt(q_ref[:, g * gw:(g + 1) * gw], _block_diag(k, ks, g))
        ps, dens = [], []
        for hh in range(ATT_GROUP):
            h = g * ATT_GROUP + hh
            s = s_all[:, hh * band:(hh + 1) * band] + bias_ref[h]
            s = jnp.where(valid, s, -1e30)
            sink = sink_ref[h]
            m = jnp.maximum(jnp.max(s, axis=-1, keepdims=True), sink)
            p = jnp.exp(s - m)
            dens.append(jnp.sum(p, axis=-1, keepdims=True) + jnp.exp(sink - m))
            ps.append(p.astype(BF16))
        o = _dot(cat(ps, 1), _block_diag(v, vs, g))
        den = dens[ATT_GROUP - 1]
        for hh in reversed(range(ATT_GROUP - 1)):
            den = jnp.where(lane < (hh + 1) * ATT_HEAD_DIM, dens[hh], den)
        outs.append(o / den)
    return cat(outs, 1)


SUBLANES = 8


def _cumsum_rows(x, t):
    within = t & (SUBLANES - 1)
    shift = 1
    while shift < SUBLANES:
        x = x + jnp.where(within >= shift, pltpu.roll(x, shift, 0), 0.0)
        shift *= 2
    tiles = []
    carry = jnp.zeros_like(x[0:1, :])
    for r in range(0, x.shape[0], SUBLANES):
        tile = x[r:r + SUBLANES, :] + carry
        tiles.append(tile)
        carry = tile[SUBLANES - 1:SUBLANES, :]
    return jnp.concatenate(tiles, 0)


def _hgrn_tile(q_ref, f_ref, v_ref, g_ref, lbl_ref, ng_ref, st_ref, layer):
    rows = [lbl_ref[i:i + 1, :] for i in range(DEPTH)]
    mx = functools.reduce(jnp.maximum, rows)
    ex = [jnp.exp(rw - mx) for rw in rows]
    tot = functools.reduce(lambda a, b: a + b, ex)
    lower = jnp.zeros_like(mx)
    for i in range(1, layer + 1):
        lower = lower + ex[i] / tot

    c = HG_TILE
    t = lax.broadcasted_iota(jnp.int32, (c, HG_DK), 0)
    ts = lax.broadcasted_iota(jnp.int32, (c, c), 0)
    ss = lax.broadcasted_iota(jnp.int32, (c, c), 1)
    same64 = (ts >> 6) == (ss >> 6)
    diag32 = ((ts >> 5) == (ss >> 5)) & (ss <= ts)
    ng = ng_ref[...]

    outs = []
    for h in range(HG_HEADS):
        sl = slice(h * HG_DK, (h + 1) * HG_DK)
        lb = lower[:, sl]
        f = lb + (1.0 - lb) * _sigmoid(f_ref[:, sl])
        k = 1.0 - f
        a = _cumsum_rows(jnp.log(f), t)
        a_last = a[c - 1:c, :]
        q = q_ref[:, sl].astype(F32)
        v = v_ref[:, sl]
        st = st_ref[h]

        o = _dot_nt((q * jnp.exp(a)).astype(BF16), st.astype(BF16))
        ke = (k * jnp.exp(a_last - a)).astype(BF16)
        vt = v.astype(F32).T.astype(BF16)
        st_ref[h] = st * jnp.exp(a_last) + _dot(vt, ke)

        zeros = jnp.zeros((HG_SUB, HG_DK), BF16)
        ops = [[] for _ in range(6)]
        for r in range(0, c, HG_SUB):
            rows = slice(r, r + HG_SUB)
            ab, qb, kb = a[rows], q[rows], k[rows]
            m1 = a[c // 2 - 1:c // 2, :]
            half = (r // (c // 2)) * (c // 2)
            m2 = a[half + c // 4 - 1:half + c // 4, :]
            m3 = a[r + HG_SUB // 2 - 1:r + HG_SUB // 2, :]
            upper1 = r >= c // 2
            upper2 = r - half >= c // 4
            ops[0].append((qb * jnp.exp(ab - m1)).astype(BF16) if upper1 else zeros)
            ops[1].append(zeros if upper1 else (kb * jnp.exp(m1 - ab)).astype(BF16))
            ops[2].append((qb * jnp.exp(ab - m2)).astype(BF16) if upper2 else zeros)
            ops[3].append(zeros if upper2 else (kb * jnp.exp(m2 - ab)).astype(BF16))
            ops[4].append((qb * jnp.exp(ab - m3)).astype(BF16))
            ops[5].append((kb * jnp.exp(m3 - ab)).astype(BF16))
        q1, k1, q2, k2, q3, k3 = (jnp.concatenate(blocks, 0) for blocks in ops)
        p = _dot_nt(q1, k1)
        p = p + jnp.where(same64, _dot_nt(q2, k2), 0.0)
        p = p + jnp.where(diag32, _dot_nt(q3, k3), 0.0)

        o = o + _dot(p.astype(BF16), v)
        o = o * lax.rsqrt(jnp.mean(o * o, axis=-1, keepdims=True) + EPS) * ng
        gate = g_ref[:, sl].astype(F32)
        outs.append(o * (gate * _sigmoid(gate)))
    return jnp.concatenate(outs, 1)


def _route(u, rwh_ref, rwl_ref):
    u_hi = u.astype(BF16)
    u_lo = (u - u_hi.astype(F32)).astype(BF16)
    logits = _dot(u_hi, rwh_ref[...]) + (_dot(u_hi, rwl_ref[...]) + _dot(u_lo, rwh_ref[...]))
    lane = lax.broadcasted_iota(jnp.int32, logits.shape, 1).astype(F32)
    none = float(logits.shape[1])
    l0 = jnp.where(lane < N_EXPERTS, logits, -jnp.inf)
    m1 = jnp.max(l0, axis=-1, keepdims=True)
    i1 = jnp.min(jnp.where(l0 == m1, lane, none), axis=-1, keepdims=True)
    l1 = jnp.where(lane == i1, -jnp.inf, l0)
    m2 = jnp.max(l1, axis=-1, keepdims=True)
    i2 = jnp.min(jnp.where(l1 == m2, lane, none), axis=-1, keepdims=True)
    e2 = jnp.exp(m2 - m1)
    den = 1.0 + e2
    idx = jnp.concatenate([i1, i2], axis=1).astype(jnp.int32)
    return idx, jnp.concatenate([1.0 / den, e2 / den], axis=1)


N_ATT_IN = 5
N_HGRN_IN = 4
N_MERGE_IN = 5


def _mixer_kernel(*refs, layer, moe):
    refs = list(refs)
    att_in = refs[:N_ATT_IN]
    hg_in = refs[N_ATT_IN:N_ATT_IN + N_HGRN_IN]
    gab_ref, h_ref, wa_ref, wb_ref, wo_ref = refs[N_ATT_IN + N_HGRN_IN:N_ATT_IN + N_HGRN_IN + N_MERGE_IN]
    rest = refs[N_ATT_IN + N_HGRN_IN + N_MERGE_IN:]
    if moe:
        (gn_ref, rwh_ref, rwl_ref, w1_ref, w3_ref, ho_ref, u_ref, idx_ref, gate_ref, w1o_ref, w3o_ref,
         st_ref) = rest
        w1o_ref[...] = w1_ref[...].astype(BF16)
        w3o_ref[...] = w3_ref[...].astype(BF16)
    else:
        ho_ref, st_ref = rest

    @pl.when(pl.program_id(1) == 0)
    def _():
        st_ref[...] = jnp.zeros_like(st_ref)

    sink_ref, q_ref, kv_ref, kvp_ref, bias_ref = att_in
    hqvg_ref, hf_ref, lbl_ref, ng_ref = hg_in
    kv_cols = [pl.ds(i * ATT_KV_W, ATT_KV_W) for i in range(4)]
    atts, hgos = [], []
    for sub in range(MIXER_BLOCKS):
        rows = pl.ds(sub * ATT_BLOCK, ATT_BLOCK)
        cur = [kv_ref.at[rows, cols] for cols in kv_cols]
        if sub == 0:
            prev = [kvp_ref.at[pl.ds(0, ATT_BLOCK), cols] for cols in kv_cols]
            first = pl.program_id(1) == 0
        else:
            before = pl.ds((sub - 1) * ATT_BLOCK, ATT_BLOCK)
            prev = [kv_ref.at[before, cols] for cols in kv_cols]
            first = False
        att = _attn_block(sink_ref, q_ref.at[rows], cur[0], prev[0], cur[1], prev[1],
                          cur[2], prev[2], cur[3], prev[3], bias_ref, first)
        hq, hi, hgt = (hqvg_ref.at[rows, pl.ds(i * HG_K_W, HG_K_W)] for i in range(3))
        hgo = _hgrn_tile(hq, hf_ref.at[rows], hi, hgt, lbl_ref, ng_ref, st_ref, layer)
        atts.append(att.astype(BF16))
        hgos.append(hgo.astype(BF16))
    ya = _dot(jnp.concatenate(atts, 0), wa_ref[...])
    yb = _dot(jnp.concatenate(hgos, 0), wb_ref[...])
    ga = gab_ref[:, pl.ds(0, D_MODEL)]
    gb = gab_ref[:, pl.ds(D_MODEL, D_MODEL)]
    merged = _sigmoid(ga).astype(F32) * ya + _sigmoid(gb).astype(F32) * yb
    hn = h_ref[...] + _dot(merged.astype(BF16), wo_ref[...])
    ho_ref[...] = hn
    if moe:
        u = _rms(hn, gn_ref[...])
        bits = pltpu.bitcast(u.astype(BF16).astype(F32), jnp.uint32)
        u_ref[...] = (bits[:, :D_MODEL // 2] >> 16) | (bits[:, D_MODEL // 2:] & jnp.uint32(0xFFFF0000))
        idx_ref[...], gate_ref[...] = _route(u, rwh_ref, rwl_ref)


def _mixer(proj, h, sinks, band_bias, lb_logits, norm_g, wa, wb, wo, layer, moe_in=None):
    aq, kv, hqvg, hf, gab = proj
    moe = moe_in is not None
    nb = SEQ // MIXER_TILE
    step = lambda b, n: b * nb + n
    cur = lambda b, n: (step(b, n), 0)
    prev = lambda b, n: (jnp.maximum(step(b, n) * MIXER_BLOCKS - 1, 0), 0)
    blk = lambda width: pl.BlockSpec((MIXER_TILE, width), cur)
    in_specs = [
        pl.BlockSpec(memory_space=pltpu.SMEM), blk(PROJ_WIDTHS[PROJ_AQ]),
        blk(PROJ_WIDTHS[PROJ_KV]), pl.BlockSpec((ATT_BLOCK, PROJ_WIDTHS[PROJ_KV]), prev),
        _resident((ATT_Q_HEADS, ATT_BLOCK, 2 * ATT_BLOCK)),
        blk(PROJ_WIDTHS[PROJ_HG]), blk(PROJ_WIDTHS[PROJ_HF]), _resident((DEPTH, HG_K_W)), _resident((1, HG_DV)),
        blk(PROJ_WIDTHS[PROJ_GATES]), blk(D_MODEL),
        _resident((ATT_Q_W, D_MODEL)), _resident((HG_V_W, D_MODEL)), _resident((D_MODEL, D_MODEL)),
    ]
    args = [sinks, aq, kv, kv, band_bias, hqvg, hf, lb_logits, norm_g, gab, h, wa, wb, wo]
    assert len(in_specs) == N_ATT_IN + N_HGRN_IN + N_MERGE_IN
    out_specs = [blk(D_MODEL)]
    out_shape = [jax.ShapeDtypeStruct((TOKENS, D_MODEL), F32)]
    if moe:
        gn, rw_hi, rw_lo, w1_all, w3_all, first_block = moe_in
        in_specs += [_resident((1, D_MODEL)), _resident((D_MODEL, ROUTER_LANES)),
                     _resident((D_MODEL, ROUTER_LANES))]
        args += [gn, rw_hi, rw_lo]
        out_specs += [blk(D_MODEL // 2), blk(TOP_K), blk(TOP_K)]
        out_shape += [jax.ShapeDtypeStruct((TOKENS, D_MODEL // 2), jnp.uint32),
                      jax.ShapeDtypeStruct((TOKENS, TOP_K), jnp.int32),
                      jax.ShapeDtypeStruct((TOKENS, TOP_K), F32)]
        for w_all in (w1_all, w3_all):
            r_in, r_out, r_shape = _cast_rider(w_all, first_block, W13_ROWS, MIXER_STEPS, step)
            in_specs.append(r_in)
            args.append(w_all)
            out_specs.append(r_out)
            out_shape.append(r_shape)
    return pl.pallas_call(
        functools.partial(_mixer_kernel, layer=layer, moe=moe),
        grid=(BATCH, nb),
        in_specs=in_specs,
        out_specs=out_specs,
        out_shape=out_shape,
        scratch_shapes=[pltpu.VMEM((HG_HEADS, HG_DV, HG_DK), F32)],
        compiler_params=_cparams(("arbitrary", "arbitrary")),
        name="mixer_moe" if moe else "mixer",
    )(*args)


def _swiglu_part(x, w1, w3, w2):
    a = _dot(x, w1)
    b = _dot(x, w3)
    return _dot((a * _sigmoid(a) * b).astype(BF16), w2)


def _dense_ffn_kernel(h_ref, g_ref, w1_ref, w3_ref, w2_ref, o_ref):
    h = h_ref[...]
    x = _rms(h, g_ref[...]).astype(BF16)
    acc = None
    for lo, hi in DENSE_F_GROUPS:
        part = _swiglu_part(x, w1_ref[:, lo:hi], w3_ref[:, lo:hi], w2_ref[lo:hi, :])
        acc = part if acc is None else acc + part
    o_ref[...] = h + acc


def _dense_ffn(h, g, w1, w3, w2):
    once = lambda shape: pl.BlockSpec(shape, lambda i: (0, 0), pipeline_mode=pl.Buffered(1))
    return pl.pallas_call(
        _dense_ffn_kernel,
        grid=(TOKENS // ROW_TILE,),
        in_specs=[
            pl.BlockSpec((ROW_TILE, D_MODEL), lambda i: (i, 0)),
            _resident((1, D_MODEL)),
            once((D_MODEL, FFN_DENSE)), once((D_MODEL, FFN_DENSE)), once((FFN_DENSE, D_MODEL)),
        ],
        out_specs=pl.BlockSpec((ROW_TILE, D_MODEL), lambda i: (i, 0)),
        out_shape=jax.ShapeDtypeStruct((TOKENS, D_MODEL), F32),
        compiler_params=_cparams(("arbitrary",)),
        name="dense_ffn",
    )(h, g, w1, w3, w2)


def _dispatch_kernel(pos_ref, u_ref, w2_ref, xs_in_ref, xs_ref, w2o_ref, sem):
    del xs_in_ref
    tok = DISPATCH_TILE // DISPATCH_PARTS
    r2 = W2_ROWS // DISPATCH_PARTS
    for part in range(DISPATCH_PARTS):
        s2 = slice(part * r2, (part + 1) * r2)
        w2o_ref[s2, :] = w2_ref[s2, :].astype(BF16)
        for r in range(part * tok, (part + 1) * tok):
            for k in range(TOP_K):
                d = pos_ref[0, 0, TOP_K * r + k]
                pltpu.make_async_copy(u_ref.at[pl.ds(r, 1)], xs_ref.at[pl.ds(d, 1)], sem).start(priority=k)

    for k in range(TOP_K):
        pltpu.make_async_copy(u_ref, xs_ref.at[pl.ds(0, DISPATCH_TILE)], sem).wait()


def _dispatch(u_packed, pos, w2_all, moe_layer):
    n = DISPATCH_STEPS
    xs0 = jnp.zeros((MOE_ROWS, D_MODEL // 2), jnp.uint32)
    w2_in, w2_out, w2_shape = _cast_rider(w2_all, moe_layer * n, W2_ROWS, n, lambda i: i)
    return pl.pallas_call(
        _dispatch_kernel,
        grid=(n,),
        in_specs=[
            pl.BlockSpec((1, 1, TOP_K * DISPATCH_TILE), lambda i: (i, 0, 0), memory_space=pltpu.SMEM),
            pl.BlockSpec((DISPATCH_TILE, D_MODEL // 2), lambda i: (i, 0)),
            w2_in,
            pl.BlockSpec(memory_space=pl.ANY),
        ],
        out_specs=[pl.BlockSpec(memory_space=pl.ANY), w2_out],
        out_shape=[jax.ShapeDtypeStruct((MOE_ROWS, D_MODEL // 2), jnp.uint32), w2_shape],
        scratch_shapes=[pltpu.SemaphoreType.DMA(())],
        input_output_aliases={3: 0},
        compiler_params=_cparams(("arbitrary",)),
        name="moe_dispatch",
    )(pos.reshape(n, 1, TOP_K * DISPATCH_TILE), u_packed, w2_all, xs0)


def _moe_kernel(te_ref, nu_ref, x_ref, w1_ref, w3_ref, w2_ref, y_ref, xb, acc):
    i = pl.program_id(0)
    f = pl.program_id(1)
    valid = i < nu_ref[0]

    @pl.when((f == 0) & valid)
    def _():
        w = x_ref[...]
        x = jnp.concatenate([pltpu.bitcast(w << 16, F32).astype(BF16),
                             pltpu.bitcast(w & jnp.uint32(0xFFFF0000), F32).astype(BF16)], axis=1)
        xb[...] = x
        acc[...] = _swiglu_part(x, w1_ref[0], w3_ref[0], w2_ref[0])

    @pl.when((f == 1) & valid)
    def _():
        y_ref[...] = acc[...] + _swiglu_part(xb[...], w1_ref[0], w3_ref[0], w2_ref[0])

    @pl.when((f == 1) & jnp.logical_not(valid))
    def _():
        y_ref[...] = jnp.zeros_like(y_ref)


def _moe_experts(xs, tile_expert, n_used, w1, w3, w2):
    row_blk = lambda i, f, te, nu: (jnp.minimum(i, nu[0] - 1), 0)
    f_blk = lambda i, f, nu: jnp.where(i < nu[0], f, MOE_NF - 1)
    grid_spec = pltpu.PrefetchScalarGridSpec(
        num_scalar_prefetch=2,
        grid=(MOE_TILES, MOE_NF),
        in_specs=[
            pl.BlockSpec((MOE_ROW_TILE, D_MODEL // 2), row_blk),
            pl.BlockSpec((1, D_MODEL, MOE_F_TILE), lambda i, f, te, nu: (te[i], 0, f_blk(i, f, nu))),
            pl.BlockSpec((1, D_MODEL, MOE_F_TILE), lambda i, f, te, nu: (te[i], 0, f_blk(i, f, nu))),
            pl.BlockSpec((1, MOE_F_TILE, D_MODEL), lambda i, f, te, nu: (te[i], f_blk(i, f, nu), 0)),
        ],
        out_specs=pl.BlockSpec((MOE_ROW_TILE, D_MODEL), lambda i, f, te, nu: (i, 0)),
        scratch_shapes=[
            pltpu.VMEM((MOE_ROW_TILE, D_MODEL), BF16),
            pltpu.VMEM((MOE_ROW_TILE, D_MODEL), F32),
        ],
    )
    return pl.pallas_call(
        _moe_kernel,
        grid_spec=grid_spec,
        out_shape=jax.ShapeDtypeStruct((MOE_ROWS, D_MODEL), F32),
        compiler_params=_cparams(("arbitrary", "arbitrary")),
        name="moe_experts",
    )(tile_expert, n_used, xs, w1, w3, w2)


def _moe_plan(idx):
    e_flat = idx.reshape(-1)
    onehot = (e_flat[:, None] == jnp.arange(N_EXPERTS, dtype=jnp.int32)[None, :]).astype(jnp.int32)
    csum = jnp.cumsum(onehot, axis=0)
    counts = csum[-1]
    rank = jnp.sum((csum - onehot) * onehot, axis=1)
    tiles_per = (counts + MOE_ROW_TILE - 1) // MOE_ROW_TILE
    tile_end = jnp.cumsum(tiles_per)
    tile_start = tile_end - tiles_per
    pos = jnp.sum(onehot * tile_start[None, :], axis=1) * MOE_ROW_TILE + rank
    tile_id = jnp.arange(MOE_TILES, dtype=jnp.int32)
    te = jnp.sum((tile_id[:, None] >= tile_end[None, :]).astype(jnp.int32), axis=1)
    n_used = tile_end[-1]
    te = jnp.minimum(te, N_EXPERTS - 1)
    te = jnp.where(tile_id < n_used, te, te[jnp.maximum(n_used - 1, 0)])
    return pos.astype(jnp.int32), te.astype(jnp.int32), n_used.reshape(1).astype(jnp.int32)


def _ple_tile(h, p, g_ref, wg_ref, wp_ref, fn_ref):
    u = _rms(h, g_ref[...]).astype(BF16)
    emb_gate = _sigmoid(_dot(u, wg_ref[...]))
    h = h + emb_gate * _dot(p.astype(BF16), wp_ref[...])
    if fn_ref is not None:
        h = _rms(h, fn_ref[...])
    return h


def _ple_kernel(h_ref, p_ref, g_ref, wg_ref, wp_ref, *rest, final):
    fn_ref, o_ref = rest if final else (None,) + rest
    o_ref[...] = _ple_tile(h_ref[...], p_ref[...], g_ref, wg_ref, wp_ref, fn_ref)


def _ple_moe_kernel(pos_ref, posn_ref, gate_ref, y_hbm, h_ref, p_ref, g_ref, wg_ref, wp_ref, *rest, final):
    if final:
        fn_ref, o_ref, yg, sems = rest
    else:
        fn_ref, (o_ref, yg, sems) = None, rest
    j = pl.program_id(0)

    def start_row(idx_ref, slot, r):
        for k in range(TOP_K):
            s = idx_ref[0, slot, TOP_K * r + k]
            pltpu.make_async_copy(y_hbm.at[pl.ds(s, 1)], yg.at[slot, k, pl.ds(r, 1)],
                                  sems.at[slot]).start(priority=k)

    def wait_rows(slot):
        for k in range(TOP_K):
            pltpu.make_async_copy(y_hbm.at[pl.ds(0, ROW_TILE)], yg.at[slot, k], sems.at[slot]).wait()

    @pl.when(j == 0)
    def _():
        for slot in range(2):
            def body(r, carry, slot=slot):
                start_row(pos_ref, slot, r)
                return carry
            lax.fori_loop(0, ROW_TILE, body, 0, unroll=8)

    for slot in range(2):
        rows = pl.ds(slot * ROW_TILE, ROW_TILE)
        wait_rows(slot)
        h = h_ref[rows, :]
        gate = gate_ref[rows, :]
        for k in range(TOP_K):
            h = h + gate[:, k:k + 1] * yg[slot, k]
        o_ref[rows, :] = _ple_tile(h, p_ref[rows, :], g_ref, wg_ref, wp_ref, fn_ref)
        for r in range(ROW_TILE):
            start_row(posn_ref, slot, r)

    @pl.when(j == pl.num_programs(0) - 1)
    def _():
        for slot in range(2):
            wait_rows(slot)


def _ple(h, moe_in, p, g, wg, wp, final_g):
    moe = moe_in is not None
    final = final_g is not None
    tile = 2 * ROW_TILE if moe else ROW_TILE
    n = TOKENS // tile
    row = lambda width: pl.BlockSpec((tile, width), lambda i: (i, 0))
    in_specs, args, scratch = [], [], []
    if moe:
        pos, gate, y = moe_in
        pos3 = pos.reshape(n, 2, TOP_K * ROW_TILE)
        smem = lambda imap: pl.BlockSpec((1, 2, TOP_K * ROW_TILE), imap, memory_space=pltpu.SMEM)
        in_specs += [smem(lambda i: (i, 0, 0)), smem(lambda i: (jnp.minimum(i + 1, n - 1), 0, 0)),
                     row(TOP_K), pl.BlockSpec(memory_space=pl.ANY)]
        args += [pos3, pos3, gate, y]
        scratch = [pltpu.VMEM((2, TOP_K, ROW_TILE, D_MODEL), F32), pltpu.SemaphoreType.DMA((2,))]
    in_specs += [row(D_MODEL), row(PLE_DIM), _resident((1, D_MODEL)), _resident((D_MODEL, D_MODEL)),
                 _resident((PLE_DIM, D_MODEL))]
    args += [h, p, g, wg, wp]
    if final:
        in_specs.append(_resident((1, D_MODEL)))
        args.append(final_g)
    return pl.pallas_call(
        functools.partial(_ple_moe_kernel if moe else _ple_kernel, final=final),
        grid=(n,),
        in_specs=in_specs,
        out_specs=row(D_MODEL),
        out_shape=jax.ShapeDtypeStruct((TOKENS, D_MODEL), F32),
        scratch_shapes=scratch,
        compiler_params=_cparams(("arbitrary",)),
        name="ple_moe" if moe else "ple",
    )(*args)


def _t5_bucket(dist):
    max_exact = REL_BUCKETS // 2
    d = jnp.maximum(dist, 0)
    large = max_exact + (jnp.log(jnp.maximum(d, 1).astype(jnp.float32) / max_exact)
                         / math.log(REL_MAX_DIST / max_exact)
                         * (REL_BUCKETS - max_exact)).astype(jnp.int32)
    large = jnp.minimum(large, REL_BUCKETS - 1)
    return jnp.where(d < max_exact, d, large)


def kernel(x, p, w_in, sinks, rel_bias, lb_logits, hgrn_norm, w_branch_a, w_branch_b, w_out,
           norm_mix, norm_ffn, norm_ple, dense_w1, dense_w3, dense_w2, router_w, moe_w1,
           moe_w3, moe_w2, ple_proj, ple_gate, final_norm):
    qi = jnp.arange(ATT_BLOCK)[:, None]
    kj = jnp.arange(2 * ATT_BLOCK)[None, :]
    bucket = _t5_bucket(qi + ATT_BLOCK - kj)[None]
    band_bias = jnp.zeros((ATT_Q_HEADS, ATT_BLOCK, 2 * ATT_BLOCK), F32)
    for b in range(REL_BUCKETS):
        band_bias = jnp.where(bucket == b, rel_bias[b].astype(F32)[:, None, None], band_bias)

    bf = lambda w: w.astype(BF16)
    h = x.reshape(TOKENS, D_MODEL)
    pt = p.reshape(DEPTH, TOKENS, PLE_DIM)
    vec = lambda g: g.reshape(1, -1)
    n_moe = moe_w1.shape[0]
    w1_all = moe_w1.reshape(n_moe * N_EXPERTS * D_MODEL, FFN_EXPERT)
    w3_all = moe_w3.reshape(n_moe * N_EXPERTS * D_MODEL, FFN_EXPERT)
    w2_all = moe_w2.reshape(n_moe * N_EXPERTS * FFN_EXPERT, D_MODEL)

    def stacked(w):
        return w.reshape(w.shape[0] * w.shape[1], w.shape[2])

    def layer_rider(w, index, rows):
        blocks = w.shape[1] // rows
        return stacked(w), index * blocks, rows, blocks

    w_in_next = bf(w_in[0])
    for l in range(DEPTH):
        moe = l % 2 == 1
        riders = [layer_rider(w_branch_a, l, 16), layer_rider(w_branch_b, l, 16),
                  layer_rider(w_out, l, 32), layer_rider(ple_gate, l, 32)]
        if not moe:
            riders += [layer_rider(dense_w1, l // 2, 32), layer_rider(dense_w3, l // 2, 32),
                       layer_rider(dense_w2, l // 2, 176)]
        if l + 1 < DEPTH:
            riders.append(layer_rider(w_in, l + 1, 32))
        proj, cast = _in_proj(h, vec(norm_mix[l]), w_in_next, riders)
        wa, wb, wo, wpg = cast[:4]
        if l + 1 < DEPTH:
            w_in_next = cast[-1]
        final_g = vec(final_norm) if l == DEPTH - 1 else None
        if not moe:
            (h,) = _mixer(proj, h, sinks[l], band_bias, lb_logits, vec(hgrn_norm[l]), wa, wb, wo, l)
            h = _dense_ffn(h, vec(norm_ffn[l]), cast[4], cast[5], cast[6])
            y = None
        else:
            rw = jnp.pad(router_w[l // 2], ((0, 0), (0, ROUTER_LANES - N_EXPERTS)))
            rw_hi = bf(rw)
            rw_lo = bf(rw - rw_hi.astype(F32))
            moe_in = (vec(norm_ffn[l]), rw_hi, rw_lo, w1_all, w3_all, (l // 2) * MIXER_STEPS)
            h, u, idx, gate, w1b, w3b = _mixer(proj, h, sinks[l], band_bias, lb_logits, vec(hgrn_norm[l]),
                                               wa, wb, wo, l, moe_in)
            pos, tile_expert, n_used = _moe_plan(idx)
            xs, w2b = _dispatch(u, pos, w2_all, l // 2)
            y = _moe_experts(xs, tile_expert, n_used,
                             w1b.reshape(N_EXPERTS, D_MODEL, FFN_EXPERT),
                             w3b.reshape(N_EXPERTS, D_MODEL, FFN_EXPERT),
                             w2b.reshape(N_EXPERTS, FFN_EXPERT, D_MODEL))
            y = (pos, gate, y)
        h = _ple(h, y, pt[l], vec(norm_ple[l]), wpg, bf(ple_proj[l]), final_g)
    return h.reshape(BATCH, SEQ, D_MODEL)
```
